```python
import math
import jax, jax.numpy as jnp
from jax import lax
import numpy as np

D_MODEL = 2048
BATCH = 8
SEQ = 8192
DEPTH = 2

N_HEADS = 16
HEAD_DIM = D_MODEL // N_HEADS
D_FF = 4 * D_MODEL
CONV_WIDTH = 31
DILATED_BRANCHES = ((128, 1), (512, 4), (2048, 16))
BAND = 128
REL_BUCKETS = 32
REL_MAX_DIST = 2048
N_A = DEPTH // 2
N_B = DEPTH - N_A
ALPHA = (2 * DEPTH) ** 0.25
BETA = (8 * DEPTH) ** -0.25
LN_EPS = 1e-5

kernel_name = "yoco_conformer_dilated_hybrid"


def layer_norm(x, g, b):
    xf = x.astype(jnp.float32)
    mu = jnp.mean(xf, axis=-1, keepdims=True)
    var = jnp.mean(jnp.square(xf - mu), axis=-1, keepdims=True)
    y = (xf - mu) * lax.rsqrt(var + LN_EPS) * g.astype(jnp.float32) + b.astype(jnp.float32)
    return y.astype(x.dtype)


def conv_module(x, pw1_w, pw1_b, dw_w, dw_b, ln_g, ln_b, pw2_w, pw2_b):
    h = x @ pw1_w + pw1_b
    a, gate = jnp.split(h, 2, axis=-1)
    h = a * jax.nn.sigmoid(gate)
    h = lax.conv_general_dilated(
        h, dw_w[:, None, :].astype(h.dtype), window_strides=(1,),
        padding=[(CONV_WIDTH - 1, 0)],
        dimension_numbers=('NWC', 'WIO', 'NWC'),
        feature_group_count=D_MODEL) + dw_b
    h = jax.nn.silu(layer_norm(h, ln_g, ln_b))
    return h @ pw2_w + pw2_b


def sq_relu_mlp(x, w1, w2):
    return jnp.square(jax.nn.relu(x @ w1)) @ w2


def t5_bucket(dist):
    max_exact = REL_BUCKETS // 2
    large = max_exact + (np.log(np.maximum(dist, 1) / max_exact)
                         / math.log(REL_MAX_DIST / max_exact)
                         * (REL_BUCKETS - max_exact)).astype(np.int32)
    large = np.minimum(large, REL_BUCKETS - 1)
    return np.where(dist < max_exact, dist, large).astype(np.int32)


def dilated_branch(q, k, v, rel_bias, window, dil):
    bsz, seq, nh, dh = q.shape
    n_keys = window // dil
    L = seq // dil
    nb = -(-L // BAND)
    Lp = nb * BAND

    def by_residue(t):
        return t.reshape(bsz, L, dil, nh, dh).transpose(0, 2, 3, 1, 4)

    qd = jnp.pad(by_residue(q), ((0, 0), (0, 0), (0, 0), (0, Lp - L), (0, 0)))
    kd = jnp.pad(by_residue(k), ((0, 0), (0, 0), (0, 0), (BAND, Lp - L), (0, 0)))
    vd = jnp.pad(by_residue(v), ((0, 0), (0, 0), (0, 0), (BAND, Lp - L), (0, 0)))
    qb = qd.reshape(bsz, dil, nh, nb, BAND, dh)
    kb = kd.reshape(bsz, dil, nh, nb + 1, BAND, dh)
    vb = vd.reshape(bsz, dil, nh, nb + 1, BAND, dh)
    kc = jnp.concatenate([kb[:, :, :, :-1], kb[:, :, :, 1:]], axis=4)
    vc = jnp.concatenate([vb[:, :, :, :-1], vb[:, :, :, 1:]], axis=4)

    i = np.arange(BAND)[:, None]
    j = np.arange(2 * BAND)[None, :]
    delta = i - j + BAND
    band_ok = (delta >= 0) & (delta <= n_keys)
    blk = np.arange(nb)[:, None, None]
    valid = band_ok[None] & ~((blk == 0) & (j[None] < BAND))
    bucket = t5_bucket(np.clip(delta, 0, None) * dil)
    bias = jnp.transpose(rel_bias[bucket], (2, 0, 1)).astype(jnp.float32)

    s = jnp.einsum('bdhnqe,bdhnke->bdhnqk', qb, kc) * (dh ** -0.5)
    s = s + bias[None, None, :, None]
    s = jnp.where(jnp.asarray(valid)[None, None, None], s, -jnp.inf)
    m = jnp.max(s, axis=-1, keepdims=True)
    p = jnp.exp(s - m)
    den = jnp.sum(p, axis=-1, keepdims=True)
    o = jnp.einsum('bdhnqk,bdhnke->bdhnqe', p, vc) / den
    lse = (m + jnp.log(den))[..., 0]

    o = o.reshape(bsz, dil, nh, Lp, dh)[:, :, :, :L].transpose(0, 3, 1, 2, 4).reshape(bsz, seq, nh, dh)
    lse = lse.reshape(bsz, dil, nh, Lp)[:, :, :, :L].transpose(0, 3, 1, 2).reshape(bsz, seq, nh)
    return o, lse


def dilated_attention(x, k_sh, v_sh, wq, wo, rel_bias):
    bsz, seq, _ = x.shape
    q = (x @ wq).reshape(bsz, seq, N_HEADS, HEAD_DIM).astype(jnp.float32)
    outs, lses = [], []
    for window, dil in DILATED_BRANCHES:
        o, lse = dilated_branch(q, k_sh, v_sh, rel_bias, window, dil)
        outs.append(o)
        lses.append(lse)
    w = jax.nn.softmax(jnp.stack(lses, axis=0), axis=0)
    o = jnp.sum(w[..., None] * jnp.stack(outs, axis=0), axis=0)
    return o.reshape(bsz, seq, N_HEADS * HEAD_DIM).astype(x.dtype) @ wo


def _fwd_setup_inputs(seed: int = 0) -> dict:
    key = jax.random.key(seed)
    ks = jax.random.split(key, 20)
    D = D_MODEL
    HD = N_HEADS * HEAD_DIM

    def nrm(k, shape, scale):
        return jax.random.normal(k, shape, jnp.float32) * scale

    w_k = nrm(ks[9], (D, HD), D ** -0.5)
    w_v = nrm(ks[10], (D, HD), D ** -0.5 * BETA)
    return {
        "x": nrm(ks[0], (BATCH, SEQ, D), 1.0),
        "conv_pw1_w": nrm(ks[1], (N_A, D, 2 * D), D ** -0.5),
        "conv_pw1_b": nrm(ks[2], (N_A, 2 * D), 0.02),
        "conv_dw_w": nrm(ks[3], (N_A, CONV_WIDTH, D), CONV_WIDTH ** -0.5),
        "conv_dw_b": nrm(ks[4], (N_A, D), 0.02),
        "conv_ln_g": 1.0 + nrm(ks[5], (N_A, D), 0.02),
        "conv_ln_b": nrm(ks[6], (N_A, D), 0.02),
        "conv_pw2_w": nrm(ks[7], (N_A, D, D), D ** -0.5 * BETA),
        "conv_pw2_b": nrm(ks[8], (N_A, D), 0.02),
        "w_kv": jnp.concatenate([w_k, w_v], axis=1),
        "attn_wq": nrm(ks[11], (N_B, D, HD), D ** -0.5),
        "attn_wo": nrm(ks[12], (N_B, HD, D), HD ** -0.5 * BETA),
        "rel_bias": nrm(ks[13], (REL_BUCKETS, N_HEADS), 0.2),
        "mlp_w1": nrm(ks[14], (DEPTH, D, D_FF), D ** -0.5 * BETA),
        "mlp_w2": nrm(ks[15], (DEPTH, D_FF, D), D_FF ** -0.5 * BETA),
        "ln_mix_g": 1.0 + nrm(ks[16], (DEPTH, D), 0.02),
        "ln_mix_b": nrm(ks[17], (DEPTH, D), 0.02),
        "ln_mlp_g": 1.0 + nrm(ks[18], (DEPTH, D), 0.02),
        "ln_mlp_b": nrm(ks[19], (DEPTH, D), 0.02),
    }


def _fwd_reference(x, conv_pw1_w, conv_pw1_b, conv_dw_w, conv_dw_b, conv_ln_g, conv_ln_b,
              conv_pw2_w, conv_pw2_b, w_kv, attn_wq, attn_wo, rel_bias,
              mlp_w1, mlp_w2, ln_mix_g, ln_mix_b, ln_mlp_g, ln_mlp_b):
    bsz, seq, _ = x.shape
    k_sh = None
    v_sh = None
    for layer in range(DEPTH):
        if layer < N_A:
            i = layer
            mix = conv_module(x, conv_pw1_w[i], conv_pw1_b[i], conv_dw_w[i], conv_dw_b[i],
                              conv_ln_g[i], conv_ln_b[i], conv_pw2_w[i], conv_pw2_b[i])
        else:
            if layer == N_A:
                kv = (x @ w_kv).astype(jnp.float32)
                k_sh, v_sh = jnp.split(kv, 2, axis=-1)
                k_sh = k_sh.reshape(bsz, seq, N_HEADS, HEAD_DIM)
                v_sh = v_sh.reshape(bsz, seq, N_HEADS, HEAD_DIM)
            j = layer - N_A
            mix = dilated_attention(x, k_sh, v_sh, attn_wq[j], attn_wo[j], rel_bias)
        x = layer_norm(ALPHA * x + mix, ln_mix_g[layer], ln_mix_b[layer])
        x = layer_norm(ALPHA * x + sq_relu_mlp(x, mlp_w1[layer], mlp_w2[layer]),
                       ln_mlp_g[layer], ln_mlp_b[layer])
    return x


import jax as _jax
import jax.numpy as _jnp

TWIN_FORMAT = 'train_step'
FWD_PARAMS = ['x', 'conv_pw1_w', 'conv_pw1_b', 'conv_dw_w', 'conv_dw_b', 'conv_ln_g', 'conv_ln_b', 'conv_pw2_w', 'conv_pw2_b', 'w_kv', 'attn_wq', 'attn_wo', 'rel_bias', 'mlp_w1', 'mlp_w2', 'ln_mix_g', 'ln_mix_b', 'ln_mlp_g', 'ln_mlp_b']
TWIN_WEIGHTS = ['conv_pw1_w', 'conv_pw1_b', 'conv_dw_w', 'conv_dw_b', 'conv_ln_g', 'conv_ln_b', 'conv_pw2_w', 'conv_pw2_b', 'w_kv', 'attn_wq', 'attn_wo', 'rel_bias', 'mlp_w1', 'mlp_w2', 'ln_mix_g', 'ln_mix_b', 'ln_mlp_g', 'ln_mlp_b']
TWIN_DIFF_INPUT = 'x'
TWIN_INPUTS = ['x', 'conv_pw1_w', 'conv_pw1_b', 'conv_dw_w', 'conv_dw_b', 'conv_ln_g', 'conv_ln_b', 'conv_pw2_w', 'conv_pw2_b', 'w_kv', 'attn_wq', 'attn_wo', 'rel_bias', 'mlp_w1', 'mlp_w2', 'ln_mix_g', 'ln_mix_b', 'ln_mlp_g', 'ln_mlp_b', 'loss_target', 'm_conv_pw1_w', 'm_conv_pw1_b', 'm_conv_dw_w', 'm_conv_dw_b', 'm_conv_ln_g', 'm_conv_ln_b', 'm_conv_pw2_w', 'm_conv_pw2_b', 'm_w_kv', 'm_attn_wq', 'm_attn_wo', 'm_rel_bias', 'm_mlp_w1', 'm_mlp_w2', 'm_ln_mix_g', 'm_ln_mix_b', 'm_ln_mlp_g', 'm_ln_mlp_b', 'v_conv_pw1_w', 'v_conv_pw1_b', 'v_conv_dw_w', 'v_conv_dw_b', 'v_conv_ln_g', 'v_conv_ln_b', 'v_conv_pw2_w', 'v_conv_pw2_b', 'v_w_kv', 'v_attn_wq', 'v_attn_wo', 'v_rel_bias', 'v_mlp_w1', 'v_mlp_w2', 'v_ln_mix_g', 'v_ln_mix_b', 'v_ln_mlp_g', 'v_ln_mlp_b']
TWIN_OUTPUTS = ['loss', 'grad_x', 'grad_conv_pw1_w', 'grad_conv_pw1_b', 'grad_conv_dw_w', 'grad_conv_dw_b', 'grad_conv_ln_g', 'grad_conv_ln_b', 'grad_conv_pw2_w', 'grad_conv_pw2_b', 'grad_w_kv', 'grad_attn_wq', 'grad_attn_wo', 'grad_rel_bias', 'grad_mlp_w1', 'grad_mlp_w2', 'grad_ln_mix_g', 'grad_ln_mix_b', 'grad_ln_mlp_g', 'grad_ln_mlp_b', 'delta_conv_pw1_w', 'delta_conv_pw1_b', 'delta_conv_dw_w', 'delta_conv_dw_b', 'delta_conv_ln_g', 'delta_conv_ln_b', 'delta_conv_pw2_w', 'delta_conv_pw2_b', 'delta_w_kv', 'delta_attn_wq', 'delta_attn_wo', 'delta_rel_bias', 'delta_mlp_w1', 'delta_mlp_w2', 'delta_ln_mix_g', 'delta_ln_mix_b', 'delta_ln_mlp_g', 'delta_ln_mlp_b', 'new_m_conv_pw1_w', 'new_m_conv_pw1_b', 'new_m_conv_dw_w', 'new_m_conv_dw_b', 'new_m_conv_ln_g', 'new_m_conv_ln_b', 'new_m_conv_pw2_w', 'new_m_conv_pw2_b', 'new_m_w_kv', 'new_m_attn_wq', 'new_m_attn_wo', 'new_m_rel_bias', 'new_m_mlp_w1', 'new_m_mlp_w2', 'new_m_ln_mix_g', 'new_m_ln_mix_b', 'new_m_ln_mlp_g', 'new_m_ln_mlp_b', 'new_v_conv_pw1_w', 'new_v_conv_pw1_b', 'new_v_conv_dw_w', 'new_v_conv_dw_b', 'new_v_conv_ln_g', 'new_v_conv_ln_b', 'new_v_conv_pw2_w', 'new_v_conv_pw2_b', 'new_v_w_kv', 'new_v_attn_wq', 'new_v_attn_wo', 'new_v_rel_bias', 'new_v_mlp_w1', 'new_v_mlp_w2', 'new_v_ln_mix_g', 'new_v_ln_mix_b', 'new_v_ln_mlp_g', 'new_v_ln_mlp_b']
TWIN_LEAF_KINDS = {'loss': 'loss', 'grad_x': 'grad_x', 'grad_conv_pw1_w': 'grad_w', 'grad_conv_pw1_b': 'grad_w', 'grad_conv_dw_w': 'grad_w', 'grad_conv_dw_b': 'grad_w', 'grad_conv_ln_g': 'grad_w', 'grad_conv_ln_b': 'grad_w', 'grad_conv_pw2_w': 'grad_w', 'grad_conv_pw2_b': 'grad_w', 'grad_w_kv': 'grad_w', 'grad_attn_wq': 'grad_w', 'grad_attn_wo': 'grad_w', 'grad_rel_bias': 'grad_w', 'grad_mlp_w1': 'grad_w', 'grad_mlp_w2': 'grad_w', 'grad_ln_mix_g': 'grad_w', 'grad_ln_mix_b': 'grad_w', 'grad_ln_mlp_g': 'grad_w', 'grad_ln_mlp_b': 'grad_w', 'delta_conv_pw1_w': 'delta_w', 'delta_conv_pw1_b': 'delta_w', 'delta_conv_dw_w': 'delta_w', 'delta_conv_dw_b': 'delta_w', 'delta_conv_ln_g': 'delta_w', 'delta_conv_ln_b': 'delta_w', 'delta_conv_pw2_w': 'delta_w', 'delta_conv_pw2_b': 'delta_w', 'delta_w_kv': 'delta_w', 'delta_attn_wq': 'delta_w', 'delta_attn_wo': 'delta_w', 'delta_rel_bias': 'delta_w', 'delta_mlp_w1': 'delta_w', 'delta_mlp_w2': 'delta_w', 'delta_ln_mix_g': 'delta_w', 'delta_ln_mix_b': 'delta_w', 'delta_ln_mlp_g': 'delta_w', 'delta_ln_mlp_b': 'delta_w', 'new_m_conv_pw1_w': 'new_m', 'new_m_conv_pw1_b': 'new_m', 'new_m_conv_dw_w': 'new_m', 'new_m_conv_dw_b': 'new_m', 'new_m_conv_ln_g': 'new_m', 'new_m_conv_ln_b': 'new_m', 'new_m_conv_pw2_w': 'new_m', 'new_m_conv_pw2_b': 'new_m', 'new_m_w_kv': 'new_m', 'new_m_attn_wq': 'new_m', 'new_m_attn_wo': 'new_m', 'new_m_rel_bias': 'new_m', 'new_m_mlp_w1': 'new_m', 'new_m_mlp_w2': 'new_m', 'new_m_ln_mix_g': 'new_m', 'new_m_ln_mix_b': 'new_m', 'new_m_ln_mlp_g': 'new_m', 'new_m_ln_mlp_b': 'new_m', 'new_v_conv_pw1_w': 'new_v', 'new_v_conv_pw1_b': 'new_v', 'new_v_conv_dw_w': 'new_v', 'new_v_conv_dw_b': 'new_v', 'new_v_conv_ln_g': 'new_v', 'new_v_conv_ln_b': 'new_v', 'new_v_conv_pw2_w': 'new_v', 'new_v_conv_pw2_b': 'new_v', 'new_v_w_kv': 'new_v', 'new_v_attn_wq': 'new_v', 'new_v_attn_wo': 'new_v', 'new_v_rel_bias': 'new_v', 'new_v_mlp_w1': 'new_v', 'new_v_mlp_w2': 'new_v', 'new_v_ln_mix_g': 'new_v', 'new_v_ln_mix_b': 'new_v', 'new_v_ln_mlp_g': 'new_v', 'new_v_ln_mlp_b': 'new_v'}


def _forward(args):
    return _fwd_reference(*[args[k] for k in FWD_PARAMS])


def _output_shape():
    def fwd():
        inp = _fwd_setup_inputs(0)
        return _fwd_reference(*[inp[k] for k in FWD_PARAMS])
    out = _jax.eval_shape(fwd)
    return out.shape, out.dtype

N_MICROBATCH = 1
ADAM_LR = 0.001
ADAM_B1 = 0.9
ADAM_B2 = 0.999
ADAM_EPS = 1e-08
ADAM_WD = 0.01
ADAM_STEP = 10
PER_EXAMPLE_BATCH_AXIS = {'x': 0, 'loss_target': 0}
SHARED_INPUTS = []
_WEIGHT_DTYPES = {'conv_pw1_w': _jnp.float32, 'conv_pw1_b': _jnp.float32, 'conv_dw_w': _jnp.float32, 'conv_dw_b': _jnp.float32, 'conv_ln_g': _jnp.float32, 'conv_ln_b': _jnp.float32, 'conv_pw2_w': _jnp.float32, 'conv_pw2_b': _jnp.float32, 'w_kv': _jnp.float32, 'attn_wq': _jnp.float32, 'attn_wo': _jnp.float32, 'rel_bias': _jnp.float32, 'mlp_w1': _jnp.float32, 'mlp_w2': _jnp.float32, 'ln_mix_g': _jnp.float32, 'ln_mix_b': _jnp.float32, 'ln_mlp_g': _jnp.float32, 'ln_mlp_b': _jnp.float32}
MOMENT_SCALE = {'conv_pw1_w': 2.049717e-02, 'conv_pw1_b': 5.524299e-02, 'conv_dw_w': 2.812510e-02, 'conv_dw_b': 1.427303e-01, 'conv_ln_g': 6.266385e-02, 'conv_ln_b': 8.778117e-02, 'conv_pw2_w': 7.505244e-02, 'conv_pw2_b': 3.519246e-01, 'w_kv': 8.780192e-03, 'attn_wq': 4.165650e-03, 'attn_wo': 1.173241e-02, 'rel_bias': 7.981934e-03, 'mlp_w1': 1.574380e-02, 'mlp_w2': 4.345915e-02, 'ln_mix_g': 1.146655e+00, 'ln_mix_b': 5.055541e-01, 'ln_mlp_g': 2.267266e+01, 'ln_mlp_b': 2.209567e+00}


def _to_microbatches(a, axis):
    t = _jnp.moveaxis(a, axis, 0)
    t = t.reshape((N_MICROBATCH, t.shape[0] // N_MICROBATCH) + t.shape[1:])
    return _jnp.moveaxis(t, 1, axis + 1)


def setup_inputs(seed: int = 0) -> dict:
    inp = _fwd_setup_inputs(seed)
    key = _jax.random.fold_in(_jax.random.key(seed), 7919)
    shape, _ = _output_shape()
    out = dict(inp)
    out["loss_target"] = _jax.random.normal(_jax.random.fold_in(key, 0), shape, _jnp.float32)
    for i, name in enumerate(TWIN_WEIGHTS):
        w = inp[name].astype(_jnp.float32)
        if MOMENT_SCALE is None:
            s = _jnp.sqrt(_jnp.mean(_jnp.square(w)) + 1e-30)
        else:
            s = MOMENT_SCALE[name]
        km, kv = _jax.random.split(_jax.random.fold_in(key, i + 1))
        out[name] = w
        out["m_" + name] = s * _jax.random.normal(km, w.shape, _jnp.float32)
        out["v_" + name] = (s * s) * _jax.random.uniform(kv, w.shape, _jnp.float32, 0.5, 1.5)
    if N_MICROBATCH > 1:
        for name, axis in PER_EXAMPLE_BATCH_AXIS.items():
            out[name] = _to_microbatches(out[name], axis)
    return {'x': out['x'], 'conv_pw1_w': out['conv_pw1_w'], 'conv_pw1_b': out['conv_pw1_b'], 'conv_dw_w': out['conv_dw_w'], 'conv_dw_b': out['conv_dw_b'], 'conv_ln_g': out['conv_ln_g'], 'conv_ln_b': out['conv_ln_b'], 'conv_pw2_w': out['conv_pw2_w'], 'conv_pw2_b': out['conv_pw2_b'], 'w_kv': out['w_kv'], 'attn_wq': out['attn_wq'], 'attn_wo': out['attn_wo'], 'rel_bias': out['rel_bias'], 'mlp_w1': out['mlp_w1'], 'mlp_w2': out['mlp_w2'], 'ln_mix_g': out['ln_mix_g'], 'ln_mix_b': out['ln_mix_b'], 'ln_mlp_g': out['ln_mlp_g'], 'ln_mlp_b': out['ln_mlp_b'], 'loss_target': out['loss_target'], 'm_conv_pw1_w': out['m_conv_pw1_w'], 'm_conv_pw1_b': out['m_conv_pw1_b'], 'm_conv_dw_w': out['m_conv_dw_w'], 'm_conv_dw_b': out['m_conv_dw_b'], 'm_conv_ln_g': out['m_conv_ln_g'], 'm_conv_ln_b': out['m_conv_ln_b'], 'm_conv_pw2_w': out['m_conv_pw2_w'], 'm_conv_pw2_b': out['m_conv_pw2_b'], 'm_w_kv': out['m_w_kv'], 'm_attn_wq': out['m_attn_wq'], 'm_attn_wo': out['m_attn_wo'], 'm_rel_bias': out['m_rel_bias'], 'm_mlp_w1': out['m_mlp_w1'], 'm_mlp_w2': out['m_mlp_w2'], 'm_ln_mix_g': out['m_ln_mix_g'], 'm_ln_mix_b': out['m_ln_mix_b'], 'm_ln_mlp_g': out['m_ln_mlp_g'], 'm_ln_mlp_b': out['m_ln_mlp_b'], 'v_conv_pw1_w': out['v_conv_pw1_w'], 'v_conv_pw1_b': out['v_conv_pw1_b'], 'v_conv_dw_w': out['v_conv_dw_w'], 'v_conv_dw_b': out['v_conv_dw_b'], 'v_conv_ln_g': out['v_conv_ln_g'], 'v_conv_ln_b': out['v_conv_ln_b'], 'v_conv_pw2_w': out['v_conv_pw2_w'], 'v_conv_pw2_b': out['v_conv_pw2_b'], 'v_w_kv': out['v_w_kv'], 'v_attn_wq': out['v_attn_wq'], 'v_attn_wo': out['v_attn_wo'], 'v_rel_bias': out['v_rel_bias'], 'v_mlp_w1': out['v_mlp_w1'], 'v_mlp_w2': out['v_mlp_w2'], 'v_ln_mix_g': out['v_ln_mix_g'], 'v_ln_mix_b': out['v_ln_mix_b'], 'v_ln_mlp_g': out['v_ln_mlp_g'], 'v_ln_mlp_b': out['v_ln_mlp_b']}


def _loss(weights, diff, rest, loss_target):
    with _jax.named_scope("forward"):
        args = {**rest, TWIN_DIFF_INPUT: diff, **{k: w.astype(_WEIGHT_DTYPES[k]) for k, w in weights.items()}}
        y = _forward(args)
    with _jax.named_scope("loss_head"):
        err = _jnp.square(y.astype(_jnp.float32) - loss_target)
        return 0.5 * _jnp.sum(_jnp.mean(err, axis=-1)) if err.ndim else 0.5 * err


def _adamw(w, g, m, v):
    m = ADAM_B1 * m + (1.0 - ADAM_B1) * g
    v = ADAM_B2 * v + (1.0 - ADAM_B2) * _jnp.square(g)
    m_hat = m / (1.0 - ADAM_B1 ** ADAM_STEP)
    v_hat = v / (1.0 - ADAM_B2 ** ADAM_STEP)
    delta = -ADAM_LR * (m_hat / (_jnp.sqrt(v_hat) + ADAM_EPS) + ADAM_WD * w)
    return delta, m, v


def reference(x, conv_pw1_w, conv_pw1_b, conv_dw_w, conv_dw_b, conv_ln_g, conv_ln_b, conv_pw2_w, conv_pw2_b, w_kv, attn_wq, attn_wo, rel_bias, mlp_w1, mlp_w2, ln_mix_g, ln_mix_b, ln_mlp_g, ln_mlp_b, loss_target, m_conv_pw1_w, m_conv_pw1_b, m_conv_dw_w, m_conv_dw_b, m_conv_ln_g, m_conv_ln_b, m_conv_pw2_w, m_conv_pw2_b, m_w_kv, m_attn_wq, m_attn_wo, m_rel_bias, m_mlp_w1, m_mlp_w2, m_ln_mix_g, m_ln_mix_b, m_ln_mlp_g, m_ln_mlp_b, v_conv_pw1_w, v_conv_pw1_b, v_conv_dw_w, v_conv_dw_b, v_conv_ln_g, v_conv_ln_b, v_conv_pw2_w, v_conv_pw2_b, v_w_kv, v_attn_wq, v_attn_wo, v_rel_bias, v_mlp_w1, v_mlp_w2, v_ln_mix_g, v_ln_mix_b, v_ln_mlp_g, v_ln_mlp_b):
    given = dict(x=x, conv_pw1_w=conv_pw1_w, conv_pw1_b=conv_pw1_b, conv_dw_w=conv_dw_w, conv_dw_b=conv_dw_b, conv_ln_g=conv_ln_g, conv_ln_b=conv_ln_b, conv_pw2_w=conv_pw2_w, conv_pw2_b=conv_pw2_b, w_kv=w_kv, attn_wq=attn_wq, attn_wo=attn_wo, rel_bias=rel_bias, mlp_w1=mlp_w1, mlp_w2=mlp_w2, ln_mix_g=ln_mix_g, ln_mix_b=ln_mix_b, ln_mlp_g=ln_mlp_g, ln_mlp_b=ln_mlp_b, loss_target=loss_target, m_conv_pw1_w=m_conv_pw1_w, m_conv_pw1_b=m_conv_pw1_b, m_conv_dw_w=m_conv_dw_w, m_conv_dw_b=m_conv_dw_b, m_conv_ln_g=m_conv_ln_g, m_conv_ln_b=m_conv_ln_b, m_conv_pw2_w=m_conv_pw2_w, m_conv_pw2_b=m_conv_pw2_b, m_w_kv=m_w_kv, m_attn_wq=m_attn_wq, m_attn_wo=m_attn_wo, m_rel_bias=m_rel_bias, m_mlp_w1=m_mlp_w1, m_mlp_w2=m_mlp_w2, m_ln_mix_g=m_ln_mix_g, m_ln_mix_b=m_ln_mix_b, m_ln_mlp_g=m_ln_mlp_g, m_ln_mlp_b=m_ln_mlp_b, v_conv_pw1_w=v_conv_pw1_w, v_conv_pw1_b=v_conv_pw1_b, v_conv_dw_w=v_conv_dw_w, v_conv_dw_b=v_conv_dw_b, v_conv_ln_g=v_conv_ln_g, v_conv_ln_b=v_conv_ln_b, v_conv_pw2_w=v_conv_pw2_w, v_conv_pw2_b=v_conv_pw2_b, v_w_kv=v_w_kv, v_attn_wq=v_attn_wq, v_attn_wo=v_attn_wo, v_rel_bias=v_rel_bias, v_mlp_w1=v_mlp_w1, v_mlp_w2=v_mlp_w2, v_ln_mix_g=v_ln_mix_g, v_ln_mix_b=v_ln_mix_b, v_ln_mlp_g=v_ln_mlp_g, v_ln_mlp_b=v_ln_mlp_b)
    weights = {n: given[n] for n in TWIN_WEIGHTS}
    shared = {n: given[n] for n in SHARED_INPUTS}
    per_example = {n: given[n] for n in ['x']}
    grad_fn = _jax.value_and_grad(_loss, argnums=(0, 1))

    def one_microbatch(ex, loss_target):
        ex = dict(ex)
        diff = ex.pop(TWIN_DIFF_INPUT)
        return grad_fn(weights, diff, {**shared, **ex}, loss_target)

    if N_MICROBATCH == 1:
        loss, (grad_w, grad_x) = one_microbatch(per_example, given["loss_target"])
    else:
        def body(carry, xs):
            loss_sum, grad_sum = carry
            l_k, (gw_k, gx_k) = one_microbatch(xs[0], xs[1])
            with _jax.named_scope("update"):
                return (loss_sum + l_k, _jax.tree.map(_jnp.add, grad_sum, gw_k)), gx_k

        init = (_jnp.zeros((), _jnp.float32), _jax.tree.map(_jnp.zeros_like, weights))
        (loss, grad_w), grad_x = _jax.lax.scan(body, init, (per_example, given["loss_target"]))
    with _jax.named_scope("update"):
        delta_w, new_m, new_v = {}, {}, {}
        for n in TWIN_WEIGHTS:
            delta_w[n], new_m[n], new_v[n] = _adamw(weights[n], grad_w[n], given["m_" + n], given["v_" + n])
    return (loss, grad_x, *[grad_w[n] for n in TWIN_WEIGHTS], *[delta_w[n] for n in TWIN_WEIGHTS],
            *[new_m[n] for n in TWIN_WEIGHTS], *[new_v[n] for n in TWIN_WEIGHTS])
```

```python
import math

import numpy as np
import jax
import jax.numpy as jnp
from jax import lax
from jax.experimental import pallas as pl
from jax.experimental.pallas import tpu as pltpu

F32 = jnp.float32
MXU_DTYPE = jnp.bfloat16
VMEM_LIMIT_BYTES = 56 * 2**20
LANE = 128
N_DEV = 8
MESH_AXES = ("x", "y", "c")
MESH = pl.DeviceIdType.MESH

HEAD_DIM = 128
BAND = 128
BRANCHES = ((128, 1), (512, 4), (2048, 16))
ATT_TB = BAND * 16
REL_MAX_DIST = 2048
LN_EPS = 1e-5
NEG = -1e30
HALO = 32

ADAM_LR, ADAM_B1, ADAM_B2, ADAM_EPS, ADAM_WD, ADAM_STEP = 0.001, 0.9, 0.999, 1e-08, 0.01, 10

NN = (((1,), (0,)), ((), ()))
NT = (((1,), (1,)), ((), ()))
TN = (((0,), (0,)), ((), ()))


def _cp(sem=None):
    return pltpu.CompilerParams(dimension_semantics=sem, vmem_limit_bytes=VMEM_LIMIT_BYTES)


def _sig(v):
    return 1.0 / (1.0 + jnp.exp(-v))


def _mm(name, a, b, *, grid, a_spec, b_spec, dims, k_axis, epilogue, out_shapes, out_specs,
        acc_shape, extra=(), extra_specs=(), exact=False):
    nk = grid[k_axis]
    n_extra, n_out = len(extra), len(out_shapes)

    def body(a_ref, b_ref, *rest):
        ex, outs = rest[:n_extra], rest[n_extra:n_extra + n_out]
        if exact:
            part = lax.dot_general(a_ref[...], b_ref[...], dims, precision=lax.Precision.HIGHEST,
                                   preferred_element_type=F32)
        else:
            part = lax.dot_general(a_ref[...].astype(MXU_DTYPE), b_ref[...].astype(MXU_DTYPE), dims,
                                   preferred_element_type=F32)
        if nk == 1:
            epilogue(part, ex, outs)
        else:
            acc = rest[n_extra + n_out]
            k = pl.program_id(k_axis)

            @pl.when(k == 0)
            def _():
                acc[...] = part

            @pl.when(k > 0)
            def _():
                acc[...] += part

            @pl.when(k == nk - 1)
            def _():
                epilogue(acc[...], ex, outs)

    sem = tuple("arbitrary" if ax == k_axis else "parallel" for ax in range(len(grid)))
    return pl.pallas_call(
        body, name=name, grid=grid,
        in_specs=[a_spec, b_spec, *extra_specs],
        out_specs=list(out_specs),
        out_shape=list(out_shapes),
        scratch_shapes=[pltpu.VMEM(acc_shape, F32)] if nk > 1 else [],
        compiler_params=_cp(sem),
    )(a, b, *extra)


def _blk(n, want):
    return min(n, want)


def mm_fwd(name, x, w, *, colsharded, layer=None, epilogue, outs, rowvec=(), tiles=()):
    T, K = x.shape
    bm = _blk(T, 1024)
    if colsharded:
        n_s = w.shape[2]
        N = N_DEV * n_s
        bn = _blk(n_s, 1024)
        per = n_s // bn
        bk, nk = K, 1
        koff = 0 if layer is None else layer
        b_spec = pl.BlockSpec((None, K, bn), lambda i, j, k: (j // per, koff, j % per))
    else:
        N = w.shape[2]
        bn = _blk(N, 1024)
        k_s = K // N_DEV
        bk = k_s if layer is not None else k_s * max(1, min(N_DEV, 2048 // k_s))
        spb = bk // k_s
        nk = K // bk
        lay = 0 if layer is None else layer
        if spb == 1:
            b_spec = pl.BlockSpec((None, bk, bn), lambda i, j, k: (k, lay, j))
        else:
            assert layer is None
            w = w.reshape(K, N)
            b_spec = pl.BlockSpec((bk, bn), lambda i, j, k: (k, j))
    grid = (T // bm, N // bn, nk)
    a_spec = pl.BlockSpec((bm, bk), lambda i, j, k: (i, k))
    tile_spec = pl.BlockSpec((bm, bn), lambda i, j, k: (i, j))
    vec_spec = pl.BlockSpec((1, bn), lambda i, j, k: (0, j))
    return _mm(name, x, w, grid=grid, a_spec=a_spec, b_spec=b_spec, dims=NN, k_axis=2, epilogue=epilogue,
               out_shapes=[jax.ShapeDtypeStruct((T, N), dt) for dt in outs], out_specs=[tile_spec] * len(outs),
               acc_shape=(bm, bn), extra=(*rowvec, *tiles), extra_specs=[vec_spec] * len(rowvec) + [tile_spec] * len(tiles))


def mm_dx(name, dy, w, *, colsharded, layer=None, epilogue, outs, tiles=()):
    T, N = dy.shape
    bm = _blk(T, 1024)
    if colsharded:
        n_s = w.shape[2]
        K = w.shape[1] if layer is None else w.shape[1] // 2
        bko = _blk(K, 1024)
        lay = 0 if layer is None else layer * (K // bko)
        grid = (T // bm, K // bko, N_DEV)
        a_spec = pl.BlockSpec((bm, n_s), lambda i, j, s: (i, s))
        b_spec = pl.BlockSpec((None, bko, n_s), lambda i, j, s: (s, lay + j, 0))
    else:
        k_s = w.shape[1] if layer is None else w.shape[1] // 2
        K = k_s * N_DEV
        if layer is None:
            w = w.reshape(K, N)
            bko = _blk(K, 1024)
            b_spec = pl.BlockSpec((bko, N), lambda i, j, s: (j, 0))
        else:
            bko = k_s
            b_spec = pl.BlockSpec((None, bko, N), lambda i, j, s: (j, layer, 0))
        grid = (T // bm, K // bko, 1)
        a_spec = pl.BlockSpec((bm, N), lambda i, j, s: (i, 0))
    tile_spec = pl.BlockSpec((bm, bko), lambda i, j, s: (i, j))
    return _mm(name, dy, w, grid=grid, a_spec=a_spec, b_spec=b_spec, dims=NT, k_axis=2, epilogue=epilogue,
               out_shapes=[jax.ShapeDtypeStruct((T, K), dt) for dt in outs], out_specs=[tile_spec] * len(outs),
               acc_shape=(bm, bko), extra=tuple(tiles), extra_specs=[tile_spec] * len(tiles))


def mm_dw(name, x, dy, *, colsharded):
    T, K = x.shape
    N = dy.shape[1]
    bt = _blk(T, 2048)
    bmo = _blk(K, 1024)
    if colsharded:
        n_s = N // N_DEV
        bno = _blk(n_s, 1024)
        per = n_s // bno
        out_shape = jax.ShapeDtypeStruct((N_DEV, K, n_s), F32)
        out_spec = pl.BlockSpec((None, bmo, bno), lambda i, j, t: (j // per, i, j % per))
    else:
        bno = _blk(N, 1024)
        out_shape = jax.ShapeDtypeStruct((K, N), F32)
        out_spec = pl.BlockSpec((bmo, bno), lambda i, j, t: (i, j))
    grid = (K // bmo, N // bno, T // bt)
    a_spec = pl.BlockSpec((bt, bmo), lambda i, j, t: (t, i))
    b_spec = pl.BlockSpec((bt, bno), lambda i, j, t: (t, j))

    def epilogue(acc, ex, outs):
        outs[0][...] = acc

    (out,) = _mm(name, x, dy, grid=grid, a_spec=a_spec, b_spec=b_spec, dims=TN, k_axis=2, epilogue=epilogue,
                 out_shapes=[out_shape], out_specs=[out_spec], acc_shape=(bmo, bno))
    return out if colsharded else out.reshape(N_DEV, K // N_DEV, N)


def cast_bf16(name, a):
    R, C = a.shape
    br = _blk(R, 512)

    def body(a_ref, o_ref):
        o_ref[...] = a_ref[...].astype(MXU_DTYPE)

    spec = pl.BlockSpec((br, C), lambda i: (i, 0))
    return pl.pallas_call(body, name=name, grid=(R // br,), in_specs=[spec], out_specs=spec,
                          out_shape=jax.ShapeDtypeStruct((R, C), MXU_DTYPE), compiler_params=_cp(("parallel",)))(a)


def _ln_stats(z):
    mu = jnp.mean(z, axis=-1, keepdims=True)
    zc = z - mu
    var = jnp.mean(zc * zc, axis=-1, keepdims=True)
    return zc * lax.rsqrt(var + LN_EPS)


def ln_fwd(name, z, g, b):
    T, D = z.shape
    br = _blk(T, 512)

    def body(z_ref, g_ref, b_ref, y_ref, yb_ref):
        y = _ln_stats(z_ref[...]) * g_ref[...] + b_ref[...]
        y_ref[...] = y
        yb_ref[...] = y.astype(MXU_DTYPE)

    row = pl.BlockSpec((br, D), lambda i: (i, 0))
    vec = pl.BlockSpec((1, D), lambda i: (0, 0))
    return pl.pallas_call(body, name=name, grid=(T // br,), in_specs=[row, vec, vec], out_specs=[row, row],
                          out_shape=[jax.ShapeDtypeStruct((T, D), F32), jax.ShapeDtypeStruct((T, D), MXU_DTYPE)],
                          compiler_params=_cp(("parallel",)))(z, g, b)


def ln_bwd(name, dy, z, g):
    T, D = z.shape
    br = _blk(T, 512)

    def body(dy_ref, z_ref, g_ref, dz_ref, dzb_ref, dg_ref, db_ref, ds_ref):
        i = pl.program_id(0)
        z = z_ref[...]
        dy = dy_ref[...]
        mu = jnp.mean(z, axis=-1, keepdims=True)
        zc = z - mu
        var = jnp.mean(zc * zc, axis=-1, keepdims=True)
        rstd = lax.rsqrt(var + LN_EPS)
        xhat = zc * rstd
        dxh = dy * g_ref[...]
        m1 = jnp.mean(dxh, axis=-1, keepdims=True)
        m2 = jnp.mean(dxh * xhat, axis=-1, keepdims=True)
        dz = rstd * (dxh - m1 - xhat * m2)
        dz_ref[...] = dz
        dzb_ref[...] = dz.astype(MXU_DTYPE)

        @pl.when(i == 0)
        def _():
            dg_ref[...] = jnp.zeros_like(dg_ref)
            db_ref[...] = jnp.zeros_like(db_ref)
            ds_ref[...] = jnp.zeros_like(ds_ref)

        dg_ref[...] += jnp.sum(dy * xhat, axis=0, keepdims=True)
        db_ref[...] += jnp.sum(dy, axis=0, keepdims=True)
        ds_ref[...] += jnp.sum(dz, axis=0, keepdims=True)

    row = pl.BlockSpec((br, D), lambda i: (i, 0))
    vec = pl.BlockSpec((1, D), lambda i: (0, 0))
    vshape = jax.ShapeDtypeStruct((1, D), F32)
    return pl.pallas_call(body, name=name, grid=(T // br,), in_specs=[row, row, vec],
                          out_specs=[row, row, vec, vec, vec],
                          out_shape=[jax.ShapeDtypeStruct((T, D), F32), jax.ShapeDtypeStruct((T, D), MXU_DTYPE),
                                     vshape, vshape, vshape],
                          compiler_params=_cp(("arbitrary",)))(dy, z, g)


def loss_head(name, y, target):
    T, D = y.shape
    br = _blk(T, 512)
    n = T // br

    def body(y_ref, t_ref, loss_ref, dy_ref, acc_ref):
        i = pl.program_id(0)
        err = y_ref[...] - t_ref[...]
        dy_ref[...] = err * (1.0 / D)

        @pl.when(i == 0)
        def _():
            acc_ref[...] = jnp.zeros_like(acc_ref)

        acc_ref[...] += jnp.sum(err * err, axis=0, keepdims=True)

        @pl.when(i == n - 1)
        def _():
            loss_ref[...] = (0.5 / D) * jnp.sum(acc_ref[...], axis=1, keepdims=True)

    row = pl.BlockSpec((br, D), lambda i: (i, 0))
    return pl.pallas_call(body, name=name, grid=(n,), in_specs=[row, row],
                          out_specs=[pl.BlockSpec((1, 1), lambda i: (0, 0)), row],
                          out_shape=[jax.ShapeDtypeStruct((1, 1), F32), jax.ShapeDtypeStruct((T, D), F32)],
                          scratch_shapes=[pltpu.VMEM((1, D), F32)], compiler_params=_cp(("arbitrary",)))(y, target)


CONV_BT = 128


def conv_fwd(name, h1, dw_w, dw_b, ln_g, ln_b):
    T, D2 = h1.shape
    D = D2 // 2
    W = dw_w.shape[0]
    taps = W - 1
    bt = _blk(T, CONV_BT)
    hb = bt // HALO

    def body(h_ref, hp_ref, w_ref, b_ref, g_ref, be_ref, c_ref, s_ref, ux):
        i = pl.program_id(0)
        up = hp_ref[:, :D] * _sig(hp_ref[:, D:])
        ux[0:HALO, :] = jnp.where(i == 0, 0.0, up)
        ux[HALO:HALO + bt, :] = h_ref[:, :D] * _sig(h_ref[:, D:])
        for cb in range(D // LANE):
            cs = slice(cb * LANE, (cb + 1) * LANE)
            acc = jnp.broadcast_to(b_ref[:, cs], (bt, LANE))
            for k in range(taps):
                acc = acc + w_ref[k:k + 1, cs] * ux[pl.ds(HALO - (taps - 1) + k, bt), cs]
            c_ref[:, cs] = acc
        n = _ln_stats(c_ref[...]) * g_ref[...] + be_ref[...]
        s_ref[...] = (n * _sig(n)).astype(MXU_DTYPE)

    main = pl.BlockSpec((bt, D2), lambda i: (i, 0))
    prev = pl.BlockSpec((HALO, D2), lambda i: (jnp.maximum(i * hb - 1, 0), 0))
    wspec = pl.BlockSpec((W, D), lambda i: (0, 0))
    vec = pl.BlockSpec((1, D), lambda i: (0, 0))
    row = pl.BlockSpec((bt, D), lambda i: (i, 0))
    return pl.pallas_call(body, name=name, grid=(T // bt,), in_specs=[main, prev, wspec, vec, vec, vec],
                          out_specs=[row, row],
                          out_shape=[jax.ShapeDtypeStruct((T, D), F32), jax.ShapeDtypeStruct((T, D), MXU_DTYPE)],
                          scratch_shapes=[pltpu.VMEM((HALO + bt, D), F32)],
                          compiler_params=_cp(("parallel",)))(h1, h1, dw_w, dw_b, ln_g, ln_b)


def conv_bwd_ln(name, ds, c, ln_g, ln_b):
    T, D = c.shape
    br = _blk(T, 512)

    def body(ds_ref, c_ref, g_ref, be_ref, dc_ref, dg_ref, db_ref, dcs_ref):
        i = pl.program_id(0)
        c = c_ref[...]
        mu = jnp.mean(c, axis=-1, keepdims=True)
        cc = c - mu
        var = jnp.mean(cc * cc, axis=-1, keepdims=True)
        rstd = lax.rsqrt(var + LN_EPS)
        xhat = cc * rstd
        n = xhat * g_ref[...] + be_ref[...]
        sg = _sig(n)
        dn = ds_ref[...] * (sg * (1.0 + n * (1.0 - sg)))
        dxh = dn * g_ref[...]
        m1 = jnp.mean(dxh, axis=-1, keepdims=True)
        m2 = jnp.mean(dxh * xhat, axis=-1, keepdims=True)
        dc = rstd * (dxh - m1 - xhat * m2)
        dc_ref[...] = dc

        @pl.when(i == 0)
        def _():
            dg_ref[...] = jnp.zeros_like(dg_ref)
            db_ref[...] = jnp.zeros_like(db_ref)
            dcs_ref[...] = jnp.zeros_like(dcs_ref)

        dg_ref[...] += jnp.sum(dn * xhat, axis=0, keepdims=True)
        db_ref[...] += jnp.sum(dn, axis=0, keepdims=True)
        dcs_ref[...] += jnp.sum(dc, axis=0, keepdims=True)

    row = pl.BlockSpec((br, D), lambda i: (i, 0))
    vec = pl.BlockSpec((1, D), lambda i: (0, 0))
    vshape = jax.ShapeDtypeStruct((1, D), F32)
    return pl.pallas_call(body, name=name, grid=(T // br,), in_specs=[row, row, vec, vec],
                          out_specs=[row, vec, vec, vec],
                          out_shape=[jax.ShapeDtypeStruct((T, D), F32), vshape, vshape, vshape],
                          compiler_params=_cp(("arbitrary",)))(ds, c, ln_g, ln_b)


def conv_bwd_dw(name, dc, h1, dw_w):
    T, D2 = h1.shape
    D = D2 // 2
    W = dw_w.shape[0]
    taps = W - 1
    bt = _blk(T, CONV_BT)
    hb = bt // HALO
    n = T // bt

    def body(dc_ref, dcn_ref, h_ref, hp_ref, w_ref, dh_ref, dw_ref, dhs_ref, ux, dcx, du):
        i = pl.program_id(0)
        a = h_ref[:, :D]
        sg = _sig(h_ref[:, D:])
        up = hp_ref[:, :D] * _sig(hp_ref[:, D:])
        ux[0:HALO, :] = jnp.where(i == 0, 0.0, up)
        ux[HALO:HALO + bt, :] = a * sg
        dcx[0:bt, :] = dc_ref[...]
        dcx[bt:bt + HALO, :] = jnp.where(i == n - 1, 0.0, dcn_ref[...])

        @pl.when(i == 0)
        def _():
            dw_ref[...] = jnp.zeros_like(dw_ref)
            dhs_ref[...] = jnp.zeros_like(dhs_ref)

        for cb in range(D // LANE):
            cs = slice(cb * LANE, (cb + 1) * LANE)
            dcb = dcx[0:bt, cs]
            acc = jnp.zeros((bt, LANE), F32)
            for k in range(taps):
                acc = acc + w_ref[k:k + 1, cs] * dcx[pl.ds(taps - 1 - k, bt), cs]
                dw_ref[k:k + 1, cs] += jnp.sum(dcb * ux[pl.ds(HALO - (taps - 1) + k, bt), cs], axis=0, keepdims=True)
            du[:, cs] = acc
        d_u = du[...]
        da = d_u * sg
        dg = d_u * a * sg * (1.0 - sg)
        dh_ref[:, :D] = da.astype(MXU_DTYPE)
        dh_ref[:, D:] = dg.astype(MXU_DTYPE)
        dhs_ref[:, :D] += jnp.sum(da, axis=0, keepdims=True)
        dhs_ref[:, D:] += jnp.sum(dg, axis=0, keepdims=True)

    row = pl.BlockSpec((bt, D), lambda i: (i, 0))
    nxt = pl.BlockSpec((HALO, D), lambda i: (jnp.minimum((i + 1) * hb, T // HALO - 1), 0))
    main = pl.BlockSpec((bt, D2), lambda i: (i, 0))
    prev = pl.BlockSpec((HALO, D2), lambda i: (jnp.maximum(i * hb - 1, 0), 0))
    wspec = pl.BlockSpec((W, D), lambda i: (0, 0))
    return pl.pallas_call(body, name=name, grid=(n,), in_specs=[row, nxt, main, prev, wspec],
                          out_specs=[main, wspec, pl.BlockSpec((1, D2), lambda i: (0, 0))],
                          out_shape=[jax.ShapeDtypeStruct((T, D2), MXU_DTYPE), jax.ShapeDtypeStruct((W, D), F32),
                                     jax.ShapeDtypeStruct((1, D2), F32)],
                          scratch_shapes=[pltpu.VMEM((HALO + bt, D), F32), pltpu.VMEM((bt + HALO, D), F32),
                                          pltpu.VMEM((bt, D), F32)],
                          compiler_params=_cp(("arbitrary",)))(dc, dc, h1, h1, dw_w)


def _t5_bucket(dist, n_buckets):
    max_exact = n_buckets // 2
    large = max_exact + (np.log(np.maximum(dist, 1) / max_exact) / math.log(REL_MAX_DIST / max_exact)
                         * (n_buckets - max_exact)).astype(np.int32)
    large = np.minimum(large, n_buckets - 1)
    return np.where(dist < max_exact, dist, large).astype(np.int32)


def _band_tables(n_buckets):
    i = np.arange(BAND)[:, None]
    j = np.arange(2 * BAND)[None, :]
    delta = i - j + BAND
    out = []
    for window, dil in BRANCHES:
        ok = (delta >= 0) & (delta <= window // dil)
        out.append((_t5_bucket(np.clip(delta, 0, None) * dil, n_buckets), ok))
    return out


def _units():
    for bi in (2, 1, 0):
        d = BRANCHES[bi][1]
        for r in range(d):
            for nb in range(ATT_TB // (BAND * d)):
                yield bi, d, r, nb


def _rows(start, size, d):
    return pl.ds(start, size) if d == 1 else pl.ds(start, size, stride=d)


def _bc(v):
    return jnp.broadcast_to(v, (BAND, LANE))


def attn_fwd(name, q, kv, bias):
    T, D = q.shape
    H = D // HEAD_DIM
    TB = ATT_TB
    scale = HEAD_DIM ** -0.5

    def body(q_ref, kc_ref, kp_ref, vc_ref, vp_ref, b_ref, o_ref, l_ref, kx, vx, m_sc, s_sc, a_sc):
        first = pl.program_id(1) == 0
        kx[0:TB, :] = kp_ref[...]
        kx[TB:2 * TB, :] = kc_ref[...]
        vx[0:TB, :] = vp_ref[...]
        vx[TB:2 * TB, :] = vc_ref[...]
        col = lax.broadcasted_iota(jnp.int32, (BAND, 2 * BAND), 1)
        for bi, d, r, nb in _units():
            qs = _rows(nb * BAND * d + r, BAND, d)
            ks = _rows(TB + (nb - 1) * BAND * d + r, 2 * BAND, d)
            qb = q_ref[qs, :].astype(MXU_DTYPE)
            kb = kx[ks, :].astype(MXU_DTYPE)
            vb = vx[ks, :].astype(MXU_DTYPE)
            s = lax.dot_general(qb, kb, NT, preferred_element_type=F32) * scale + b_ref[bi]
            if nb == 0:
                s = jnp.where(jnp.logical_and(first, col < BAND), NEG, s)
            mrow = jnp.max(s, axis=1, keepdims=True)
            if bi == 2:
                p = jnp.exp(s - mrow)
                m_new = _bc(mrow)
                l_new = _bc(jnp.sum(p, axis=1, keepdims=True))
                acc = lax.dot_general(p.astype(MXU_DTYPE), vb, NN, preferred_element_type=F32)
            else:
                m_old = m_sc[qs, :]
                m_new = jnp.maximum(m_old, _bc(mrow))
                alpha = jnp.exp(m_old - m_new)
                p = jnp.exp(s - m_new[:, :1])
                l_new = alpha * s_sc[qs, :] + _bc(jnp.sum(p, axis=1, keepdims=True))
                acc = alpha * a_sc[qs, :] + lax.dot_general(p.astype(MXU_DTYPE), vb, NN, preferred_element_type=F32)
            if bi == 0:
                o_ref[qs, :] = acc / l_new
                l_ref[qs, :] = m_new + jnp.log(l_new)
            else:
                m_sc[qs, :] = m_new
                s_sc[qs, :] = l_new
                a_sc[qs, :] = acc

    cur = lambda off: pl.BlockSpec((TB, HEAD_DIM), lambda h, i: (i, off + h))
    prv = lambda off: pl.BlockSpec((TB, HEAD_DIM), lambda h, i: (jnp.maximum(i - 1, 0), off + h))
    bspec = pl.BlockSpec((3, None, BAND, 2 * BAND), lambda h, i: (0, h, 0, 0))
    sc = lambda rows: pltpu.VMEM((rows, HEAD_DIM), F32)
    return pl.pallas_call(body, name=name, grid=(H, T // TB),
                          in_specs=[cur(0), cur(0), prv(0), cur(H), prv(H), bspec],
                          out_specs=[cur(0), cur(0)],
                          out_shape=[jax.ShapeDtypeStruct((T, D), F32), jax.ShapeDtypeStruct((T, D), F32)],
                          scratch_shapes=[sc(2 * TB), sc(2 * TB), sc(TB), sc(TB), sc(TB)],
                          compiler_params=_cp(("parallel", "parallel")))(q, kv, kv, kv, kv, bias)


def attn_bwd(name, q, kv, bias, o, lse, do):
    T, D = q.shape
    H = D // HEAD_DIM
    TB = ATT_TB
    nI = T // TB
    scale = HEAD_DIM ** -0.5

    def body(q_ref, kc_ref, kp_ref, vc_ref, vp_ref, b_ref, o_ref, l_ref, do_ref,
             dq_ref, dk_ref, dv_ref, db_ref, kx, vx, dkx, dvx, ck, cv, dl_sc):
        step = pl.program_id(1)
        first = step == nI - 1
        kx[0:TB, :] = kp_ref[...]
        kx[TB:2 * TB, :] = kc_ref[...]
        vx[0:TB, :] = vp_ref[...]
        vx[TB:2 * TB, :] = vc_ref[...]
        dkx[...] = jnp.zeros_like(dkx)
        dvx[...] = jnp.zeros_like(dvx)
        dq_ref[...] = jnp.zeros_like(dq_ref)
        dl_sc[...] = jnp.broadcast_to(jnp.sum(do_ref[...] * o_ref[...], axis=1, keepdims=True), (TB, LANE))

        @pl.when(step == 0)
        def _():
            db_ref[...] = jnp.zeros_like(db_ref)
            ck[...] = jnp.zeros_like(ck)
            cv[...] = jnp.zeros_like(cv)

        col = lax.broadcasted_iota(jnp.int32, (BAND, 2 * BAND), 1)
        for bi, d, r, nb in _units():
            qs = _rows(nb * BAND * d + r, BAND, d)
            ks = _rows(TB + (nb - 1) * BAND * d + r, 2 * BAND, d)
            qb = q_ref[qs, :].astype(MXU_DTYPE)
            kb = kx[ks, :].astype(MXU_DTYPE)
            vb = vx[ks, :].astype(MXU_DTYPE)
            dob = do_ref[qs, :].astype(MXU_DTYPE)
            s = lax.dot_general(qb, kb, NT, preferred_element_type=F32) * scale + b_ref[bi]
            if nb == 0:
                s = jnp.where(jnp.logical_and(first, col < BAND), NEG, s)
            p = jnp.exp(s - l_ref[qs, :][:, :1])
            dp = lax.dot_general(dob, vb, NT, preferred_element_type=F32)
            dsv = p * (dp - dl_sc[qs, :][:, :1])
            db_ref[bi] += dsv
            dsb = dsv.astype(MXU_DTYPE)
            dvx[ks, :] += lax.dot_general(p.astype(MXU_DTYPE), dob, TN, preferred_element_type=F32)
            dkx[ks, :] += lax.dot_general(dsb, qb, TN, preferred_element_type=F32) * scale
            dq_ref[qs, :] += lax.dot_general(dsb, kb, NN, preferred_element_type=F32) * scale

        dk_ref[...] = dkx[TB:2 * TB, :] + ck[...]
        dv_ref[...] = dvx[TB:2 * TB, :] + cv[...]
        ck[...] = dkx[0:TB, :]
        cv[...] = dvx[0:TB, :]

    blk = lambda h, i: nI - 1 - i
    cur = lambda off: pl.BlockSpec((TB, HEAD_DIM), lambda h, i: (blk(h, i), off + h))
    prv = lambda off: pl.BlockSpec((TB, HEAD_DIM), lambda h, i: (jnp.maximum(blk(h, i) - 1, 0), off + h))
    bspec = pl.BlockSpec((3, None, BAND, 2 * BAND), lambda h, i: (0, h, 0, 0))
    sc = lambda rows: pltpu.VMEM((rows, HEAD_DIM), F32)
    return pl.pallas_call(body, name=name, grid=(H, nI),
                          in_specs=[cur(0), cur(0), prv(0), cur(H), prv(H), bspec, cur(0), cur(0), cur(0)],
                          out_specs=[cur(0), cur(0), cur(0), bspec],
                          out_shape=[jax.ShapeDtypeStruct((T, D), F32), jax.ShapeDtypeStruct((T, D), F32),
                                     jax.ShapeDtypeStruct((T, D), F32),
                                     jax.ShapeDtypeStruct((3, H, BAND, 2 * BAND), F32)],
                          scratch_shapes=[sc(2 * TB), sc(2 * TB), sc(2 * TB), sc(2 * TB), sc(TB), sc(TB), sc(TB)],
                          compiler_params=_cp(("arbitrary", "arbitrary")))(q, kv, kv, kv, kv, bias, o, lse, do)


def adamw(name, parts, w, m, v):
    P, R, C = parts.shape
    br = R if R % 8 else _blk(R, max(8, (1 << 18) // C))
    c1 = 1.0 / (1.0 - ADAM_B1 ** ADAM_STEP)
    c2 = 1.0 / (1.0 - ADAM_B2 ** ADAM_STEP)

    def body(p_ref, w_ref, m_ref, v_ref, g_ref, d_ref, nm_ref, nv_ref):
        g = p_ref[0]
        for k in range(1, P):
            g = g + p_ref[k]
        nm = ADAM_B1 * m_ref[...] + (1.0 - ADAM_B1) * g
        nv = ADAM_B2 * v_ref[...] + (1.0 - ADAM_B2) * (g * g)
        g_ref[...] = g
        nm_ref[...] = nm
        nv_ref[...] = nv
        d_ref[...] = -ADAM_LR * ((nm * c1) / (jnp.sqrt(nv * c2) + ADAM_EPS) + ADAM_WD * w_ref[...])

    row = pl.BlockSpec((br, C), lambda i: (i, 0))
    pspec = pl.BlockSpec((P, br, C), lambda i: (0, i, 0))
    shp = jax.ShapeDtypeStruct((R, C), F32)
    return pl.pallas_call(body, name=name, grid=(R // br,), in_specs=[pspec, row, row, row],
                          out_specs=[row] * 4, out_shape=[shp] * 4, compiler_params=_cp(("parallel",)))(parts, w, m, v)


def pair_sum(name, g, s1, c_idx):
    _, _, R, C = g.shape
    br = _blk(R, max(8, (1 << 19) // C))

    def body(c_ref, g_ref, s_ref, t_ref):
        t_ref[...] = g_ref[...] + s_ref[...]

    return pl.pallas_call(
        body, name=name,
        grid_spec=pltpu.PrefetchScalarGridSpec(
            num_scalar_prefetch=1, grid=(4, R // br),
            in_specs=[pl.BlockSpec((None, None, br, C), lambda j, i, c: (j, c[0], i, 0)),
                      pl.BlockSpec((None, br, C), lambda j, i, c: (j, i, 0))],
            out_specs=pl.BlockSpec((None, br, C), lambda j, i, c: (j, i, 0))),
        out_shape=jax.ShapeDtypeStruct((4, R, C), F32), compiler_params=_cp(("parallel", "parallel")))(c_idx, g, s1)


def _me():
    return lax.axis_index("x"), lax.axis_index("y"), lax.axis_index("c")


ANY = pl.BlockSpec(memory_space=pl.ANY)


def all_gather(name, shards):
    n = len(shards)

    def body(*refs):
        ins, outs = refs[:n], refs[n:2 * n]
        send_sems, recv_sems, local_sems = refs[2 * n:]
        x, y, c = _me()
        sibling = (x, y, 1 - c)
        chips = [(1 - x, y), (x, 1 - y), (1 - x, 1 - y)]

        def slot(px, py, pc):
            return 4 * px + 2 * py + pc

        def copy(a, k, block, to, src=None):
            dst = outs[a].at[slot(*block)]
            return pltpu.make_async_remote_copy(src_ref=dst if src is None else src, dst_ref=dst,
                                                send_sem=send_sems.at[a, k], recv_sem=recv_sems.at[a, k],
                                                device_id=to, device_id_type=MESH)

        mine = [pltpu.make_async_copy(ins[a], outs[a].at[slot(x, y, c)], local_sems.at[a]) for a in range(n)]
        for cp in mine:
            cp.start()
        first = []
        for a in range(n):
            first.append(copy(a, 0, (x, y, c), sibling, src=ins[a]))
            first += [copy(a, 1 + j, (x, y, c), (*chip, c), src=ins[a]) for j, chip in enumerate(chips)]
        for cp in first:
            cp.start()
        passed = []
        for j, chip in enumerate(chips):
            for a in range(n):
                copy(a, 1 + j, (*chip, c), (x, y, c)).wait_recv()
                fwd = copy(a, 4 + j, (*chip, c), sibling)
                fwd.start()
                passed.append(fwd)
        for a in range(n):
            copy(a, 0, (x, y, 1 - c), (x, y, c)).wait_recv()
            for j, chip in enumerate(chips):
                copy(a, 4 + j, (*chip, 1 - c), (x, y, c)).wait_recv()
        for cp in first + passed:
            cp.wait_send()
        for cp in mine:
            cp.wait()

    return pl.pallas_call(
        body, name=name, in_specs=[ANY] * n, out_specs=[ANY] * n,
        out_shape=[jax.ShapeDtypeStruct((N_DEV,) + s.shape, s.dtype) for s in shards],
        scratch_shapes=[pltpu.SemaphoreType.DMA((n, 7)), pltpu.SemaphoreType.DMA((n, 7)), pltpu.SemaphoreType.DMA((n,))],
    )(*shards)


def pair_exchange(name, grads):
    n = len(grads)

    def body(*refs):
        ins, outs = refs[:n], refs[n:2 * n]
        send_sems, recv_sems = refs[2 * n:]
        x, y, c = _me()
        copies = []
        for a in range(n):
            for j in range(4):
                copies.append(pltpu.make_async_remote_copy(
                    src_ref=ins[a].at[j, 1 - c], dst_ref=outs[a].at[j],
                    send_sem=send_sems.at[a, j], recv_sem=recv_sems.at[a, j],
                    device_id=(x, y, 1 - c), device_id_type=MESH))
        for cp in copies:
            cp.start()
        for cp in copies:
            cp.wait()

    return pl.pallas_call(
        body, name=name, in_specs=[ANY] * n, out_specs=[ANY] * n,
        out_shape=[jax.ShapeDtypeStruct((4,) + g.shape[2:], g.dtype) for g in grads],
        scratch_shapes=[pltpu.SemaphoreType.DMA((n, 4)), pltpu.SemaphoreType.DMA((n, 4))],
    )(*grads)


def chip_exchange(name, sums):
    n = len(sums)

    def body(*refs):
        ins, outs = refs[:n], refs[n:2 * n]
        send_sems, recv_sems, local_sems = refs[2 * n:]
        x, y, c = _me()
        my = 2 * x + y
        chips = [(1 - x, y), (x, 1 - y), (1 - x, 1 - y)]
        mine = [pltpu.make_async_copy(ins[a].at[my], outs[a].at[my], local_sems.at[a]) for a in range(n)]
        for cp in mine:
            cp.start()
        copies = []
        for a in range(n):
            for j, (px, py) in enumerate(chips):
                copies.append(pltpu.make_async_remote_copy(
                    src_ref=ins[a].at[2 * px + py], dst_ref=outs[a].at[my],
                    send_sem=send_sems.at[a, j], recv_sem=recv_sems.at[a, j],
                    device_id=(px, py, c), device_id_type=MESH))
        for cp in copies:
            cp.start()
        for a in range(n):
            for j, (px, py) in enumerate(chips):
                pltpu.make_async_remote_copy(
                    src_ref=ins[a].at[my], dst_ref=outs[a].at[2 * px + py],
                    send_sem=send_sems.at[a, j], recv_sem=recv_sems.at[a, j],
                    device_id=(px, py, c), device_id_type=MESH).wait_recv()
        for cp in copies:
            cp.wait_send()
        for cp in mine:
            cp.wait()

    return pl.pallas_call(
        body, name=name, in_specs=[ANY] * n, out_specs=[ANY] * n,
        out_shape=[jax.ShapeDtypeStruct(s.shape, s.dtype) for s in sums],
        scratch_shapes=[pltpu.SemaphoreType.DMA((n, 3)), pltpu.SemaphoreType.DMA((n, 3)), pltpu.SemaphoreType.DMA((n,))],
    )(*sums)


def all_reduce_small(name, pack):
    R, C = pack.shape

    def body(p_ref, o_ref, buf, send_sems, recv_sems):
        x, y, c = _me()
        me = 4 * x + 2 * y + c
        buf[me] = p_ref[...]
        copies = []
        for k in range(1, N_DEV):
            px, py, pc = x ^ (k >> 2), y ^ ((k >> 1) & 1), c ^ (k & 1)
            copies.append(pltpu.make_async_remote_copy(
                src_ref=p_ref, dst_ref=buf.at[me], send_sem=send_sems.at[k - 1], recv_sem=recv_sems.at[k - 1],
                device_id=(px, py, pc), device_id_type=MESH))
        for cp in copies:
            cp.start()
        for cp in copies:
            cp.wait()
        acc = buf[0]
        for d in range(1, N_DEV):
            acc = acc + buf[d]
        o_ref[...] = acc

    vm = pl.BlockSpec(memory_space=pltpu.VMEM)
    return pl.pallas_call(
        body, name=name, in_specs=[vm], out_specs=vm, out_shape=jax.ShapeDtypeStruct((R, C), F32),
        scratch_shapes=[pltpu.VMEM((N_DEV, R, C), F32), pltpu.SemaphoreType.DMA((N_DEV - 1,)),
                        pltpu.SemaphoreType.DMA((N_DEV - 1,))],
    )(pack)


def _ep_bias(acc, ex, outs):
    outs[0][...] = acc + ex[0][...]


def _ep_store(acc, ex, outs):
    outs[0][...] = acc.astype(outs[0].dtype)


def _ep_resid(alpha, bias):
    def ep(acc, ex, outs):
        if bias:
            outs[0][...] = alpha * ex[1][...] + (acc + ex[0][...])
        else:
            outs[0][...] = alpha * ex[0][...] + acc
    return ep


def _ep_relu2(acc, ex, outs):
    r = jnp.maximum(acc, 0.0)
    outs[0][...] = (r * r).astype(outs[0].dtype)
    outs[1][...] = r


def _ep_relu2_bwd(acc, ex, outs):
    outs[0][...] = (acc * (2.0 * ex[0][...])).astype(outs[0].dtype)


def _ep_add(acc, ex, outs):
    outs[0][...] = ex[0][...] + acc


def kernel(x, conv_pw1_w, conv_pw1_b, conv_dw_w, conv_dw_b, conv_ln_g, conv_ln_b, conv_pw2_w, conv_pw2_b, w_kv, attn_wq, attn_wo, rel_bias, mlp_w1, mlp_w2, ln_mix_g, ln_mix_b, ln_mlp_g, ln_mlp_b, loss_target, m_conv_pw1_w, m_conv_pw1_b, m_conv_dw_w, m_conv_dw_b, m_conv_ln_g, m_conv_ln_b, m_conv_pw2_w, m_conv_pw2_b, m_w_kv, m_attn_wq, m_attn_wo, m_rel_bias, m_mlp_w1, m_mlp_w2, m_ln_mix_g, m_ln_mix_b, m_ln_mlp_g, m_ln_mlp_b, v_conv_pw1_w, v_conv_pw1_b, v_conv_dw_w, v_conv_dw_b, v_conv_ln_g, v_conv_ln_b, v_conv_pw2_w, v_conv_pw2_b, v_w_kv, v_attn_wq, v_attn_wo, v_rel_bias, v_mlp_w1, v_mlp_w2, v_ln_mix_g, v_ln_mix_b, v_ln_mlp_g, v_ln_mlp_b):
    T, D = x.shape[1], x.shape[2]
    H = D // HEAD_DIM
    depth = mlp_w1.shape[0]
    assert depth == 2 and T % ATT_TB == 0
    alpha = (2 * depth) ** 0.25
    ds_ = D // N_DEV
    xi, yi, ci = _me()
    me = 4 * xi + 2 * yi + ci
    c_idx = ci.astype(jnp.int32).reshape(1)
    x2 = x.reshape(T, D)
    target = loss_target.reshape(T, D)
    n_buckets = rel_bias.shape[0]
    taps = conv_dw_w.shape[1]

    w2d = {
        "pw1": conv_pw1_w[0], "pw2": conv_pw2_w[0], "wkv": w_kv, "wq": attn_wq[0], "wo": attn_wo[0],
        "w1": mlp_w1.reshape(depth * D, -1), "w2": mlp_w2.reshape(-1, D),
    }
    names = list(w2d)
    small = jnp.concatenate([conv_dw_w[0], conv_dw_b, conv_ln_g, conv_ln_b, conv_pw2_b,
                             conv_pw1_b.reshape(2, ds_)], axis=0)
    small = jnp.pad(small, ((0, (-small.shape[0]) % 8), (0, 0)))
    gathered = all_gather("gather_weights", [cast_bf16("cast_" + k, w2d[k]) for k in names] + [small])
    G = dict(zip(names, gathered[:-1]))
    sm = jnp.transpose(gathered[-1], (1, 0, 2)).reshape(small.shape[0], D)
    dw_w = jnp.pad(sm[:taps], ((0, HALO - taps), (0, 0)))
    dw_b, cln_g, cln_b, pw2_b = (sm[taps + k:taps + k + 1] for k in range(4))
    pw1_b = gathered[-1][:, taps + 4:taps + 6, :].reshape(1, 2 * D)

    row = lambda a, l: a[l:l + 1]

    x_b = cast_bf16("cast_x", x2)
    (h1,) = mm_fwd("pw1", x_b, G["pw1"], colsharded=True, epilogue=_ep_bias, outs=[F32], rowvec=[pw1_b])
    c_pre, s_b = conv_fwd("conv_fwd", h1, dw_w, dw_b, cln_g, cln_b)
    (z1,) = mm_fwd("pw2", s_b, G["pw2"], colsharded=False, epilogue=_ep_resid(alpha, True), outs=[F32],
                   rowvec=[pw2_b], tiles=[x2])
    x1, x1_b = ln_fwd("ln_mix0", z1, row(ln_mix_g, 0), row(ln_mix_b, 0))

    def mlp_fwd(l, xin, xin_b):
        act_b, r = mm_fwd(f"mlp_up{l}", xin_b, G["w1"], colsharded=True, layer=l, epilogue=_ep_relu2, outs=[MXU_DTYPE, F32])
        (z,) = mm_fwd(f"mlp_down{l}", act_b, G["w2"], colsharded=False, layer=l, epilogue=_ep_resid(alpha, False),
                      outs=[F32], tiles=[xin])
        y, y_b = ln_fwd(f"ln_mlp{l}", z, row(ln_mlp_g, l), row(ln_mlp_b, l))
        return act_b, r, z, y, y_b

    act0_b, r0, z2, x2_, x2_b = mlp_fwd(0, x1, x1_b)

    (kv,) = mm_fwd("kv_proj", x2_b, G["wkv"], colsharded=True, epilogue=_ep_store, outs=[F32])
    (q,) = mm_fwd("q_proj", x2_b, G["wq"], colsharded=False, epilogue=_ep_store, outs=[F32])
    tables = _band_tables(n_buckets)
    bias = jnp.stack([jnp.where(jnp.asarray(ok)[None], jnp.transpose(rel_bias[bucket], (2, 0, 1)), NEG)
                      for bucket, ok in tables], axis=0)
    o, lse = attn_fwd("attn_fwd", q, kv, bias)
    (z3,) = mm_fwd("o_proj", o, G["wo"], colsharded=False, epilogue=_ep_resid(alpha, False), outs=[F32], tiles=[x2_])
    x3, x3_b = ln_fwd("ln_mix1", z3, row(ln_mix_g, 1), row(ln_mix_b, 1))
    act1_b, r1, z4, x4, _ = mlp_fwd(1, x3, x3_b)

    loss_local, dy = loss_head("loss", x4, target)
    loss = lax.psum(loss_local[0, 0], MESH_AXES)

    def mlp_bwd(l, dy_out, z, xin_b, act_b, r):
        dz, dz_b, dg, db, _ = ln_bwd(f"ln_mlp{l}_bwd", dy_out, z, row(ln_mlp_g, l))
        (dhm_b,) = mm_dx(f"mlp_down{l}_dx", dz_b, G["w2"], colsharded=False, layer=l, epilogue=_ep_relu2_bwd,
                         outs=[MXU_DTYPE], tiles=[r])
        dw2 = mm_dw(f"mlp_down{l}_dw", act_b, dz_b, colsharded=False)
        (dxin,) = mm_dx(f"mlp_up{l}_dx", dhm_b, G["w1"], colsharded=True, layer=l, epilogue=_ep_resid(alpha, False),
                        outs=[F32], tiles=[dz])
        dw1 = mm_dw(f"mlp_up{l}_dw", xin_b, dhm_b, colsharded=True)
        return dxin, dw1, dw2, dg, db

    dx3, dw1_1, dw2_1, dg_mlp1, db_mlp1 = mlp_bwd(1, dy, z4, x3_b, act1_b, r1)

    dz3, dz3_b, dg_mix1, db_mix1, _ = ln_bwd("ln_mix1_bwd", dx3, z3, row(ln_mix_g, 1))
    (do,) = mm_dx("o_proj_dx", dz3_b, G["wo"], colsharded=False, epilogue=_ep_store, outs=[F32])
    dwo = mm_dw("o_proj_dw", o, dz3_b, colsharded=False)
    dq, dk, dv, dbias = attn_bwd("attn_bwd", q, kv, bias, o, lse, do)
    dkv = jnp.concatenate([dk, dv], axis=1)
    (dx2a,) = mm_dx("q_proj_dx", dq, G["wq"], colsharded=False, epilogue=_ep_resid(alpha, False), outs=[F32], tiles=[dz3])
    (dx2,) = mm_dx("kv_proj_dx", dkv, G["wkv"], colsharded=True, epilogue=_ep_add, outs=[F32], tiles=[dx2a])
    dwq = mm_dw("q_proj_dw", x2_b, dq, colsharded=False)
    dwkv = mm_dw("kv_proj_dw", x2_b, dkv, colsharded=True)

    onehot = jnp.concatenate([(jnp.arange(n_buckets)[:, None] == jnp.asarray(bucket).reshape(1, -1)).astype(F32)
                              for bucket, _ in tables], axis=1)
    dbias2 = jnp.transpose(dbias, (1, 0, 2, 3)).reshape(H, -1)
    KB = dbias2.shape[1] // 8
    (drel_t,) = _mm("rel_bias_grad", dbias2, onehot, grid=(1, 1, 8),
                    a_spec=pl.BlockSpec((H, KB), lambda i, j, k: (0, k)),
                    b_spec=pl.BlockSpec((n_buckets, KB), lambda i, j, k: (0, k)), dims=NT, k_axis=2,
                    epilogue=_ep_store, out_shapes=[jax.ShapeDtypeStruct((H, n_buckets), F32)],
                    out_specs=[pl.BlockSpec((H, n_buckets), lambda i, j, k: (0, 0))], acc_shape=(H, n_buckets), exact=True)

    dx1, dw1_0, dw2_0, dg_mlp0, db_mlp0 = mlp_bwd(0, dx2, z2, x1_b, act0_b, r0)

    dz1, dz1_b, dg_mix0, db_mix0, dpw2_b = ln_bwd("ln_mix0_bwd", dx1, z1, row(ln_mix_g, 0))
    (ds,) = mm_dx("pw2_dx", dz1_b, G["pw2"], colsharded=False, epilogue=_ep_store, outs=[F32])
    dwpw2 = mm_dw("pw2_dw", s_b, dz1_b, colsharded=False)
    dc, dcln_g, dcln_b, ddw_b = conv_bwd_ln("conv_bwd_ln", ds, c_pre, cln_g, cln_b)
    dh1_b, ddw_w, dpw1_b = conv_bwd_dw("conv_bwd_dw", dc, h1, dw_w)
    (dx,) = mm_dx("pw1_dx", dh1_b, G["pw1"], colsharded=True, epilogue=_ep_resid(alpha, False), outs=[F32], tiles=[dz1])
    dwpw1 = mm_dw("pw1_dw", x_b, dh1_b, colsharded=True)
    grad_x = dx.reshape(1, T, D)

    big = {"pw1": dwpw1, "pw2": dwpw2, "wkv": dwkv, "wq": dwq, "wo": dwo,
           "w1_0": dw1_0, "w1_1": dw1_1, "w2_0": dw2_0, "w2_1": dw2_1}
    bnames = list(big)
    g4 = [big[k].reshape((4, 2) + big[k].shape[1:]) for k in bnames]
    from_sibling = pair_exchange("grad_pair_exchange", g4)
    sums = [pair_sum("grad_pair_sum_" + k, g, s, c_idx) for k, g, s in zip(bnames, g4, from_sibling)]
    parts = dict(zip(bnames, chip_exchange("grad_chip_exchange", sums)))

    vec_rows = [dg_mix0, dg_mix1, db_mix0, db_mix1, dg_mlp0, dg_mlp1, db_mlp0, db_mlp1,
                ddw_b, dcln_g, dcln_b, dpw2_b, dpw1_b.reshape(2, D), ddw_w[:taps],
                jnp.pad(jnp.transpose(drel_t).reshape(1, -1), ((0, 0), (0, D - H * n_buckets)))]
    pack = jnp.concatenate(vec_rows, axis=0)
    pack = jnp.pad(pack, ((0, (-pack.shape[0]) % 8), (0, 0)))
    tot = all_reduce_small("grad_small_all_reduce", pack)

    def mine(rows):
        return lax.dynamic_slice_in_dim(rows, me * ds_, ds_, axis=1)

    g_ln_mix_g, g_ln_mix_b, g_ln_mlp_g, g_ln_mlp_b = tot[0:2], tot[2:4], tot[4:6], tot[6:8]
    g_dw_b, g_cln_g, g_cln_b, g_pw2_b = (mine(tot[8 + k:9 + k]) for k in range(4))
    g_pw1_b = lax.dynamic_slice_in_dim(tot[12:14].reshape(1, 2 * D), me * 2 * ds_, 2 * ds_, axis=1)
    g_dw_w = mine(tot[14:14 + taps])
    g_rel = tot[14 + taps, :H * n_buckets].reshape(n_buckets, H)

    res = {}

    def upd(nm, parts_, w, m, v):
        shp = w.shape
        w2_, m2_, v2_ = (a.reshape((-1, shp[-1])) for a in (w, m, v))
        if parts_.ndim == 2:
            parts_ = parts_[None]
        outs = adamw("adamw_" + nm, parts_.reshape((parts_.shape[0],) + w2_.shape), w2_, m2_, v2_)
        res[nm] = tuple(o_.reshape(shp) for o_ in outs)

    upd("conv_pw1_w", parts["pw1"], conv_pw1_w, m_conv_pw1_w, v_conv_pw1_w)
    upd("conv_pw1_b", g_pw1_b, conv_pw1_b, m_conv_pw1_b, v_conv_pw1_b)
    upd("conv_dw_w", g_dw_w, conv_dw_w, m_conv_dw_w, v_conv_dw_w)
    upd("conv_dw_b", g_dw_b, conv_dw_b, m_conv_dw_b, v_conv_dw_b)
    upd("conv_ln_g", g_cln_g, conv_ln_g, m_conv_ln_g, v_conv_ln_g)
    upd("conv_ln_b", g_cln_b, conv_ln_b, m_conv_ln_b, v_conv_ln_b)
    upd("conv_pw2_w", parts["pw2"], conv_pw2_w, m_conv_pw2_w, v_conv_pw2_w)
    upd("conv_pw2_b", g_pw2_b, conv_pw2_b, m_conv_pw2_b, v_conv_pw2_b)
    upd("w_kv", parts["wkv"], w_kv, m_w_kv, v_w_kv)
    upd("attn_wq", parts["wq"], attn_wq, m_attn_wq, v_attn_wq)
    upd("attn_wo", parts["wo"], attn_wo, m_attn_wo, v_attn_wo)
    upd("rel_bias", g_rel, rel_bias, m_rel_bias, v_rel_bias)
    upd("mlp_w1", jnp.concatenate([parts["w1_0"], parts["w1_1"]], axis=1), mlp_w1, m_mlp_w1, v_mlp_w1)
    upd("mlp_w2", jnp.concatenate([parts["w2_0"], parts["w2_1"]], axis=1), mlp_w2, m_mlp_w2, v_mlp_w2)
    upd("ln_mix_g", g_ln_mix_g, ln_mix_g, m_ln_mix_g, v_ln_mix_g)
    upd("ln_mix_b", g_ln_mix_b, ln_mix_b, m_ln_mix_b, v_ln_mix_b)
    upd("ln_mlp_g", g_ln_mlp_g, ln_mlp_g, m_ln_mlp_g, v_ln_mlp_g)
    upd("ln_mlp_b", g_ln_mlp_b, ln_mlp_b, m_ln_mlp_b, v_ln_mlp_b)

    order = ["conv_pw1_w", "conv_pw1_b", "conv_dw_w", "conv_dw_b", "conv_ln_g", "conv_ln_b", "conv_pw2_w",
             "conv_pw2_b", "w_kv", "attn_wq", "attn_wo", "rel_bias", "mlp_w1", "mlp_w2", "ln_mix_g", "ln_mix_b",
             "ln_mlp_g", "ln_mlp_b"]
    return (loss, grad_x, *[res[n_][0] for n_ in order], *[res[n_][1] for n_ in order],
            *[res[n_][2] for n_ in order], *[res[n_][3] for n_ in order])
```

```python
import math

import numpy as np
import jax
import jax.numpy as jnp
from jax import lax
from jax.experimental import pallas as pl
from jax.experimental.pallas import tpu as pltpu

F32 = jnp.float32
MXU_DTYPE = jnp.bfloat16
GRAD_DTYPE = jnp.bfloat16
VMEM_LIMIT_BYTES = 56 * 2**20
LANE = 128
N_DEV = 8
MESH_AXES = ("x", "y", "c")
MESH = pl.DeviceIdType.MESH

HEAD_DIM = 128
BAND = 128
BRANCHES = ((128, 1), (512, 4), (2048, 16))
ATT_TB = BAND * 16
REL_MAX_DIST = 2048
LN_EPS = 1e-5
NEG = -1e30
HALO = 32

ADAM_LR, ADAM_B1, ADAM_B2, ADAM_EPS, ADAM_WD, ADAM_STEP = 0.001, 0.9, 0.999, 1e-08, 0.01, 10

NN = (((1,), (0,)), ((), ()))
NT = (((1,), (1,)), ((), ()))
TN = (((0,), (0,)), ((), ()))


def _cp(sem=None):
    return pltpu.CompilerParams(dimension_semantics=sem, vmem_limit_bytes=VMEM_LIMIT_BYTES)


def _sig(v):
    return 1.0 / (1.0 + jnp.exp(-v))


def _call(body, *, name, grid, in_specs, out_specs, out_shape, args, sem, scratch_shapes=(), comm=None):
    if comm is None:
        res = pl.pallas_call(body, name=name, grid=grid, in_specs=list(in_specs), out_specs=list(out_specs),
                             out_shape=list(out_shape), scratch_shapes=list(scratch_shapes),
                             compiler_params=_cp(sem))(*args)
        return list(res), []
    n_in, n_out, n_sc, nc_in, nc_out = len(in_specs), len(out_specs), len(scratch_shapes), len(comm.arrays), len(comm.out_shapes)

    def wrapped(*refs):
        pos = 0
        parts = []
        for cnt in (n_in, nc_in, n_out, nc_out, n_sc):
            parts.append(refs[pos:pos + cnt])
            pos += cnt
        ins, cin, outs, cout, sc = parts
        csem = refs[pos:]
        pids = [pl.program_id(ax) for ax in range(len(grid))]
        first, last = pids[0] == 0, pids[0] == grid[0] - 1
        for p, g in zip(pids[1:], grid[1:]):
            first = jnp.logical_and(first, p == 0)
            last = jnp.logical_and(last, p == g - 1)

        @pl.when(first)
        def _():
            comm.start(cin, cout, csem)

        body(*ins, *outs, *sc)

        @pl.when(last)
        def _():
            comm.wait(cin, cout, csem)

    res = pl.pallas_call(wrapped, name=name, grid=grid, in_specs=[*in_specs, *[ANY] * nc_in],
                         out_specs=[*out_specs, *[ANY] * nc_out], out_shape=[*out_shape, *comm.out_shapes],
                         scratch_shapes=[*scratch_shapes, *comm.sems],
                         compiler_params=_cp(("arbitrary",) * len(grid)))(*args, *comm.arrays)
    return list(res[:n_out]), list(res[n_out:])


def _mm(name, a, b, *, grid, a_spec, b_spec, dims, k_axis, epilogue, out_shapes, out_specs,
        acc_shape, extra=(), extra_specs=(), exact=False, nsplit=1, comm=None):
    nk = grid[k_axis]
    n_extra, n_out = len(extra), len(out_shapes)

    def dot(av, bv):
        if exact:
            return lax.dot_general(av, bv, dims, precision=lax.Precision.HIGHEST, preferred_element_type=F32)
        return lax.dot_general(av.astype(MXU_DTYPE), bv.astype(MXU_DTYPE), dims, preferred_element_type=F32)

    def body(a_ref, b_ref, *rest):
        ex, outs = rest[:n_extra], rest[n_extra:n_extra + n_out]
        if nsplit == 1:
            part = dot(a_ref[...], b_ref[...])
        else:
            w = a_ref.shape[1] // nsplit
            part = dot(a_ref[:, 0:w], b_ref[0])
            for s in range(1, nsplit):
                part = part + dot(a_ref[:, s * w:(s + 1) * w], b_ref[s])
        if nk == 1:
            epilogue(part, ex, outs)
        else:
            acc = rest[n_extra + n_out]
            k = pl.program_id(k_axis)

            @pl.when(k == 0)
            def _():
                acc[...] = jnp.zeros_like(acc)

            acc[...] += part

            @pl.when(k == nk - 1)
            def _():
                epilogue(acc[...], ex, outs)

    sem = tuple("arbitrary" if ax == k_axis else "parallel" for ax in range(len(grid)))
    outs, couts = _call(body, name=name, grid=grid, in_specs=[a_spec, b_spec, *extra_specs], out_specs=out_specs,
                        out_shape=out_shapes, args=(a, b, *extra), sem=sem,
                        scratch_shapes=[pltpu.VMEM(acc_shape, F32)] if nk > 1 else [], comm=comm)
    return outs if comm is None else (outs, couts)


def _blk(n, want):
    return min(n, want)


MM_BLOCK = 1024
MM_K = 2048


def mm_fwd(name, x, w, *, colsharded, epilogue, outs, rowvec=(), tiles=(), comm=None):
    T, K = x.shape
    bm = _blk(T, MM_BLOCK)
    if colsharded:
        n_s = w.shape[2]
        N = N_DEV * n_s
        bn = _blk(n_s, MM_BLOCK)
        per = n_s // bn
        bk = K
        b_spec = pl.BlockSpec((None, K, bn), lambda i, j, k: (j // per, 0, j % per))
    else:
        N = w.shape[2]
        w = w.reshape(K, N)
        bn = _blk(N, MM_BLOCK)
        bk = _blk(K, MM_K)
        b_spec = pl.BlockSpec((bk, bn), lambda i, j, k: (k, j))
    grid = (T // bm, N // bn, K // bk)
    a_spec = pl.BlockSpec((bm, bk), lambda i, j, k: (i, k))
    tile_spec = pl.BlockSpec((bm, bn), lambda i, j, k: (i, j))
    vec_spec = pl.BlockSpec((1, bn), lambda i, j, k: (0, j))
    return _mm(name, x, w, grid=grid, a_spec=a_spec, b_spec=b_spec, dims=NN, k_axis=2, epilogue=epilogue,
               out_shapes=[jax.ShapeDtypeStruct((T, N), dt) for dt in outs], out_specs=[tile_spec] * len(outs),
               acc_shape=(bm, bn), extra=(*rowvec, *tiles),
               extra_specs=[vec_spec] * len(rowvec) + [tile_spec] * len(tiles), comm=comm)


def mm_dx(name, dy, w, *, colsharded, epilogue, outs, tiles=(), comm=None):
    T, N = dy.shape
    bm = _blk(T, MM_BLOCK)
    if colsharded:
        K, n_s = w.shape[1], w.shape[2]
        bko = _blk(K, MM_BLOCK)
        spk = max(1, min(N_DEV, MM_K // n_s))
        grid = (T // bm, K // bko, N_DEV // spk)
        a_spec = pl.BlockSpec((bm, spk * n_s), lambda i, j, s: (i, s))
        b_spec = pl.BlockSpec((spk, bko, n_s), lambda i, j, s: (s, j, 0))
    else:
        K = w.shape[1] * N_DEV
        w = w.reshape(K, N)
        bko = _blk(K, MM_BLOCK)
        spk = 1
        grid = (T // bm, K // bko, 1)
        a_spec = pl.BlockSpec((bm, N), lambda i, j, s: (i, 0))
        b_spec = pl.BlockSpec((bko, N), lambda i, j, s: (j, 0))
    tile_spec = pl.BlockSpec((bm, bko), lambda i, j, s: (i, j))
    if spk == 1 and colsharded:
        b_spec = pl.BlockSpec((None, bko, n_s), lambda i, j, s: (s, j, 0))
    return _mm(name, dy, w, grid=grid, a_spec=a_spec, b_spec=b_spec, dims=NT, k_axis=2, epilogue=epilogue,
               out_shapes=[jax.ShapeDtypeStruct((T, K), dt) for dt in outs], out_specs=[tile_spec] * len(outs),
               acc_shape=(bm, bko), extra=tuple(tiles), extra_specs=[tile_spec] * len(tiles), nsplit=spk, comm=comm)


def mm_dw(name, x, dy, *, colsharded, comm=None):
    T, K = x.shape
    N = dy.shape[1]
    bt = _blk(T, MM_K)
    bmo = _blk(K, MM_BLOCK)
    if colsharded:
        n_s = N // N_DEV
        bno = _blk(n_s, MM_BLOCK)
        per = n_s // bno
        out_shape = jax.ShapeDtypeStruct((N_DEV, K, n_s), GRAD_DTYPE)
        out_spec = pl.BlockSpec((None, bmo, bno), lambda i, j, t: (j // per, i, j % per))
    else:
        bno = _blk(N, MM_BLOCK)
        out_shape = jax.ShapeDtypeStruct((K, N), GRAD_DTYPE)
        out_spec = pl.BlockSpec((bmo, bno), lambda i, j, t: (i, j))
    grid = (K // bmo, N // bno, T // bt)
    a_spec = pl.BlockSpec((bt, bmo), lambda i, j, t: (t, i))
    b_spec = pl.BlockSpec((bt, bno), lambda i, j, t: (t, j))
    res = _mm(name, x, dy, grid=grid, a_spec=a_spec, b_spec=b_spec, dims=TN, k_axis=2, epilogue=_ep_store,
              out_shapes=[out_shape], out_specs=[out_spec], acc_shape=(bmo, bno), comm=comm)
    (out,), couts = res if comm is not None else (res, [])
    out = out if colsharded else out.reshape(N_DEV, K // N_DEV, N)
    return out if comm is None else (out, couts)


def cast_bf16(name, a, layer=None):
    R, C = a.shape[-2:]
    br = _blk(R, 512)

    def body(a_ref, o_ref):
        o_ref[...] = a_ref[...].astype(MXU_DTYPE)

    spec = pl.BlockSpec((br, C), lambda i: (i, 0))
    in_spec = spec if layer is None else pl.BlockSpec((None, br, C), lambda i: (layer, i, 0))
    return pl.pallas_call(body, name=name, grid=(R // br,), in_specs=[in_spec], out_specs=spec,
                          out_shape=jax.ShapeDtypeStruct((R, C), MXU_DTYPE), compiler_params=_cp(("parallel",)))(a)


def _ln_stats(z):
    mu = jnp.mean(z, axis=-1, keepdims=True)
    zc = z - mu
    var = jnp.mean(zc * zc, axis=-1, keepdims=True)
    return zc * lax.rsqrt(var + LN_EPS)


def ln_fwd(name, z, g, b):
    T, D = z.shape
    br = _blk(T, 512)

    def body(z_ref, g_ref, b_ref, y_ref, yb_ref):
        y = _ln_stats(z_ref[...]) * g_ref[...] + b_ref[...]
        y_ref[...] = y
        yb_ref[...] = y.astype(MXU_DTYPE)

    row = pl.BlockSpec((br, D), lambda i: (i, 0))
    vec = pl.BlockSpec((1, D), lambda i: (0, 0))
    return pl.pallas_call(body, name=name, grid=(T // br,), in_specs=[row, vec, vec], out_specs=[row, row],
                          out_shape=[jax.ShapeDtypeStruct((T, D), F32), jax.ShapeDtypeStruct((T, D), MXU_DTYPE)],
                          compiler_params=_cp(("parallel",)))(z, g, b)


def ln_bwd(name, dy, z, g):
    T, D = z.shape
    br = _blk(T, 512)

    def body(dy_ref, z_ref, g_ref, dz_ref, dzb_ref, dg_ref, db_ref, ds_ref):
        i = pl.program_id(0)
        z = z_ref[...]
        dy = dy_ref[...]
        mu = jnp.mean(z, axis=-1, keepdims=True)
        zc = z - mu
        var = jnp.mean(zc * zc, axis=-1, keepdims=True)
        rstd = lax.rsqrt(var + LN_EPS)
        xhat = zc * rstd
        dxh = dy * g_ref[...]
        m1 = jnp.mean(dxh, axis=-1, keepdims=True)
        m2 = jnp.mean(dxh * xhat, axis=-1, keepdims=True)
        dz = rstd * (dxh - m1 - xhat * m2)
        dz_ref[...] = dz
        dzb_ref[...] = dz.astype(MXU_DTYPE)

        @pl.when(i == 0)
        def _():
            dg_ref[...] = jnp.zeros_like(dg_ref)
            db_ref[...] = jnp.zeros_like(db_ref)
            ds_ref[...] = jnp.zeros_like(ds_ref)

        dg_ref[...] += jnp.sum(dy * xhat, axis=0, keepdims=True)
        db_ref[...] += jnp.sum(dy, axis=0, keepdims=True)
        ds_ref[...] += jnp.sum(dz, axis=0, keepdims=True)

    row = pl.BlockSpec((br, D), lambda i: (i, 0))
    vec = pl.BlockSpec((1, D), lambda i: (0, 0))
    vshape = jax.ShapeDtypeStruct((1, D), F32)
    return pl.pallas_call(body, name=name, grid=(T // br,), in_specs=[row, row, vec],
                          out_specs=[row, row, vec, vec, vec],
                          out_shape=[jax.ShapeDtypeStruct((T, D), F32), jax.ShapeDtypeStruct((T, D), MXU_DTYPE),
                                     vshape, vshape, vshape],
                          compiler_params=_cp(("arbitrary",)))(dy, z, g)


def loss_head(name, y, target):
    T, D = y.shape
    br = _blk(T, 512)
    n = T // br

    def body(y_ref, t_ref, loss_ref, dy_ref, acc_ref):
        i = pl.program_id(0)
        err = y_ref[...] - t_ref[...]
        dy_ref[...] = err * (1.0 / D)

        @pl.when(i == 0)
        def _():
            acc_ref[...] = jnp.zeros_like(acc_ref)

        acc_ref[...] += jnp.sum(err * err, axis=0, keepdims=True)

        @pl.when(i == n - 1)
        def _():
            loss_ref[...] = (0.5 / D) * jnp.sum(acc_ref[...], axis=1, keepdims=True)

    row = pl.BlockSpec((br, D), lambda i: (i, 0))
    return pl.pallas_call(body, name=name, grid=(n,), in_specs=[row, row],
                          out_specs=[pl.BlockSpec((1, 1), lambda i: (0, 0)), row],
                          out_shape=[jax.ShapeDtypeStruct((1, 1), F32), jax.ShapeDtypeStruct((T, D), F32)],
                          scratch_shapes=[pltpu.VMEM((1, D), F32)], compiler_params=_cp(("arbitrary",)))(y, target)


CONV_BT = 128


def conv_fwd(name, h1, dw_w, dw_b, ln_g, ln_b):
    T, D2 = h1.shape
    D = D2 // 2
    W = dw_w.shape[0]
    taps = W - 1
    bt = _blk(T, CONV_BT)
    hb = bt // HALO

    def body(h_ref, hp_ref, w_ref, b_ref, g_ref, be_ref, c_ref, s_ref, ux):
        i = pl.program_id(0)
        up = hp_ref[:, :D] * _sig(hp_ref[:, D:])
        ux[0:HALO, :] = jnp.where(i == 0, 0.0, up)
        ux[HALO:HALO + bt, :] = h_ref[:, :D] * _sig(h_ref[:, D:])
        for cb in range(D // LANE):
            cs = slice(cb * LANE, (cb + 1) * LANE)
            acc = jnp.broadcast_to(b_ref[:, cs], (bt, LANE))
            for k in range(taps):
                acc = acc + w_ref[k:k + 1, cs] * ux[pl.ds(HALO - (taps - 1) + k, bt), cs]
            c_ref[:, cs] = acc
        n = _ln_stats(c_ref[...]) * g_ref[...] + be_ref[...]
        s_ref[...] = (n * _sig(n)).astype(MXU_DTYPE)

    main = pl.BlockSpec((bt, D2), lambda i: (i, 0))
    prev = pl.BlockSpec((HALO, D2), lambda i: (jnp.maximum(i * hb - 1, 0), 0))
    wspec = pl.BlockSpec((W, D), lambda i: (0, 0))
    vec = pl.BlockSpec((1, D), lambda i: (0, 0))
    row = pl.BlockSpec((bt, D), lambda i: (i, 0))
    return pl.pallas_call(body, name=name, grid=(T // bt,), in_specs=[main, prev, wspec, vec, vec, vec],
                          out_specs=[row, row],
                          out_shape=[jax.ShapeDtypeStruct((T, D), F32), jax.ShapeDtypeStruct((T, D), MXU_DTYPE)],
                          scratch_shapes=[pltpu.VMEM((HALO + bt, D), F32)],
                          compiler_params=_cp(("parallel",)))(h1, h1, dw_w, dw_b, ln_g, ln_b)


def conv_bwd_ln(name, ds, c, ln_g, ln_b):
    T, D = c.shape
    br = _blk(T, 512)

    def body(ds_ref, c_ref, g_ref, be_ref, dc_ref, dg_ref, db_ref, dcs_ref):
        i = pl.program_id(0)
        c = c_ref[...]
        mu = jnp.mean(c, axis=-1, keepdims=True)
        cc = c - mu
        var = jnp.mean(cc * cc, axis=-1, keepdims=True)
        rstd = lax.rsqrt(var + LN_EPS)
        xhat = cc * rstd
        n = xhat * g_ref[...] + be_ref[...]
        sg = _sig(n)
        dn = ds_ref[...] * (sg * (1.0 + n * (1.0 - sg)))
        dxh = dn * g_ref[...]
        m1 = jnp.mean(dxh, axis=-1, keepdims=True)
        m2 = jnp.mean(dxh * xhat, axis=-1, keepdims=True)
        dc = rstd * (dxh - m1 - xhat * m2)
        dc_ref[...] = dc

        @pl.when(i == 0)
        def _():
            dg_ref[...] = jnp.zeros_like(dg_ref)
            db_ref[...] = jnp.zeros_like(db_ref)
            dcs_ref[...] = jnp.zeros_like(dcs_ref)

        dg_ref[...] += jnp.sum(dn * xhat, axis=0, keepdims=True)
        db_ref[...] += jnp.sum(dn, axis=0, keepdims=True)
        dcs_ref[...] += jnp.sum(dc, axis=0, keepdims=True)

    row = pl.BlockSpec((br, D), lambda i: (i, 0))
    vec = pl.BlockSpec((1, D), lambda i: (0, 0))
    vshape = jax.ShapeDtypeStruct((1, D), F32)
    return pl.pallas_call(body, name=name, grid=(T // br,), in_specs=[row, row, vec, vec],
                          out_specs=[row, vec, vec, vec],
                          out_shape=[jax.ShapeDtypeStruct((T, D), F32), vshape, vshape, vshape],
                          compiler_params=_cp(("arbitrary",)))(ds, c, ln_g, ln_b)


def conv_bwd_dw(name, dc, h1, dw_w, comm=None):
    T, D2 = h1.shape
    D = D2 // 2
    W = dw_w.shape[0]
    taps = W - 1
    bt = _blk(T, CONV_BT)
    hb = bt // HALO
    n = T // bt

    def body(dc_ref, dcn_ref, h_ref, hp_ref, w_ref, dh_ref, dw_ref, dhs_ref, ux, dcx, du):
        i = pl.program_id(0)
        a = h_ref[:, :D]
        sg = _sig(h_ref[:, D:])
        up = hp_ref[:, :D] * _sig(hp_ref[:, D:])
        ux[0:HALO, :] = jnp.where(i == 0, 0.0, up)
        ux[HALO:HALO + bt, :] = a * sg
        dcx[0:bt, :] = dc_ref[...]
        dcx[bt:bt + HALO, :] = jnp.where(i == n - 1, 0.0, dcn_ref[...])

        @pl.when(i == 0)
        def _():
            dw_ref[...] = jnp.zeros_like(dw_ref)
            dhs_ref[...] = jnp.zeros_like(dhs_ref)

        for cb in range(D // LANE):
            cs = slice(cb * LANE, (cb + 1) * LANE)
            dcb = dcx[0:bt, cs]
            acc = jnp.zeros((bt, LANE), F32)
            for k in range(taps):
                acc = acc + w_ref[k:k + 1, cs] * dcx[pl.ds(taps - 1 - k, bt), cs]
                dw_ref[k:k + 1, cs] += jnp.sum(dcb * ux[pl.ds(HALO - (taps - 1) + k, bt), cs], axis=0, keepdims=True)
            du[:, cs] = acc
        d_u = du[...]
        da = d_u * sg
        dg = d_u * a * sg * (1.0 - sg)
        dh_ref[:, :D] = da.astype(MXU_DTYPE)
        dh_ref[:, D:] = dg.astype(MXU_DTYPE)
        dhs_ref[:, :D] += jnp.sum(da, axis=0, keepdims=True)
        dhs_ref[:, D:] += jnp.sum(dg, axis=0, keepdims=True)

    row = pl.BlockSpec((bt, D), lambda i: (i, 0))
    nxt = pl.BlockSpec((HALO, D), lambda i: (jnp.minimum((i + 1) * hb, T // HALO - 1), 0))
    main = pl.BlockSpec((bt, D2), lambda i: (i, 0))
    prev = pl.BlockSpec((HALO, D2), lambda i: (jnp.maximum(i * hb - 1, 0), 0))
    wspec = pl.BlockSpec((W, D), lambda i: (0, 0))
    return _call(body, name=name, grid=(n,), in_specs=[row, nxt, main, prev, wspec],
                 out_specs=[main, wspec, pl.BlockSpec((1, D2), lambda i: (0, 0))],
                 out_shape=[jax.ShapeDtypeStruct((T, D2), MXU_DTYPE), jax.ShapeDtypeStruct((W, D), F32),
                            jax.ShapeDtypeStruct((1, D2), F32)],
                 scratch_shapes=[pltpu.VMEM((HALO + bt, D), F32), pltpu.VMEM((bt + HALO, D), F32),
                                 pltpu.VMEM((bt, D), F32)],
                 args=(dc, dc, h1, h1, dw_w), sem=("arbitrary",), comm=comm)


def _t5_bucket(dist, n_buckets):
    max_exact = n_buckets // 2
    large = max_exact + (np.log(np.maximum(dist, 1) / max_exact) / math.log(REL_MAX_DIST / max_exact)
                         * (n_buckets - max_exact)).astype(np.int32)
    large = np.minimum(large, n_buckets - 1)
    return np.where(dist < max_exact, dist, large).astype(np.int32)


def _band_tables(n_buckets):
    i = np.arange(BAND)[:, None]
    j = np.arange(2 * BAND)[None, :]
    delta = i - j + BAND
    out = []
    for window, dil in BRANCHES:
        ok = (delta >= 0) & (delta <= window // dil)
        out.append((_t5_bucket(np.clip(delta, 0, None) * dil, n_buckets), ok))
    return out


def _units():
    for bi in (2, 1, 0):
        d = BRANCHES[bi][1]
        for r in range(d):
            for nb in range(ATT_TB // (BAND * d)):
                yield bi, d, r, nb


def _rows(start, size, d):
    return pl.ds(start, size) if d == 1 else pl.ds(start, size, stride=d)


def _bc(v):
    return jnp.broadcast_to(v, (BAND, LANE))


def attn_fwd(name, q, kv, bias):
    T, D = q.shape
    H = D // HEAD_DIM
    TB = ATT_TB
    scale = HEAD_DIM ** -0.5

    def body(q_ref, kc_ref, kp_ref, vc_ref, vp_ref, b_ref, o_ref, l_ref, kx, vx, m_sc, s_sc, a_sc):
        first = pl.program_id(1) == 0
        kx[0:TB, :] = kp_ref[...]
        kx[TB:2 * TB, :] = kc_ref[...]
        vx[0:TB, :] = vp_ref[...]
        vx[TB:2 * TB, :] = vc_ref[...]
        col = lax.broadcasted_iota(jnp.int32, (BAND, 2 * BAND), 1)
        for bi, d, r, nb in _units():
            qs = _rows(nb * BAND * d + r, BAND, d)
            ks = _rows(TB + (nb - 1) * BAND * d + r, 2 * BAND, d)
            qb = q_ref[qs, :].astype(MXU_DTYPE)
            kb = kx[ks, :].astype(MXU_DTYPE)
            vb = vx[ks, :].astype(MXU_DTYPE)
            s = lax.dot_general(qb, kb, NT, preferred_element_type=F32) * scale + b_ref[bi]
            if nb == 0:
                s = jnp.where(jnp.logical_and(first, col < BAND), NEG, s)
            mrow = jnp.max(s, axis=1, keepdims=True)
            if bi == 2:
                p = jnp.exp(s - mrow)
                m_new = _bc(mrow)
                l_new = _bc(jnp.sum(p, axis=1, keepdims=True))
                acc = lax.dot_general(p.astype(MXU_DTYPE), vb, NN, preferred_element_type=F32)
            else:
                m_old = m_sc[qs, :]
                m_new = jnp.maximum(m_old, _bc(mrow))
                alpha = jnp.exp(m_old - m_new)
                p = jnp.exp(s - m_new[:, :1])
                l_new = alpha * s_sc[qs, :] + _bc(jnp.sum(p, axis=1, keepdims=True))
                acc = alpha * a_sc[qs, :] + lax.dot_general(p.astype(MXU_DTYPE), vb, NN, preferred_element_type=F32)
            if bi == 0:
                o_ref[qs, :] = acc / l_new
                l_ref[qs, :] = m_new + jnp.log(l_new)
            else:
                m_sc[qs, :] = m_new
                s_sc[qs, :] = l_new
                a_sc[qs, :] = acc

    cur = lambda off: pl.BlockSpec((TB, HEAD_DIM), lambda h, i: (i, off + h))
    prv = lambda off: pl.BlockSpec((TB, HEAD_DIM), lambda h, i: (jnp.maximum(i - 1, 0), off + h))
    bspec = pl.BlockSpec((3, None, BAND, 2 * BAND), lambda h, i: (0, h, 0, 0))
    sc = lambda rows: pltpu.VMEM((rows, HEAD_DIM), F32)
    return pl.pallas_call(body, name=name, grid=(H, T // TB),
                          in_specs=[cur(0), cur(0), prv(0), cur(H), prv(H), bspec],
                          out_specs=[cur(0), cur(0)],
                          out_shape=[jax.ShapeDtypeStruct((T, D), F32), jax.ShapeDtypeStruct((T, D), F32)],
                          scratch_shapes=[sc(2 * TB), sc(2 * TB), sc(TB), sc(TB), sc(TB)],
                          compiler_params=_cp(("parallel", "parallel")))(q, kv, kv, kv, kv, bias)


def attn_bwd(name, q, kv, bias, o, lse, do, comm=None):
    T, D = q.shape
    H = D // HEAD_DIM
    TB = ATT_TB
    nI = T // TB
    scale = HEAD_DIM ** -0.5

    def body(q_ref, kc_ref, kp_ref, vc_ref, vp_ref, b_ref, o_ref, l_ref, do_ref,
             dq_ref, dk_ref, dv_ref, db_ref, kx, vx, dkx, dvx, ck, cv, dl_sc):
        step = pl.program_id(1)
        first = step == nI - 1
        kx[0:TB, :] = kp_ref[...]
        kx[TB:2 * TB, :] = kc_ref[...]
        vx[0:TB, :] = vp_ref[...]
        vx[TB:2 * TB, :] = vc_ref[...]
        dkx[...] = jnp.zeros_like(dkx)
        dvx[...] = jnp.zeros_like(dvx)
        dq_ref[...] = jnp.zeros_like(dq_ref)
        dl_sc[...] = jnp.broadcast_to(jnp.sum(do_ref[...] * o_ref[...], axis=1, keepdims=True), (TB, LANE))

        @pl.when(step == 0)
        def _():
            db_ref[...] = jnp.zeros_like(db_ref)
            ck[...] = jnp.zeros_like(ck)
            cv[...] = jnp.zeros_like(cv)

        col = lax.broadcasted_iota(jnp.int32, (BAND, 2 * BAND), 1)
        for bi, d, r, nb in _units():
            qs = _rows(nb * BAND * d + r, BAND, d)
            ks = _rows(TB + (nb - 1) * BAND * d + r, 2 * BAND, d)
            qb = q_ref[qs, :].astype(MXU_DTYPE)
            kb = kx[ks, :].astype(MXU_DTYPE)
            vb = vx[ks, :].astype(MXU_DTYPE)
            dob = do_ref[qs, :].astype(MXU_DTYPE)
            s = lax.dot_general(qb, kb, NT, preferred_element_type=F32) * scale + b_ref[bi]
            if nb == 0:
                s = jnp.where(jnp.logical_and(first, col < BAND), NEG, s)
            p = jnp.exp(s - l_ref[qs, :][:, :1])
            dp = lax.dot_general(dob, vb, NT, preferred_element_type=F32)
            dsv = p * (dp - dl_sc[qs, :][:, :1])
            db_ref[bi] += dsv
            dsb = dsv.astype(MXU_DTYPE)
            dvx[ks, :] += lax.dot_general(p.astype(MXU_DTYPE), dob, TN, preferred_element_type=F32)
            dkx[ks, :] += lax.dot_general(dsb, qb, TN, preferred_element_type=F32) * scale
            dq_ref[qs, :] += lax.dot_general(dsb, kb, NN, preferred_element_type=F32) * scale

        dk_ref[...] = dkx[TB:2 * TB, :] + ck[...]
        dv_ref[...] = dvx[TB:2 * TB, :] + cv[...]
        ck[...] = dkx[0:TB, :]
        cv[...] = dvx[0:TB, :]

    blk = lambda h, i: nI - 1 - i
    cur = lambda off: pl.BlockSpec((TB, HEAD_DIM), lambda h, i: (blk(h, i), off + h))
    prv = lambda off: pl.BlockSpec((TB, HEAD_DIM), lambda h, i: (jnp.maximum(blk(h, i) - 1, 0), off + h))
    bspec = pl.BlockSpec((3, None, BAND, 2 * BAND), lambda h, i: (0, h, 0, 0))
    sc = lambda rows: pltpu.VMEM((rows, HEAD_DIM), F32)
    return _call(body, name=name, grid=(H, nI),
                 in_specs=[cur(0), cur(0), prv(0), cur(H), prv(H), bspec, cur(0), cur(0), cur(0)],
                 out_specs=[cur(0), cur(0), cur(0), bspec],
                 out_shape=[jax.ShapeDtypeStruct((T, D), F32), jax.ShapeDtypeStruct((T, D), F32),
                            jax.ShapeDtypeStruct((T, D), F32), jax.ShapeDtypeStruct((3, H, BAND, 2 * BAND), F32)],
                 scratch_shapes=[sc(2 * TB), sc(2 * TB), sc(2 * TB), sc(2 * TB), sc(TB), sc(TB), sc(TB)],
                 args=(q, kv, kv, kv, kv, bias, o, lse, do), sem=("arbitrary", "arbitrary"), comm=comm)


def adamw(name, parts, w, m, v):
    L, R, C = w.shape
    P = parts[0].shape[0]
    br = R if R % 8 else _blk(R, max(8, (1 << 18) // C))
    c1 = 1.0 / (1.0 - ADAM_B1 ** ADAM_STEP)
    c2 = 1.0 / (1.0 - ADAM_B2 ** ADAM_STEP)

    def body(*refs):
        p_refs = refs[:L]
        w_ref, m_ref, v_ref, g_ref, d_ref, nm_ref, nv_ref = refs[L:]
        lay = pl.program_id(0)

        def total(p_ref):
            g = p_ref[0].astype(F32)
            for k in range(1, P):
                g = g + p_ref[k].astype(F32)
            return g

        g = total(p_refs[0])
        for j in range(1, L):
            g = jnp.where(lay == j, total(p_refs[j]), g)
        nm = ADAM_B1 * m_ref[...] + (1.0 - ADAM_B1) * g
        nv = ADAM_B2 * v_ref[...] + (1.0 - ADAM_B2) * (g * g)
        g_ref[...] = g
        nm_ref[...] = nm
        nv_ref[...] = nv
        d_ref[...] = -ADAM_LR * ((nm * c1) / (jnp.sqrt(nv * c2) + ADAM_EPS) + ADAM_WD * w_ref[...])

    row = pl.BlockSpec((None, br, C), lambda l, i: (l, i, 0))
    pspecs = [pl.BlockSpec((P, br, C), lambda l, i, j=j: (0, jnp.where(l == j, i, 0), 0)) for j in range(L)]
    shp = jax.ShapeDtypeStruct((L, R, C), F32)
    return pl.pallas_call(body, name=name, grid=(L, R // br), in_specs=[*pspecs, row, row, row],
                          out_specs=[row] * 4, out_shape=[shp] * 4,
                          compiler_params=_cp(("parallel", "parallel")))(*parts, w, m, v)


def pair_sum(name, g, s1, c_idx):
    _, _, R, C = g.shape
    br = _blk(R, max(16, (1 << 19) // C))

    def body(c_ref, g_ref, s_ref, t_ref):
        t_ref[...] = (g_ref[...].astype(F32) + s_ref[...].astype(F32)).astype(t_ref.dtype)

    return pl.pallas_call(
        body, name=name,
        grid_spec=pltpu.PrefetchScalarGridSpec(
            num_scalar_prefetch=1, grid=(4, R // br),
            in_specs=[pl.BlockSpec((None, None, br, C), lambda j, i, c: (j, c[0], i, 0)),
                      pl.BlockSpec((None, br, C), lambda j, i, c: (j, i, 0))],
            out_specs=pl.BlockSpec((None, br, C), lambda j, i, c: (j, i, 0))),
        out_shape=jax.ShapeDtypeStruct((4, R, C), g.dtype), compiler_params=_cp(("parallel", "parallel")))(c_idx, g, s1)


def _me():
    return lax.axis_index("x"), lax.axis_index("y"), lax.axis_index("c")


ANY = pl.BlockSpec(memory_space=pl.ANY)


def all_gather(name, shards):
    n = len(shards)

    def body(*refs):
        ins, outs = refs[:n], refs[n:2 * n]
        send_sems, recv_sems, local_sems = refs[2 * n:]
        x, y, c = _me()
        sibling = (x, y, 1 - c)
        chips = [(1 - x, y), (x, 1 - y), (1 - x, 1 - y)]

        def slot(px, py, pc):
            return 4 * px + 2 * py + pc

        def copy(a, k, block, to, src=None):
            dst = outs[a].at[slot(*block)]
            return pltpu.make_async_remote_copy(src_ref=dst if src is None else src, dst_ref=dst,
                                                send_sem=send_sems.at[a, k], recv_sem=recv_sems.at[a, k],
                                                device_id=to, device_id_type=MESH)

        mine = [pltpu.make_async_copy(ins[a], outs[a].at[slot(x, y, c)], local_sems.at[a]) for a in range(n)]
        for cp in mine:
            cp.start()
        first = []
        for a in range(n):
            first.append(copy(a, 0, (x, y, c), sibling, src=ins[a]))
            first += [copy(a, 1 + j, (x, y, c), (*chip, c), src=ins[a]) for j, chip in enumerate(chips)]
        for cp in first:
            cp.start()
        passed = []
        for j, chip in enumerate(chips):
            for a in range(n):
                copy(a, 1 + j, (*chip, c), (x, y, c)).wait_recv()
                fwd = copy(a, 4 + j, (*chip, c), sibling)
                fwd.start()
                passed.append(fwd)
        for a in range(n):
            copy(a, 0, (x, y, 1 - c), (x, y, c)).wait_recv()
            for j, chip in enumerate(chips):
                copy(a, 4 + j, (*chip, 1 - c), (x, y, c)).wait_recv()
        for cp in first + passed:
            cp.wait_send()
        for cp in mine:
            cp.wait()

    return pl.pallas_call(
        body, name=name, in_specs=[ANY] * n, out_specs=[ANY] * n,
        out_shape=[jax.ShapeDtypeStruct((N_DEV,) + s.shape, s.dtype) for s in shards],
        scratch_shapes=[pltpu.SemaphoreType.DMA((n, 7)), pltpu.SemaphoreType.DMA((n, 7)), pltpu.SemaphoreType.DMA((n,))],
    )(*shards)


def pair_exchange(name, grads):
    n = len(grads)

    def body(*refs):
        ins, outs = refs[:n], refs[n:2 * n]
        send_sems, recv_sems = refs[2 * n:]
        x, y, c = _me()
        copies = []
        for a in range(n):
            for j in range(4):
                copies.append(pltpu.make_async_remote_copy(
                    src_ref=ins[a].at[j, 1 - c], dst_ref=outs[a].at[j],
                    send_sem=send_sems.at[a, j], recv_sem=recv_sems.at[a, j],
                    device_id=(x, y, 1 - c), device_id_type=MESH))
        for cp in copies:
            cp.start()
        for cp in copies:
            cp.wait()

    return pl.pallas_call(
        body, name=name, in_specs=[ANY] * n, out_specs=[ANY] * n,
        out_shape=[jax.ShapeDtypeStruct((4,) + g.shape[2:], g.dtype) for g in grads],
        scratch_shapes=[pltpu.SemaphoreType.DMA((n, 4)), pltpu.SemaphoreType.DMA((n, 4))],
    )(*grads)


class ChipExchange:
    def __init__(self, sums):
        n = len(sums)
        self.arrays = list(sums)
        self.out_shapes = [jax.ShapeDtypeStruct(s.shape, s.dtype) for s in sums]
        self.sems = [pltpu.SemaphoreType.DMA((n, 3)), pltpu.SemaphoreType.DMA((n, 3)), pltpu.SemaphoreType.DMA((n,))]

    def _copies(self, ins, outs, sems, arrivals):
        send_sems, recv_sems, local_sems = sems
        x, y, c = _me()
        my = 2 * x + y
        chips = [(1 - x, y), (x, 1 - y), (1 - x, 1 - y)]
        mine, sends, recvs = [], [], []
        for a in range(len(ins)):
            mine.append(pltpu.make_async_copy(ins[a].at[my], outs[a].at[my], local_sems.at[a]))
            for j, (px, py) in enumerate(chips):
                sends.append(pltpu.make_async_remote_copy(
                    src_ref=ins[a].at[2 * px + py], dst_ref=outs[a].at[my],
                    send_sem=send_sems.at[a, j], recv_sem=recv_sems.at[a, j],
                    device_id=(px, py, c), device_id_type=MESH))
                if arrivals:
                    recvs.append(pltpu.make_async_remote_copy(
                        src_ref=ins[a].at[my], dst_ref=outs[a].at[2 * px + py],
                        send_sem=send_sems.at[a, j], recv_sem=recv_sems.at[a, j],
                        device_id=(px, py, c), device_id_type=MESH))
        return mine, sends, recvs

    def start(self, ins, outs, sems):
        mine, sends, _ = self._copies(ins, outs, sems, arrivals=False)
        for cp in mine + sends:
            cp.start()

    def wait(self, ins, outs, sems):
        mine, sends, recvs = self._copies(ins, outs, sems, arrivals=True)
        for cp in recvs:
            cp.wait_recv()
        for cp in sends:
            cp.wait_send()
        for cp in mine:
            cp.wait()


def all_reduce_small(name, pack):
    R, C = pack.shape

    def body(p_ref, o_ref, buf, send_sems, recv_sems):
        x, y, c = _me()
        me = 4 * x + 2 * y + c
        buf[me] = p_ref[...]
        copies = []
        for k in range(1, N_DEV):
            px, py, pc = x ^ (k >> 2), y ^ ((k >> 1) & 1), c ^ (k & 1)
            copies.append(pltpu.make_async_remote_copy(
                src_ref=p_ref, dst_ref=buf.at[me], send_sem=send_sems.at[k - 1], recv_sem=recv_sems.at[k - 1],
                device_id=(px, py, pc), device_id_type=MESH))
        for cp in copies:
            cp.start()
        for cp in copies:
            cp.wait()
        acc = buf[0]
        for d in range(1, N_DEV):
            acc = acc + buf[d]
        o_ref[...] = acc

    vm = pl.BlockSpec(memory_space=pltpu.VMEM)
    return pl.pallas_call(
        body, name=name, in_specs=[vm], out_specs=vm, out_shape=jax.ShapeDtypeStruct((R, C), F32),
        scratch_shapes=[pltpu.VMEM((N_DEV, R, C), F32), pltpu.SemaphoreType.DMA((N_DEV - 1,)),
                        pltpu.SemaphoreType.DMA((N_DEV - 1,))],
    )(pack)


def _ep_bias(acc, ex, outs):
    outs[0][...] = acc + ex[0][...]


def _ep_store(acc, ex, outs):
    outs[0][...] = acc.astype(outs[0].dtype)


def _ep_resid(alpha, bias):
    def ep(acc, ex, outs):
        if bias:
            outs[0][...] = alpha * ex[1][...] + (acc + ex[0][...])
        else:
            outs[0][...] = alpha * ex[0][...] + acc
    return ep


def _ep_relu2(acc, ex, outs):
    r = jnp.maximum(acc, 0.0)
    outs[0][...] = (r * r).astype(outs[0].dtype)
    outs[1][...] = r


def _ep_relu2_bwd(acc, ex, outs):
    outs[0][...] = (acc * (2.0 * ex[0][...])).astype(outs[0].dtype)


def _ep_add(acc, ex, outs):
    outs[0][...] = ex[0][...] + acc


def kernel(x, conv_pw1_w, conv_pw1_b, conv_dw_w, conv_dw_b, conv_ln_g, conv_ln_b, conv_pw2_w, conv_pw2_b, w_kv, attn_wq, attn_wo, rel_bias, mlp_w1, mlp_w2, ln_mix_g, ln_mix_b, ln_mlp_g, ln_mlp_b, loss_target, m_conv_pw1_w, m_conv_pw1_b, m_conv_dw_w, m_conv_dw_b, m_conv_ln_g, m_conv_ln_b, m_conv_pw2_w, m_conv_pw2_b, m_w_kv, m_attn_wq, m_attn_wo, m_rel_bias, m_mlp_w1, m_mlp_w2, m_ln_mix_g, m_ln_mix_b, m_ln_mlp_g, m_ln_mlp_b, v_conv_pw1_w, v_conv_pw1_b, v_conv_dw_w, v_conv_dw_b, v_conv_ln_g, v_conv_ln_b, v_conv_pw2_w, v_conv_pw2_b, v_w_kv, v_attn_wq, v_attn_wo, v_rel_bias, v_mlp_w1, v_mlp_w2, v_ln_mix_g, v_ln_mix_b, v_ln_mlp_g, v_ln_mlp_b):
    T, D = x.shape[1], x.shape[2]
    H = D // HEAD_DIM
    depth = mlp_w1.shape[0]
    assert depth == 2 and T % ATT_TB == 0
    alpha = (2 * depth) ** 0.25
    ds_ = D // N_DEV
    xi, yi, ci = _me()
    me = 4 * xi + 2 * yi + ci
    c_idx = ci.astype(jnp.int32).reshape(1)
    x2 = x.reshape(T, D)
    target = loss_target.reshape(T, D)
    n_buckets = rel_bias.shape[0]
    taps = conv_dw_w.shape[1]

    local = {
        "pw1": (conv_pw1_w, 0), "pw2": (conv_pw2_w, 0), "wkv": (w_kv, None), "wq": (attn_wq, 0), "wo": (attn_wo, 0),
        "w1_0": (mlp_w1, 0), "w1_1": (mlp_w1, 1), "w2_0": (mlp_w2, 0), "w2_1": (mlp_w2, 1),
    }
    names = list(local)
    small = jnp.concatenate([conv_dw_w[0], conv_dw_b, conv_ln_g, conv_ln_b, conv_pw2_b,
                             conv_pw1_b.reshape(2, ds_)], axis=0)
    small = jnp.pad(small, ((0, (-small.shape[0]) % 8), (0, 0)))
    gathered = all_gather("gather_weights", [cast_bf16("cast_" + k, *local[k]) for k in names] + [small])
    G = dict(zip(names, gathered[:-1]))
    sm = jnp.transpose(gathered[-1], (1, 0, 2)).reshape(small.shape[0], D)
    dw_w = jnp.pad(sm[:taps], ((0, HALO - taps), (0, 0)))
    dw_b, cln_g, cln_b, pw2_b = (sm[taps + k:taps + k + 1] for k in range(4))
    pw1_b = gathered[-1][:, taps + 4:taps + 6, :].reshape(1, 2 * D)

    row = lambda a, l: a[l:l + 1]

    x_b = cast_bf16("cast_x", x2)
    (h1,) = mm_fwd("pw1", x_b, G["pw1"], colsharded=True, epilogue=_ep_bias, outs=[F32], rowvec=[pw1_b])
    c_pre, s_b = conv_fwd("conv_fwd", h1, dw_w, dw_b, cln_g, cln_b)
    (z1,) = mm_fwd("pw2", s_b, G["pw2"], colsharded=False, epilogue=_ep_resid(alpha, True), outs=[F32],
                   rowvec=[pw2_b], tiles=[x2])
    x1, x1_b = ln_fwd("ln_mix0", z1, row(ln_mix_g, 0), row(ln_mix_b, 0))

    def mlp_fwd(l, xin, xin_b):
        act_b, r = mm_fwd(f"mlp_up{l}", xin_b, G[f"w1_{l}"], colsharded=True, epilogue=_ep_relu2, outs=[MXU_DTYPE, F32])
        (z,) = mm_fwd(f"mlp_down{l}", act_b, G[f"w2_{l}"], colsharded=False, epilogue=_ep_resid(alpha, False),
                      outs=[F32], tiles=[xin])
        y, y_b = ln_fwd(f"ln_mlp{l}", z, row(ln_mlp_g, l), row(ln_mlp_b, l))
        return act_b, r, z, y, y_b

    act0_b, r0, z2, x2_, x2_b = mlp_fwd(0, x1, x1_b)

    (kv,) = mm_fwd("kv_proj", x2_b, G["wkv"], colsharded=True, epilogue=_ep_store, outs=[F32])
    (q,) = mm_fwd("q_proj", x2_b, G["wq"], colsharded=False, epilogue=_ep_store, outs=[F32])
    tables = _band_tables(n_buckets)
    onehot = jnp.concatenate([(jnp.arange(n_buckets)[:, None] == jnp.asarray(bucket).reshape(1, -1)).astype(F32)
                              for bucket, _ in tables], axis=1)
    KB = onehot.shape[1] // 8
    (btab,) = _mm("rel_bias_table", rel_bias, onehot, grid=(1, 8, 1),
                  a_spec=pl.BlockSpec((n_buckets, H), lambda i, j, k: (0, 0)),
                  b_spec=pl.BlockSpec((n_buckets, KB), lambda i, j, k: (0, j)), dims=TN, k_axis=2,
                  epilogue=_ep_store, out_shapes=[jax.ShapeDtypeStruct((H, 8 * KB), F32)],
                  out_specs=[pl.BlockSpec((H, KB), lambda i, j, k: (0, j))], acc_shape=(H, KB), exact=True)
    band_ok = jnp.asarray(np.stack([ok for _, ok in tables]))[:, None]
    bias = jnp.where(band_ok, jnp.transpose(btab.reshape(H, 3, BAND, 2 * BAND), (1, 0, 2, 3)), NEG)
    o, lse = attn_fwd("attn_fwd", q, kv, bias)
    (z3,) = mm_fwd("o_proj", o, G["wo"], colsharded=False, epilogue=_ep_resid(alpha, False), outs=[F32], tiles=[x2_])
    x3, x3_b = ln_fwd("ln_mix1", z3, row(ln_mix_g, 1), row(ln_mix_b, 1))
    act1_b, r1, z4, x4, _ = mlp_fwd(1, x3, x3_b)

    loss_local, dy = loss_head("loss", x4, target)
    loss = lax.psum(loss_local[0, 0], MESH_AXES)

    parts = {}

    def reduce_start(tag, grads):
        g4 = [g.reshape((4, 2) + g.shape[1:]) for g in grads.values()]
        from_sibling = pair_exchange("grad_pair_exchange_" + tag, g4)
        return ChipExchange([pair_sum("grad_pair_sum_" + k, g, s, c_idx) for k, g, s in zip(grads, g4, from_sibling)])

    def reduce_done(grads, delivered):
        parts.update(zip(grads, delivered))

    def mlp_bwd(l, dy_out, z, xin_b, act_b, r, comm=None):
        dz, dz_b, dg, db, _ = ln_bwd(f"ln_mlp{l}_bwd", dy_out, z, row(ln_mlp_g, l))
        res_ = mm_dx(f"mlp_down{l}_dx", dz_b, G[f"w2_{l}"], colsharded=False, epilogue=_ep_relu2_bwd,
                     outs=[MXU_DTYPE], tiles=[r], comm=comm)
        ((dhm_b,), delivered) = res_ if comm is not None else (res_, [])
        dw2 = mm_dw(f"mlp_down{l}_dw", act_b, dz_b, colsharded=False)
        (dxin,) = mm_dx(f"mlp_up{l}_dx", dhm_b, G[f"w1_{l}"], colsharded=True, epilogue=_ep_resid(alpha, False),
                        outs=[F32], tiles=[dz])
        dw1 = mm_dw(f"mlp_up{l}_dw", xin_b, dhm_b, colsharded=True)
        return dxin, dw1, dw2, dg, db, delivered

    dx3, dw1_1, dw2_1, dg_mlp1, db_mlp1, _ = mlp_bwd(1, dy, z4, x3_b, act1_b, r1)
    grads1 = {"w1_1": dw1_1, "w2_1": dw2_1}
    comm1 = reduce_start("mlp1", grads1)

    dz3, dz3_b, dg_mix1, db_mix1, _ = ln_bwd("ln_mix1_bwd", dx3, z3, row(ln_mix_g, 1))
    (do,) = mm_dx("o_proj_dx", dz3_b, G["wo"], colsharded=False, epilogue=_ep_store, outs=[F32])
    dwo = mm_dw("o_proj_dw", o, dz3_b, colsharded=False)
    (dq, dk, dv, dbias), delivered = attn_bwd("attn_bwd", q, kv, bias, o, lse, do, comm=comm1)
    reduce_done(grads1, delivered)
    dkv = jnp.concatenate([dk, dv], axis=1)
    (dx2a,) = mm_dx("q_proj_dx", dq, G["wq"], colsharded=False, epilogue=_ep_resid(alpha, False), outs=[F32], tiles=[dz3])
    (dx2,) = mm_dx("kv_proj_dx", dkv, G["wkv"], colsharded=True, epilogue=_ep_add, outs=[F32], tiles=[dx2a])
    dwq = mm_dw("q_proj_dw", x2_b, dq, colsharded=False)
    dwkv = mm_dw("kv_proj_dw", x2_b, dkv, colsharded=True)
    grads2 = {"wo": dwo, "wq": dwq, "wkv": dwkv}
    comm2 = reduce_start("attn", grads2)

    dbias2 = jnp.transpose(dbias, (1, 0, 2, 3)).reshape(H, -1)
    (drel_t,) = _mm("rel_bias_grad", dbias2, onehot, grid=(1, 1, 8),
                    a_spec=pl.BlockSpec((H, KB), lambda i, j, k: (0, k)),
                    b_spec=pl.BlockSpec((n_buckets, KB), lambda i, j, k: (0, k)), dims=NT, k_axis=2,
                    epilogue=_ep_store, out_shapes=[jax.ShapeDtypeStruct((H, n_buckets), F32)],
                    out_specs=[pl.BlockSpec((H, n_buckets), lambda i, j, k: (0, 0))], acc_shape=(H, n_buckets), exact=True)

    dx1, dw1_0, dw2_0, dg_mlp0, db_mlp0, delivered = mlp_bwd(0, dx2, z2, x1_b, act0_b, r0, comm=comm2)
    reduce_done(grads2, delivered)

    dz1, dz1_b, dg_mix0, db_mix0, dpw2_b = ln_bwd("ln_mix0_bwd", dx1, z1, row(ln_mix_g, 0))
    (ds,) = mm_dx("pw2_dx", dz1_b, G["pw2"], colsharded=False, epilogue=_ep_store, outs=[F32])
    dwpw2 = mm_dw("pw2_dw", s_b, dz1_b, colsharded=False)
    grads3 = {"w1_0": dw1_0, "w2_0": dw2_0, "pw2": dwpw2}
    comm3 = reduce_start("mlp0", grads3)
    dc, dcln_g, dcln_b, ddw_b = conv_bwd_ln("conv_bwd_ln", ds, c_pre, cln_g, cln_b)
    (dh1_b, ddw_w, dpw1_b), delivered = conv_bwd_dw("conv_bwd_dw", dc, h1, dw_w, comm=comm3)
    reduce_done(grads3, delivered)
    grads4 = {"pw1": mm_dw("pw1_dw", x_b, dh1_b, colsharded=True)}
    (dx,), delivered = mm_dx("pw1_dx", dh1_b, G["pw1"], colsharded=True, epilogue=_ep_resid(alpha, False), outs=[F32],
                             tiles=[dz1], comm=reduce_start("pw1", grads4))
    reduce_done(grads4, delivered)
    grad_x = dx.reshape(1, T, D)

    vec_rows = [dg_mix0, dg_mix1, db_mix0, db_mix1, dg_mlp0, dg_mlp1, db_mlp0, db_mlp1,
                ddw_b, dcln_g, dcln_b, dpw2_b, dpw1_b.reshape(2, D), ddw_w[:taps],
                jnp.pad(jnp.transpose(drel_t).reshape(1, -1), ((0, 0), (0, D - H * n_buckets)))]
    pack = jnp.concatenate(vec_rows, axis=0)
    pack = jnp.pad(pack, ((0, (-pack.shape[0]) % 8), (0, 0)))
    tot = all_reduce_small("grad_small_all_reduce", pack)

    def mine(rows):
        return lax.dynamic_slice_in_dim(rows, me * ds_, ds_, axis=1)

    g_ln_mix_g, g_ln_mix_b, g_ln_mlp_g, g_ln_mlp_b = tot[0:2], tot[2:4], tot[4:6], tot[6:8]
    g_dw_b, g_cln_g, g_cln_b, g_pw2_b = (mine(tot[8 + k:9 + k]) for k in range(4))
    g_pw1_b = lax.dynamic_slice_in_dim(tot[12:14].reshape(1, 2 * D), me * 2 * ds_, 2 * ds_, axis=1)
    g_dw_w = mine(tot[14:14 + taps])
    g_rel = tot[14 + taps, :H * n_buckets].reshape(n_buckets, H)

    res = {}

    def upd(nm, parts_, w, m, v):
        shp = w.shape
        if not isinstance(parts_, list):
            parts_ = [parts_]
        parts_ = [p[None] if p.ndim == 2 else p for p in parts_]
        w3, m3, v3 = (a.reshape((len(parts_),) + parts_[0].shape[1:]) for a in (w, m, v))
        outs = adamw("adamw_" + nm, parts_, w3, m3, v3)
        res[nm] = tuple(o_.reshape(shp) for o_ in outs)

    upd("conv_pw1_w", parts["pw1"], conv_pw1_w, m_conv_pw1_w, v_conv_pw1_w)
    upd("conv_pw1_b", g_pw1_b, conv_pw1_b, m_conv_pw1_b, v_conv_pw1_b)
    upd("conv_dw_w", g_dw_w, conv_dw_w, m_conv_dw_w, v_conv_dw_w)
    upd("conv_dw_b", g_dw_b, conv_dw_b, m_conv_dw_b, v_conv_dw_b)
    upd("conv_ln_g", g_cln_g, conv_ln_g, m_conv_ln_g, v_conv_ln_g)
    upd("conv_ln_b", g_cln_b, conv_ln_b, m_conv_ln_b, v_conv_ln_b)
    upd("conv_pw2_w", parts["pw2"], conv_pw2_w, m_conv_pw2_w, v_conv_pw2_w)
    upd("conv_pw2_b", g_pw2_b, conv_pw2_b, m_conv_pw2_b, v_conv_pw2_b)
    upd("w_kv", parts["wkv"], w_kv, m_w_kv, v_w_kv)
    upd("attn_wq", parts["wq"], attn_wq, m_attn_wq, v_attn_wq)
    upd("attn_wo", parts["wo"], attn_wo, m_attn_wo, v_attn_wo)
    upd("rel_bias", g_rel, rel_bias, m_rel_bias, v_rel_bias)
    upd("mlp_w1", [parts["w1_0"], parts["w1_1"]], mlp_w1, m_mlp_w1, v_mlp_w1)
    upd("mlp_w2", [parts["w2_0"], parts["w2_1"]], mlp_w2, m_mlp_w2, v_mlp_w2)
    upd("ln_mix_g", g_ln_mix_g, ln_mix_g, m_ln_mix_g, v_ln_mix_g)
    upd("ln_mix_b", g_ln_mix_b, ln_mix_b, m_ln_mix_b, v_ln_mix_b)
    upd("ln_mlp_g", g_ln_mlp_g, ln_mlp_g, m_ln_mlp_g, v_ln_mlp_g)
    upd("ln_mlp_b", g_ln_mlp_b, ln_mlp_b, m_ln_mlp_b, v_ln_mlp_b)

    order = ["conv_pw1_w", "conv_pw1_b", "conv_dw_w", "conv_dw_b", "conv_ln_g", "conv_ln_b", "conv_pw2_w",
             "conv_pw2_b", "w_kv", "attn_wq", "attn_wo", "rel_bias", "mlp_w1", "mlp_w2", "ln_mix_g", "ln_mix_b",
             "ln_mlp_g", "ln_mlp_b"]
    return (loss, grad_x, *[res[n_][0] for n_ in order], *[res[n_][1] for n_ in order],
            *[res[n_][2] for n_ in order], *[res[n_][3] for n_ in order])
```

```python
import math

import numpy as np
import jax
import jax.numpy as jnp
from jax import lax
from jax.experimental import pallas as pl
from jax.experimental.pallas import tpu as pltpu

F32 = jnp.float32
MXU_DTYPE = jnp.bfloat16
GRAD_DTYPE = jnp.bfloat16
VMEM_LIMIT_BYTES = 56 * 2**20
LANE = 128
N_DEV = 8
MESH_AXES = ("x", "y", "c")
MESH = pl.DeviceIdType.MESH

HEAD_DIM = 128
BAND = 128
BRANCHES = ((128, 1), (512, 4), (2048, 16))
ATT_TB = BAND * 16
REL_MAX_DIST = 2048
LN_EPS = 1e-5
NEG = -1e30
HALO = 32

ADAM_LR, ADAM_B1, ADAM_B2, ADAM_EPS, ADAM_WD, ADAM_STEP = 0.001, 0.9, 0.999, 1e-08, 0.01, 10

NN = (((1,), (0,)), ((), ()))
NT = (((1,), (1,)), ((), ()))
TN = (((0,), (0,)), ((), ()))


def _cp(sem=None):
    return pltpu.CompilerParams(dimension_semantics=sem, vmem_limit_bytes=VMEM_LIMIT_BYTES)


def _sig(v):
    return 1.0 / (1.0 + jnp.exp(-v))


def _call(body, *, name, grid, in_specs, out_specs, out_shape, args, sem, scratch_shapes=(), comm=None):
    if comm is None:
        res = pl.pallas_call(body, name=name, grid=grid, in_specs=list(in_specs), out_specs=list(out_specs),
                             out_shape=list(out_shape), scratch_shapes=list(scratch_shapes),
                             compiler_params=_cp(sem))(*args)
        return list(res), []
    n_in, n_out, n_sc, nc_in, nc_out = len(in_specs), len(out_specs), len(scratch_shapes), len(comm.arrays), len(comm.out_shapes)

    def wrapped(*refs):
        pos = 0
        parts = []
        for cnt in (n_in, nc_in, n_out, nc_out, n_sc):
            parts.append(refs[pos:pos + cnt])
            pos += cnt
        ins, cin, outs, cout, sc = parts
        csem = refs[pos:]
        step = pl.program_id(0)
        for ax in range(1, len(grid)):
            step = step * grid[ax] + pl.program_id(ax)
        n_steps = math.prod(grid)

        @pl.when(step == 0)
        def _():
            comm.start(cin, cout, csem)

        if comm.relay_at is not None:
            @pl.when(step == min(n_steps - 1, int(n_steps * comm.relay_at)))
            def _():
                comm.relay(cin, cout, csem)

        body(*ins, *outs, *sc)

        @pl.when(step == n_steps - 1)
        def _():
            comm.wait(cin, cout, csem)

    res = pl.pallas_call(wrapped, name=name, grid=grid, in_specs=[*in_specs, *[ANY] * nc_in],
                         out_specs=[*out_specs, *[ANY] * nc_out], out_shape=[*out_shape, *comm.out_shapes],
                         scratch_shapes=[*scratch_shapes, *comm.sems],
                         compiler_params=_cp(("arbitrary",) * len(grid)))(*args, *comm.arrays)
    return list(res[:n_out]), list(res[n_out:])


def _mm(name, a, b, *, grid, a_spec, b_spec, dims, k_axis, epilogue, out_shapes, out_specs,
        acc_shape, extra=(), extra_specs=(), exact=False, nsplit=1, ncat=1, comm=None):
    nk = grid[k_axis]
    n_extra, n_out = len(extra), len(out_shapes)

    def dot(av, bv):
        if exact:
            return lax.dot_general(av, bv, dims, precision=lax.Precision.HIGHEST, preferred_element_type=F32)
        return lax.dot_general(av.astype(MXU_DTYPE), bv.astype(MXU_DTYPE), dims, preferred_element_type=F32)

    def body(a_ref, b_ref, *rest):
        ex, outs = rest[:n_extra], rest[n_extra:n_extra + n_out]
        if ncat > 1:
            part = jnp.concatenate([dot(a_ref[...], b_ref[g]) for g in range(ncat)], axis=1)
        elif nsplit == 1:
            part = dot(a_ref[...], b_ref[...])
        else:
            w = a_ref.shape[1] // nsplit
            part = dot(a_ref[:, 0:w], b_ref[0])
            for s in range(1, nsplit):
                part = part + dot(a_ref[:, s * w:(s + 1) * w], b_ref[s])
        if nk == 1:
            epilogue(part, ex, outs)
        else:
            acc = rest[n_extra + n_out]
            k = pl.program_id(k_axis)

            @pl.when(k == 0)
            def _():
                acc[...] = jnp.zeros_like(acc)

            acc[...] += part

            @pl.when(k == nk - 1)
            def _():
                epilogue(acc[...], ex, outs)

    sem = tuple("arbitrary" if ax == k_axis else "parallel" for ax in range(len(grid)))
    outs, couts = _call(body, name=name, grid=grid, in_specs=[a_spec, b_spec, *extra_specs], out_specs=out_specs,
                        out_shape=out_shapes, args=(a, b, *extra), sem=sem,
                        scratch_shapes=[pltpu.VMEM(acc_shape, F32)] if nk > 1 else [], comm=comm)
    return outs if comm is None else (outs, couts)


def _blk(n, want):
    return min(n, want)


MM_BLOCK = 1024
MM_K = 2048


def mm_fwd(name, x, w, *, colsharded, epilogue, outs, rowvec=(), tiles=(), comm=None):
    T, K = x.shape
    bm = _blk(T, MM_BLOCK)
    ncat = 1
    if colsharded:
        n_s = w.shape[2]
        N = N_DEV * n_s
        bk = K
        if n_s < MM_BLOCK and (N_DEV * n_s) % MM_BLOCK == 0:
            ncat = MM_BLOCK // n_s
            bn = MM_BLOCK
            b_spec = pl.BlockSpec((ncat, K, n_s), lambda i, j, k: (j, 0, 0))
        else:
            bn = _blk(n_s, MM_BLOCK)
            per = n_s // bn
            b_spec = pl.BlockSpec((None, K, bn), lambda i, j, k: (j // per, 0, j % per))
    else:
        N = w.shape[2]
        w = w.reshape(K, N)
        bn = _blk(N, MM_BLOCK)
        bk = _blk(K, MM_K)
        b_spec = pl.BlockSpec((bk, bn), lambda i, j, k: (k, j))
    grid = (T // bm, N // bn, K // bk)
    a_spec = pl.BlockSpec((bm, bk), lambda i, j, k: (i, k))
    tile_spec = pl.BlockSpec((bm, bn), lambda i, j, k: (i, j))
    vec_spec = pl.BlockSpec((1, bn), lambda i, j, k: (0, j))
    return _mm(name, x, w, grid=grid, a_spec=a_spec, b_spec=b_spec, dims=NN, k_axis=2, epilogue=epilogue,
               out_shapes=[jax.ShapeDtypeStruct((T, N), dt) for dt in outs], out_specs=[tile_spec] * len(outs),
               acc_shape=(bm, bn), extra=(*rowvec, *tiles),
               extra_specs=[vec_spec] * len(rowvec) + [tile_spec] * len(tiles), ncat=ncat, comm=comm)


def mm_dx(name, dy, w, *, colsharded, epilogue, outs, tiles=(), comm=None):
    T, N = dy.shape
    bm = _blk(T, MM_BLOCK)
    if colsharded:
        K, n_s = w.shape[1], w.shape[2]
        bko = _blk(K, MM_BLOCK)
        spk = max(1, min(N_DEV, MM_K // n_s))
        grid = (T // bm, K // bko, N_DEV // spk)
        a_spec = pl.BlockSpec((bm, spk * n_s), lambda i, j, s: (i, s))
        b_spec = pl.BlockSpec((spk, bko, n_s), lambda i, j, s: (s, j, 0))
    else:
        K = w.shape[1] * N_DEV
        w = w.reshape(K, N)
        bko = _blk(K, MM_BLOCK)
        spk = 1
        grid = (T // bm, K // bko, 1)
        a_spec = pl.BlockSpec((bm, N), lambda i, j, s: (i, 0))
        b_spec = pl.BlockSpec((bko, N), lambda i, j, s: (j, 0))
    tile_spec = pl.BlockSpec((bm, bko), lambda i, j, s: (i, j))
    if spk == 1 and colsharded:
        b_spec = pl.BlockSpec((None, bko, n_s), lambda i, j, s: (s, j, 0))
    return _mm(name, dy, w, grid=grid, a_spec=a_spec, b_spec=b_spec, dims=NT, k_axis=2, epilogue=epilogue,
               out_shapes=[jax.ShapeDtypeStruct((T, K), dt) for dt in outs], out_specs=[tile_spec] * len(outs),
               acc_shape=(bm, bko), extra=tuple(tiles), extra_specs=[tile_spec] * len(tiles), nsplit=spk, comm=comm)


def mm_dw(name, x, dy, *, colsharded, comm=None):
    T, K = x.shape
    N = dy.shape[1]
    narrow = x.dtype.itemsize == 2 and dy.dtype.itemsize == 2
    bt = _blk(T, 2 * MM_K if narrow else MM_K)
    bmo = _blk(K, MM_BLOCK)
    epilogue = _ep_store
    if colsharded:
        n_s = N // N_DEV
        out_shape = jax.ShapeDtypeStruct((N_DEV, K, n_s), GRAD_DTYPE)
        if n_s < MM_BLOCK and N % MM_BLOCK == 0:
            group = MM_BLOCK // n_s
            bno = MM_BLOCK
            out_spec = pl.BlockSpec((group, bmo, n_s), lambda i, j, t: (j, i, 0))

            def epilogue(acc, ex, outs):
                for g in range(group):
                    outs[0][g] = acc[:, g * n_s:(g + 1) * n_s].astype(outs[0].dtype)
        else:
            bno = _blk(n_s, MM_BLOCK)
            per = n_s // bno
            out_spec = pl.BlockSpec((None, bmo, bno), lambda i, j, t: (j // per, i, j % per))
    else:
        bno = _blk(N, MM_BLOCK)
        out_shape = jax.ShapeDtypeStruct((K, N), GRAD_DTYPE)
        out_spec = pl.BlockSpec((bmo, bno), lambda i, j, t: (i, j))
    grid = (K // bmo, N // bno, T // bt)
    a_spec = pl.BlockSpec((bt, bmo), lambda i, j, t: (t, i))
    b_spec = pl.BlockSpec((bt, bno), lambda i, j, t: (t, j))
    res = _mm(name, x, dy, grid=grid, a_spec=a_spec, b_spec=b_spec, dims=TN, k_axis=2, epilogue=epilogue,
              out_shapes=[out_shape], out_specs=[out_spec], acc_shape=(bmo, bno), comm=comm)
    (out,), couts = res if comm is not None else (res, [])
    out = out if colsharded else out.reshape(N_DEV, K // N_DEV, N)
    return out if comm is None else (out, couts)


def cast_bf16(name, a, layer=None):
    R, C = a.shape[-2:]
    br = _blk(R, 512)

    def body(a_ref, o_ref):
        o_ref[...] = a_ref[...].astype(MXU_DTYPE)

    spec = pl.BlockSpec((br, C), lambda i: (i, 0))
    in_spec = spec if layer is None else pl.BlockSpec((None, br, C), lambda i: (layer, i, 0))
    return pl.pallas_call(body, name=name, grid=(R // br,), in_specs=[in_spec], out_specs=spec,
                          out_shape=jax.ShapeDtypeStruct((R, C), MXU_DTYPE), compiler_params=_cp(("parallel",)))(a)


def _ln_stats(z):
    mu = jnp.mean(z, axis=-1, keepdims=True)
    zc = z - mu
    var = jnp.mean(zc * zc, axis=-1, keepdims=True)
    return zc * lax.rsqrt(var + LN_EPS)


def ln_fwd(name, z, g, b):
    T, D = z.shape
    br = _blk(T, 512)

    def body(z_ref, g_ref, b_ref, y_ref, yb_ref):
        y = _ln_stats(z_ref[...]) * g_ref[...] + b_ref[...]
        y_ref[...] = y
        yb_ref[...] = y.astype(MXU_DTYPE)

    row = pl.BlockSpec((br, D), lambda i: (i, 0))
    vec = pl.BlockSpec((1, D), lambda i: (0, 0))
    return pl.pallas_call(body, name=name, grid=(T // br,), in_specs=[row, vec, vec], out_specs=[row, row],
                          out_shape=[jax.ShapeDtypeStruct((T, D), F32), jax.ShapeDtypeStruct((T, D), MXU_DTYPE)],
                          compiler_params=_cp(("parallel",)))(z, g, b)


def ln_bwd(name, dy, z, g):
    T, D = z.shape
    br = _blk(T, 512)

    def body(dy_ref, z_ref, g_ref, dz_ref, dzb_ref, dg_ref, db_ref, ds_ref):
        i = pl.program_id(0)
        z = z_ref[...]
        dy = dy_ref[...]
        mu = jnp.mean(z, axis=-1, keepdims=True)
        zc = z - mu
        var = jnp.mean(zc * zc, axis=-1, keepdims=True)
        rstd = lax.rsqrt(var + LN_EPS)
        xhat = zc * rstd
        dxh = dy * g_ref[...]
        m1 = jnp.mean(dxh, axis=-1, keepdims=True)
        m2 = jnp.mean(dxh * xhat, axis=-1, keepdims=True)
        dz = rstd * (dxh - m1 - xhat * m2)
        dz_ref[...] = dz
        dzb_ref[...] = dz.astype(MXU_DTYPE)

        @pl.when(i == 0)
        def _():
            dg_ref[...] = jnp.zeros_like(dg_ref)
            db_ref[...] = jnp.zeros_like(db_ref)
            ds_ref[...] = jnp.zeros_like(ds_ref)

        dg_ref[...] += jnp.sum(dy * xhat, axis=0, keepdims=True)
        db_ref[...] += jnp.sum(dy, axis=0, keepdims=True)
        ds_ref[...] += jnp.sum(dz, axis=0, keepdims=True)

    row = pl.BlockSpec((br, D), lambda i: (i, 0))
    vec = pl.BlockSpec((1, D), lambda i: (0, 0))
    vshape = jax.ShapeDtypeStruct((1, D), F32)
    return pl.pallas_call(body, name=name, grid=(T // br,), in_specs=[row, row, vec],
                          out_specs=[row, row, vec, vec, vec],
                          out_shape=[jax.ShapeDtypeStruct((T, D), F32), jax.ShapeDtypeStruct((T, D), MXU_DTYPE),
                                     vshape, vshape, vshape],
                          compiler_params=_cp(("arbitrary",)))(dy, z, g)


def loss_head(name, y, target):
    T, D = y.shape
    br = _blk(T, 512)
    n = T // br

    def body(y_ref, t_ref, loss_ref, dy_ref, acc_ref):
        i = pl.program_id(0)
        err = y_ref[...] - t_ref[...]
        dy_ref[...] = err * (1.0 / D)

        @pl.when(i == 0)
        def _():
            acc_ref[...] = jnp.zeros_like(acc_ref)

        acc_ref[...] += jnp.sum(err * err, axis=0, keepdims=True)

        @pl.when(i == n - 1)
        def _():
            loss_ref[...] = (0.5 / D) * jnp.sum(acc_ref[...], axis=1, keepdims=True)

    row = pl.BlockSpec((br, D), lambda i: (i, 0))
    return pl.pallas_call(body, name=name, grid=(n,), in_specs=[row, row],
                          out_specs=[pl.BlockSpec((1, 1), lambda i: (0, 0)), row],
                          out_shape=[jax.ShapeDtypeStruct((1, 1), F32), jax.ShapeDtypeStruct((T, D), F32)],
                          scratch_shapes=[pltpu.VMEM((1, D), F32)], compiler_params=_cp(("arbitrary",)))(y, target)


CONV_BT = 128
SUBLANES = 8


def _shifted(ref, cs, bt, offsets, stage):
    for r in range(SUBLANES):
        offs = [off for off in offsets if off % SUBLANES == r]
        if offs:
            rows = bt + max(offs) - r
            stage[r, 0:rows, :] = ref[pl.ds(r, rows), cs]
    for k, off in enumerate(offsets):
        r = off % SUBLANES
        yield k, stage[r, off - r:off - r + bt, :]


def conv_fwd(name, h1, dw_w, dw_b, ln_g, ln_b, comm=None):
    T, D2 = h1.shape
    D = D2 // 2
    W = dw_w.shape[0]
    taps = W - 1
    bt = _blk(T, CONV_BT)
    hb = bt // HALO
    u_off = [HALO - (taps - 1) + k for k in range(taps)]

    def body(h_ref, hp_ref, w_ref, b_ref, g_ref, be_ref, c_ref, s_ref, ux, stage):
        i = pl.program_id(0)
        up = hp_ref[:, :D] * _sig(hp_ref[:, D:])
        ux[0:HALO, :] = jnp.where(i == 0, 0.0, up)
        ux[HALO:HALO + bt, :] = h_ref[:, :D] * _sig(h_ref[:, D:])
        for cb in range(D // LANE):
            cs = slice(cb * LANE, (cb + 1) * LANE)
            acc = jnp.broadcast_to(b_ref[:, cs], (bt, LANE))
            for k, u_k in _shifted(ux, cs, bt, u_off, stage.at[cb % 2]):
                acc = acc + w_ref[k:k + 1, cs] * u_k
            c_ref[:, cs] = acc
        n = _ln_stats(c_ref[...]) * g_ref[...] + be_ref[...]
        s_ref[...] = (n * _sig(n)).astype(MXU_DTYPE)

    main = pl.BlockSpec((bt, D2), lambda i: (i, 0))
    prev = pl.BlockSpec((HALO, D2), lambda i: (jnp.maximum(i * hb - 1, 0), 0))
    wspec = pl.BlockSpec((W, D), lambda i: (0, 0))
    vec = pl.BlockSpec((1, D), lambda i: (0, 0))
    row = pl.BlockSpec((bt, D), lambda i: (i, 0))
    return _call(body, name=name, grid=(T // bt,), in_specs=[main, prev, wspec, vec, vec, vec], out_specs=[row, row],
                 out_shape=[jax.ShapeDtypeStruct((T, D), F32), jax.ShapeDtypeStruct((T, D), MXU_DTYPE)],
                 scratch_shapes=[pltpu.VMEM((HALO + bt, D), F32), pltpu.VMEM((2, SUBLANES, HALO + bt, LANE), F32)],
                 args=(h1, h1, dw_w, dw_b, ln_g, ln_b), sem=("parallel",), comm=comm)


def conv_bwd_ln(name, ds, c, ln_g, ln_b):
    T, D = c.shape
    br = _blk(T, 512)

    def body(ds_ref, c_ref, g_ref, be_ref, dc_ref, dg_ref, db_ref, dcs_ref):
        i = pl.program_id(0)
        c = c_ref[...]
        mu = jnp.mean(c, axis=-1, keepdims=True)
        cc = c - mu
        var = jnp.mean(cc * cc, axis=-1, keepdims=True)
        rstd = lax.rsqrt(var + LN_EPS)
        xhat = cc * rstd
        n = xhat * g_ref[...] + be_ref[...]
        sg = _sig(n)
        dn = ds_ref[...] * (sg * (1.0 + n * (1.0 - sg)))
        dxh = dn * g_ref[...]
        m1 = jnp.mean(dxh, axis=-1, keepdims=True)
        m2 = jnp.mean(dxh * xhat, axis=-1, keepdims=True)
        dc = rstd * (dxh - m1 - xhat * m2)
        dc_ref[...] = dc

        @pl.when(i == 0)
        def _():
            dg_ref[...] = jnp.zeros_like(dg_ref)
            db_ref[...] = jnp.zeros_like(db_ref)
            dcs_ref[...] = jnp.zeros_like(dcs_ref)

        dg_ref[...] += jnp.sum(dn * xhat, axis=0, keepdims=True)
        db_ref[...] += jnp.sum(dn, axis=0, keepdims=True)
        dcs_ref[...] += jnp.sum(dc, axis=0, keepdims=True)

    row = pl.BlockSpec((br, D), lambda i: (i, 0))
    vec = pl.BlockSpec((1, D), lambda i: (0, 0))
    vshape = jax.ShapeDtypeStruct((1, D), F32)
    return pl.pallas_call(body, name=name, grid=(T // br,), in_specs=[row, row, vec, vec],
                          out_specs=[row, vec, vec, vec],
                          out_shape=[jax.ShapeDtypeStruct((T, D), F32), vshape, vshape, vshape],
                          compiler_params=_cp(("arbitrary",)))(ds, c, ln_g, ln_b)


def conv_bwd_dw(name, dc, h1, dw_w, comm=None):
    T, D2 = h1.shape
    D = D2 // 2
    W = dw_w.shape[0]
    taps = W - 1
    bt = _blk(T, CONV_BT)
    hb = bt // HALO
    n = T // bt

    def body(dc_ref, dcn_ref, h_ref, hp_ref, w_ref, dh_ref, dw_ref, dhs_ref, ux, dcx, du, stage_dc, stage_u):
        i = pl.program_id(0)
        a = h_ref[:, :D]
        sg = _sig(h_ref[:, D:])
        up = hp_ref[:, :D] * _sig(hp_ref[:, D:])
        ux[0:HALO, :] = jnp.where(i == 0, 0.0, up)
        ux[HALO:HALO + bt, :] = a * sg
        dcx[0:bt, :] = dc_ref[...]
        dcx[bt:bt + HALO, :] = jnp.where(i == n - 1, 0.0, dcn_ref[...])

        @pl.when(i == 0)
        def _():
            dw_ref[...] = jnp.zeros_like(dw_ref)
            dhs_ref[...] = jnp.zeros_like(dhs_ref)

        for cb in range(D // LANE):
            cs = slice(cb * LANE, (cb + 1) * LANE)
            dcb = dcx[0:bt, cs]
            acc = jnp.zeros((bt, LANE), F32)
            for k, dc_k in _shifted(dcx, cs, bt, [taps - 1 - k for k in range(taps)], stage_dc.at[cb % 2]):
                acc = acc + w_ref[k:k + 1, cs] * dc_k
            du[:, cs] = acc
            for k, u_k in _shifted(ux, cs, bt, [HALO - (taps - 1) + k for k in range(taps)], stage_u.at[cb % 2]):
                dw_ref[k:k + 1, cs] += jnp.sum(dcb * u_k, axis=0, keepdims=True)
        d_u = du[...]
        da = d_u * sg
        dg = d_u * a * sg * (1.0 - sg)
        dh_ref[:, :D] = da.astype(MXU_DTYPE)
        dh_ref[:, D:] = dg.astype(MXU_DTYPE)
        dhs_ref[:, :D] += jnp.sum(da, axis=0, keepdims=True)
        dhs_ref[:, D:] += jnp.sum(dg, axis=0, keepdims=True)

    row = pl.BlockSpec((bt, D), lambda i: (i, 0))
    nxt = pl.BlockSpec((HALO, D), lambda i: (jnp.minimum((i + 1) * hb, T // HALO - 1), 0))
    main = pl.BlockSpec((bt, D2), lambda i: (i, 0))
    prev = pl.BlockSpec((HALO, D2), lambda i: (jnp.maximum(i * hb - 1, 0), 0))
    wspec = pl.BlockSpec((W, D), lambda i: (0, 0))
    return _call(body, name=name, grid=(n,), in_specs=[row, nxt, main, prev, wspec],
                 out_specs=[main, wspec, pl.BlockSpec((1, D2), lambda i: (0, 0))],
                 out_shape=[jax.ShapeDtypeStruct((T, D2), MXU_DTYPE), jax.ShapeDtypeStruct((W, D), F32),
                            jax.ShapeDtypeStruct((1, D2), F32)],
                 scratch_shapes=[pltpu.VMEM((HALO + bt, D), F32), pltpu.VMEM((bt + HALO, D), F32),
                                 pltpu.VMEM((bt, D), F32), pltpu.VMEM((2, SUBLANES, HALO + bt, LANE), F32),
                                 pltpu.VMEM((2, SUBLANES, HALO + bt, LANE), F32)],
                 args=(dc, dc, h1, h1, dw_w), sem=("arbitrary",), comm=comm)


def _t5_bucket(dist, n_buckets):
    max_exact = n_buckets // 2
    large = max_exact + (np.log(np.maximum(dist, 1) / max_exact) / math.log(REL_MAX_DIST / max_exact)
                         * (n_buckets - max_exact)).astype(np.int32)
    large = np.minimum(large, n_buckets - 1)
    return np.where(dist < max_exact, dist, large).astype(np.int32)


def _band_tables(n_buckets):
    i = np.arange(BAND)[:, None]
    j = np.arange(2 * BAND)[None, :]
    delta = i - j + BAND
    out = []
    for window, dil in BRANCHES:
        ok = (delta >= 0) & (delta <= window // dil)
        out.append((_t5_bucket(np.clip(delta, 0, None) * dil, n_buckets), ok))
    return out


def _units():
    for bi in (2, 1, 0):
        d = BRANCHES[bi][1]
        for r in range(d):
            for nb in range(ATT_TB // (BAND * d)):
                yield bi, d, r, nb


def _rows(start, size, d):
    return pl.ds(start, size) if d == 1 else pl.ds(start, size, stride=d)


def _bc(v):
    return jnp.broadcast_to(v, (BAND, LANE))


def attn_fwd(name, q, kv, bias, comm=None):
    T, D = q.shape
    H = D // HEAD_DIM
    TB = ATT_TB
    scale = HEAD_DIM ** -0.5

    def body(q_ref, kc_ref, kp_ref, vc_ref, vp_ref, b_ref, o_ref, l_ref, kx, vx, m_sc, s_sc, a_sc):
        first = pl.program_id(1) == 0
        kx[0:TB, :] = kp_ref[...]
        kx[TB:2 * TB, :] = kc_ref[...]
        vx[0:TB, :] = vp_ref[...]
        vx[TB:2 * TB, :] = vc_ref[...]
        col = lax.broadcasted_iota(jnp.int32, (BAND, 2 * BAND), 1)
        for bi, d, r, nb in _units():
            qs = _rows(nb * BAND * d + r, BAND, d)
            ks = _rows(TB + (nb - 1) * BAND * d + r, 2 * BAND, d)
            qb = q_ref[qs, :].astype(MXU_DTYPE)
            kb = kx[ks, :].astype(MXU_DTYPE)
            vb = vx[ks, :].astype(MXU_DTYPE)
            s = lax.dot_general(qb, kb, NT, preferred_element_type=F32) * scale + b_ref[bi]
            if nb == 0:
                s = jnp.where(jnp.logical_and(first, col < BAND), NEG, s)
            mrow = jnp.max(s, axis=1, keepdims=True)
            if bi == 2:
                p = jnp.exp(s - mrow)
                m_new = _bc(mrow)
                l_new = _bc(jnp.sum(p, axis=1, keepdims=True))
                acc = lax.dot_general(p.astype(MXU_DTYPE), vb, NN, preferred_element_type=F32)
            else:
                m_old = m_sc[qs, :]
                m_new = jnp.maximum(m_old, _bc(mrow))
                alpha = jnp.exp(m_old - m_new)
                p = jnp.exp(s - m_new[:, :1])
                l_new = alpha * s_sc[qs, :] + _bc(jnp.sum(p, axis=1, keepdims=True))
                acc = alpha * a_sc[qs, :] + lax.dot_general(p.astype(MXU_DTYPE), vb, NN, preferred_element_type=F32)
            if bi == 0:
                o_ref[qs, :] = acc / l_new
                l_ref[qs, :] = m_new + jnp.log(l_new)
            else:
                m_sc[qs, :] = m_new
                s_sc[qs, :] = l_new
                a_sc[qs, :] = acc

    cur = lambda off: pl.BlockSpec((TB, HEAD_DIM), lambda h, i: (i, off + h))
    prv = lambda off: pl.BlockSpec((TB, HEAD_DIM), lambda h, i: (jnp.maximum(i - 1, 0), off + h))
    bspec = pl.BlockSpec((3, None, BAND, 2 * BAND), lambda h, i: (0, h, 0, 0))
    sc = lambda rows: pltpu.VMEM((rows, HEAD_DIM), F32)
    return _call(body, name=name, grid=(H, T // TB), in_specs=[cur(0), cur(0), prv(0), cur(H), prv(H), bspec],
                 out_specs=[cur(0), cur(0)],
                 out_shape=[jax.ShapeDtypeStruct((T, D), F32), jax.ShapeDtypeStruct((T, D), F32)],
                 scratch_shapes=[sc(2 * TB), sc(2 * TB), sc(TB), sc(TB), sc(TB)], args=(q, kv, kv, kv, kv, bias),
                 sem=("parallel", "parallel"), comm=comm)


def attn_bwd(name, q, kv, bias, o, lse, do, comm=None):
    T, D = q.shape
    H = D // HEAD_DIM
    TB = ATT_TB
    nI = T // TB
    scale = HEAD_DIM ** -0.5

    def body(q_ref, kc_ref, kp_ref, vc_ref, vp_ref, b_ref, o_ref, l_ref, do_ref,
             dq_ref, dk_ref, dv_ref, db_ref, kx, vx, dkx, dvx, ck, cv, dl_sc, dq_sc):
        step = pl.program_id(1)
        first = step == nI - 1
        kx[0:TB, :] = kp_ref[...]
        kx[TB:2 * TB, :] = kc_ref[...]
        vx[0:TB, :] = vp_ref[...]
        vx[TB:2 * TB, :] = vc_ref[...]
        dkx[...] = jnp.zeros_like(dkx)
        dvx[...] = jnp.zeros_like(dvx)
        dq_sc[...] = jnp.zeros_like(dq_sc)
        dl_sc[...] = jnp.broadcast_to(jnp.sum(do_ref[...] * o_ref[...], axis=1, keepdims=True), (TB, LANE))

        @pl.when(step == 0)
        def _():
            db_ref[...] = jnp.zeros_like(db_ref)
            ck[...] = jnp.zeros_like(ck)
            cv[...] = jnp.zeros_like(cv)

        col = lax.broadcasted_iota(jnp.int32, (BAND, 2 * BAND), 1)
        for bi, d, r, nb in _units():
            qs = _rows(nb * BAND * d + r, BAND, d)
            ks = _rows(TB + (nb - 1) * BAND * d + r, 2 * BAND, d)
            qb = q_ref[qs, :].astype(MXU_DTYPE)
            kb = kx[ks, :].astype(MXU_DTYPE)
            vb = vx[ks, :].astype(MXU_DTYPE)
            dob = do_ref[qs, :].astype(MXU_DTYPE)
            s = lax.dot_general(qb, kb, NT, preferred_element_type=F32) * scale + b_ref[bi]
            if nb == 0:
                s = jnp.where(jnp.logical_and(first, col < BAND), NEG, s)
            p = jnp.exp(s - l_ref[qs, :][:, :1])
            dp = lax.dot_general(dob, vb, NT, preferred_element_type=F32)
            dsv = p * (dp - dl_sc[qs, :][:, :1])
            db_ref[bi] += dsv
            dsb = dsv.astype(MXU_DTYPE)
            dvx[ks, :] += lax.dot_general(p.astype(MXU_DTYPE), dob, TN, preferred_element_type=F32)
            dkx[ks, :] += lax.dot_general(dsb, qb, TN, preferred_element_type=F32) * scale
            dq_sc[qs, :] += lax.dot_general(dsb, kb, NN, preferred_element_type=F32) * scale

        dq_ref[...] = dq_sc[...].astype(dq_ref.dtype)
        dk_ref[...] = (dkx[TB:2 * TB, :] + ck[...]).astype(dk_ref.dtype)
        dv_ref[...] = (dvx[TB:2 * TB, :] + cv[...]).astype(dv_ref.dtype)
        ck[...] = dkx[0:TB, :]
        cv[...] = dvx[0:TB, :]

    blk = lambda h, i: nI - 1 - i
    cur = lambda off: pl.BlockSpec((TB, HEAD_DIM), lambda h, i: (blk(h, i), off + h))
    prv = lambda off: pl.BlockSpec((TB, HEAD_DIM), lambda h, i: (jnp.maximum(blk(h, i) - 1, 0), off + h))
    bspec = pl.BlockSpec((3, None, BAND, 2 * BAND), lambda h, i: (0, h, 0, 0))
    sc = lambda rows: pltpu.VMEM((rows, HEAD_DIM), F32)
    return _call(body, name=name, grid=(H, nI),
                 in_specs=[cur(0), cur(0), prv(0), cur(H), prv(H), bspec, cur(0), cur(0), cur(0)],
                 out_specs=[cur(0), cur(0), cur(0), bspec],
                 out_shape=[jax.ShapeDtypeStruct((T, D), MXU_DTYPE), jax.ShapeDtypeStruct((T, D), MXU_DTYPE),
                            jax.ShapeDtypeStruct((T, D), MXU_DTYPE), jax.ShapeDtypeStruct((3, H, BAND, 2 * BAND), F32)],
                 scratch_shapes=[sc(2 * TB), sc(2 * TB), sc(2 * TB), sc(2 * TB), sc(TB), sc(TB), sc(TB), sc(TB)],
                 args=(q, kv, kv, kv, kv, bias, o, lse, do), sem=("arbitrary", "arbitrary"), comm=comm)


def adamw(name, parts, w, m, v):
    L, R, C = w.shape
    P = parts[0].shape[0]
    br = R if R % 8 else _blk(R, max(8, (1 << 18) // C))
    c1 = 1.0 / (1.0 - ADAM_B1 ** ADAM_STEP)
    c2 = 1.0 / (1.0 - ADAM_B2 ** ADAM_STEP)

    def body(*refs):
        p_refs = refs[:L]
        w_ref, m_ref, v_ref, g_ref, d_ref, nm_ref, nv_ref = refs[L:]
        lay = pl.program_id(0)

        def total(p_ref):
            g = p_ref[0].astype(F32)
            for k in range(1, P):
                g = g + p_ref[k].astype(F32)
            return g

        g = total(p_refs[0])
        for j in range(1, L):
            g = jnp.where(lay == j, total(p_refs[j]), g)
        nm = ADAM_B1 * m_ref[...] + (1.0 - ADAM_B1) * g
        nv = ADAM_B2 * v_ref[...] + (1.0 - ADAM_B2) * (g * g)
        g_ref[...] = g
        nm_ref[...] = nm
        nv_ref[...] = nv
        d_ref[...] = -ADAM_LR * ((nm * c1) / (jnp.sqrt(nv * c2) + ADAM_EPS) + ADAM_WD * w_ref[...])

    row = pl.BlockSpec((None, br, C), lambda l, i: (l, i, 0))
    pspecs = [pl.BlockSpec((P, br, C), lambda l, i, j=j: (0, jnp.where(l == j, i, 0), 0)) for j in range(L)]
    shp = jax.ShapeDtypeStruct((L, R, C), F32)
    return pl.pallas_call(body, name=name, grid=(L, R // br), in_specs=[*pspecs, row, row, row],
                          out_specs=[row] * 4, out_shape=[shp] * 4,
                          compiler_params=_cp(("parallel", "parallel")))(*parts, w, m, v)


def pair_sum(name, g, s1, c_idx):
    _, _, R, C = g.shape
    br = _blk(R, max(16, (1 << 19) // C))

    def body(c_ref, g_ref, s_ref, t_ref):
        t_ref[...] = (g_ref[...].astype(F32) + s_ref[...].astype(F32)).astype(t_ref.dtype)

    return pl.pallas_call(
        body, name=name,
        grid_spec=pltpu.PrefetchScalarGridSpec(
            num_scalar_prefetch=1, grid=(4, R // br),
            in_specs=[pl.BlockSpec((None, None, br, C), lambda j, i, c: (j, c[0], i, 0)),
                      pl.BlockSpec((None, br, C), lambda j, i, c: (j, i, 0))],
            out_specs=pl.BlockSpec((None, br, C), lambda j, i, c: (j, i, 0))),
        out_shape=jax.ShapeDtypeStruct((4, R, C), g.dtype), compiler_params=_cp(("parallel", "parallel")))(c_idx, g, s1)


def _me():
    return lax.axis_index("x"), lax.axis_index("y"), lax.axis_index("c")


ANY = pl.BlockSpec(memory_space=pl.ANY)


class Gather:
    relay_at = 0.7

    def __init__(self, shards):
        n = len(shards)
        self.arrays = list(shards)
        self.out_shapes = [jax.ShapeDtypeStruct((N_DEV,) + s.shape, s.dtype) for s in shards]
        self.sems = [pltpu.SemaphoreType.DMA((n, 7)), pltpu.SemaphoreType.DMA((n, 7)), pltpu.SemaphoreType.DMA((n,))]

    def _ctx(self, ins, outs, sems):
        send_sems, recv_sems, local_sems = sems
        x, y, c = _me()
        chips = [(1 - x, y), (x, 1 - y), (1 - x, 1 - y)]

        def copy(a, k, block, to, src=None):
            px, py, pc = block
            dst = outs[a].at[4 * px + 2 * py + pc]
            return pltpu.make_async_remote_copy(src_ref=dst if src is None else src, dst_ref=dst,
                                                send_sem=send_sems.at[a, k], recv_sem=recv_sems.at[a, k],
                                                device_id=to, device_id_type=MESH)

        def own(a):
            return pltpu.make_async_copy(ins[a], outs[a].at[4 * x + 2 * y + c], local_sems.at[a])

        def first(a):
            return [copy(a, 0, (x, y, c), (x, y, 1 - c), src=ins[a])] + \
                   [copy(a, 1 + j, (x, y, c), (*chip, c), src=ins[a]) for j, chip in enumerate(chips)]

        def passed(a):
            return [copy(a, 4 + j, (*chip, c), (x, y, 1 - c)) for j, chip in enumerate(chips)]

        return (x, y, c), chips, copy, own, first, passed

    def start(self, ins, outs, sems):
        _, _, _, own, first, _ = self._ctx(ins, outs, sems)
        for a in range(len(ins)):
            own(a).start()
            for cp in first(a):
                cp.start()

    def relay(self, ins, outs, sems):
        me, chips, copy, _, _, passed = self._ctx(ins, outs, sems)
        for a in range(len(ins)):
            fwd = passed(a)
            for j, chip in enumerate(chips):
                copy(a, 1 + j, (*chip, me[2]), me).wait_recv()
                fwd[j].start()

    def wait(self, ins, outs, sems):
        (x, y, c), chips, copy, own, first, passed = self._ctx(ins, outs, sems)
        for a in range(len(ins)):
            copy(a, 0, (x, y, 1 - c), (x, y, c)).wait_recv()
            for j, chip in enumerate(chips):
                copy(a, 4 + j, (*chip, 1 - c), (x, y, c)).wait_recv()
            for cp in first(a) + passed(a):
                cp.wait_send()
            own(a).wait()


def all_gather(name, shards):
    comm = Gather(shards)
    n = len(shards)

    def body(*refs):
        comm.start(refs[:n], refs[n:2 * n], refs[2 * n:])
        comm.relay(refs[:n], refs[n:2 * n], refs[2 * n:])
        comm.wait(refs[:n], refs[n:2 * n], refs[2 * n:])

    return pl.pallas_call(body, name=name, in_specs=[ANY] * n, out_specs=[ANY] * n, out_shape=comm.out_shapes,
                          scratch_shapes=comm.sems)(*shards)


def pair_exchange(name, grads):
    n = len(grads)

    def body(*refs):
        ins, outs = refs[:n], refs[n:2 * n]
        send_sems, recv_sems = refs[2 * n:]
        x, y, c = _me()
        copies = []
        for a in range(n):
            for j in range(4):
                copies.append(pltpu.make_async_remote_copy(
                    src_ref=ins[a].at[j, 1 - c], dst_ref=outs[a].at[j],
                    send_sem=send_sems.at[a, j], recv_sem=recv_sems.at[a, j],
                    device_id=(x, y, 1 - c), device_id_type=MESH))
        for cp in copies:
            cp.start()
        for cp in copies:
            cp.wait()

    return pl.pallas_call(
        body, name=name, in_specs=[ANY] * n, out_specs=[ANY] * n,
        out_shape=[jax.ShapeDtypeStruct((4,) + g.shape[2:], g.dtype) for g in grads],
        scratch_shapes=[pltpu.SemaphoreType.DMA((n, 4)), pltpu.SemaphoreType.DMA((n, 4))],
    )(*grads)


class ChipExchange:
    relay_at = None

    def __init__(self, sums):
        n = len(sums)
        self.arrays = list(sums)
        self.out_shapes = [jax.ShapeDtypeStruct(s.shape, s.dtype) for s in sums]
        self.sems = [pltpu.SemaphoreType.DMA((n, 3)), pltpu.SemaphoreType.DMA((n, 3)), pltpu.SemaphoreType.DMA((n,))]

    def _copies(self, ins, outs, sems, arrivals):
        send_sems, recv_sems, local_sems = sems
        x, y, c = _me()
        my = 2 * x + y
        chips = [(1 - x, y), (x, 1 - y), (1 - x, 1 - y)]
        mine, sends, recvs = [], [], []
        for a in range(len(ins)):
            mine.append(pltpu.make_async_copy(ins[a].at[my], outs[a].at[my], local_sems.at[a]))
            for j, (px, py) in enumerate(chips):
                sends.append(pltpu.make_async_remote_copy(
                    src_ref=ins[a].at[2 * px + py], dst_ref=outs[a].at[my],
                    send_sem=send_sems.at[a, j], recv_sem=recv_sems.at[a, j],
                    device_id=(px, py, c), device_id_type=MESH))
                if arrivals:
                    recvs.append(pltpu.make_async_remote_copy(
                        src_ref=ins[a].at[my], dst_ref=outs[a].at[2 * px + py],
                        send_sem=send_sems.at[a, j], recv_sem=recv_sems.at[a, j],
                        device_id=(px, py, c), device_id_type=MESH))
        return mine, sends, recvs

    def start(self, ins, outs, sems):
        mine, sends, _ = self._copies(ins, outs, sems, arrivals=False)
        for cp in mine + sends:
            cp.start()

    def wait(self, ins, outs, sems):
        mine, sends, recvs = self._copies(ins, outs, sems, arrivals=True)
        for cp in recvs:
            cp.wait_recv()
        for cp in sends:
            cp.wait_send()
        for cp in mine:
            cp.wait()


def all_reduce_small(name, pack):
    R, C = pack.shape

    def body(p_ref, o_ref, buf, send_sems, recv_sems):
        x, y, c = _me()
        me = 4 * x + 2 * y + c
        buf[me] = p_ref[...]
        copies = []
        for k in range(1, N_DEV):
            px, py, pc = x ^ (k >> 2), y ^ ((k >> 1) & 1), c ^ (k & 1)
            copies.append(pltpu.make_async_remote_copy(
                src_ref=p_ref, dst_ref=buf.at[me], send_sem=send_sems.at[k - 1], recv_sem=recv_sems.at[k - 1],
                device_id=(px, py, pc), device_id_type=MESH))
        for cp in copies:
            cp.start()
        for cp in copies:
            cp.wait()
        acc = buf[0]
        for d in range(1, N_DEV):
            acc = acc + buf[d]
        o_ref[...] = acc

    vm = pl.BlockSpec(memory_space=pltpu.VMEM)
    return pl.pallas_call(
        body, name=name, in_specs=[vm], out_specs=vm, out_shape=jax.ShapeDtypeStruct((R, C), F32),
        scratch_shapes=[pltpu.VMEM((N_DEV, R, C), F32), pltpu.SemaphoreType.DMA((N_DEV - 1,)),
                        pltpu.SemaphoreType.DMA((N_DEV - 1,))],
    )(pack)


def _ep_bias(acc, ex, outs):
    outs[0][...] = acc + ex[0][...]


def _ep_store(acc, ex, outs):
    outs[0][...] = acc.astype(outs[0].dtype)


def _ep_resid(alpha, bias):
    def ep(acc, ex, outs):
        if bias:
            outs[0][...] = alpha * ex[1][...] + (acc + ex[0][...])
        else:
            outs[0][...] = alpha * ex[0][...] + acc
    return ep


def _ep_relu2(acc, ex, outs):
    r = jnp.maximum(acc, 0.0)
    outs[0][...] = (r * r).astype(outs[0].dtype)
    outs[1][...] = r


def _ep_relu2_bwd(acc, ex, outs):
    outs[0][...] = (acc * (2.0 * ex[0][...])).astype(outs[0].dtype)


def _ep_add(acc, ex, outs):
    outs[0][...] = ex[0][...] + acc


def kernel(x, conv_pw1_w, conv_pw1_b, conv_dw_w, conv_dw_b, conv_ln_g, conv_ln_b, conv_pw2_w, conv_pw2_b, w_kv, attn_wq, attn_wo, rel_bias, mlp_w1, mlp_w2, ln_mix_g, ln_mix_b, ln_mlp_g, ln_mlp_b, loss_target, m_conv_pw1_w, m_conv_pw1_b, m_conv_dw_w, m_conv_dw_b, m_conv_ln_g, m_conv_ln_b, m_conv_pw2_w, m_conv_pw2_b, m_w_kv, m_attn_wq, m_attn_wo, m_rel_bias, m_mlp_w1, m_mlp_w2, m_ln_mix_g, m_ln_mix_b, m_ln_mlp_g, m_ln_mlp_b, v_conv_pw1_w, v_conv_pw1_b, v_conv_dw_w, v_conv_dw_b, v_conv_ln_g, v_conv_ln_b, v_conv_pw2_w, v_conv_pw2_b, v_w_kv, v_attn_wq, v_attn_wo, v_rel_bias, v_mlp_w1, v_mlp_w2, v_ln_mix_g, v_ln_mix_b, v_ln_mlp_g, v_ln_mlp_b):
    T, D = x.shape[1], x.shape[2]
    H = D // HEAD_DIM
    depth = mlp_w1.shape[0]
    assert depth == 2 and T % ATT_TB == 0
    alpha = (2 * depth) ** 0.25
    ds_ = D // N_DEV
    xi, yi, ci = _me()
    me = 4 * xi + 2 * yi + ci
    c_idx = ci.astype(jnp.int32).reshape(1)
    x2 = x.reshape(T, D)
    target = loss_target.reshape(T, D)
    n_buckets = rel_bias.shape[0]
    taps = conv_dw_w.shape[1]

    local = {
        "pw1": (conv_pw1_w, 0), "pw2": (conv_pw2_w, 0), "wkv": (w_kv, None), "wq": (attn_wq, 0), "wo": (attn_wo, 0),
        "w1_0": (mlp_w1, 0), "w1_1": (mlp_w1, 1), "w2_0": (mlp_w2, 0), "w2_1": (mlp_w2, 1),
    }
    names = list(local)
    small = jnp.concatenate([conv_dw_w[0], conv_dw_b, conv_ln_g, conv_ln_b, conv_pw2_b,
                             conv_pw1_b.reshape(2, ds_)], axis=0)
    small = jnp.pad(small, ((0, (-small.shape[0]) % 8), (0, 0)))
    shard = {k: cast_bf16("cast_" + k, *local[k]) for k in names}
    G = {}

    def gather_start(*keys):
        return Gather([shard[k] for k in keys])

    def gather_done(keys, delivered):
        G.update(zip(keys, delivered))

    g_pw1, g_small = all_gather("gather_pw1", [shard["pw1"], small])
    G["pw1"] = g_pw1
    sm = jnp.transpose(g_small, (1, 0, 2)).reshape(small.shape[0], D)
    dw_w = jnp.pad(sm[:taps], ((0, HALO - taps), (0, 0)))
    dw_b, cln_g, cln_b, pw2_b = (sm[taps + k:taps + k + 1] for k in range(4))
    pw1_b = g_small[:, taps + 4:taps + 6, :].reshape(1, 2 * D)

    row = lambda a, l: a[l:l + 1]

    x_b = cast_bf16("cast_x", x2)
    (h1,) = mm_fwd("pw1", x_b, G["pw1"], colsharded=True, epilogue=_ep_bias, outs=[F32], rowvec=[pw1_b])
    (c_pre, s_b), delivered = conv_fwd("conv_fwd", h1, dw_w, dw_b, cln_g, cln_b, comm=gather_start("pw2", "w1_0"))
    gather_done(("pw2", "w1_0"), delivered)
    (z1,) = mm_fwd("pw2", s_b, G["pw2"], colsharded=False, epilogue=_ep_resid(alpha, True), outs=[F32],
                   rowvec=[pw2_b], tiles=[x2])
    x1, x1_b = ln_fwd("ln_mix0", z1, row(ln_mix_g, 0), row(ln_mix_b, 0))

    def mlp_fwd(l, xin, xin_b, up_keys=(), down_keys=()):
        res_ = mm_fwd(f"mlp_up{l}", xin_b, G[f"w1_{l}"], colsharded=True, epilogue=_ep_relu2, outs=[MXU_DTYPE, F32],
                      comm=gather_start(*up_keys) if up_keys else None)
        (act_b, r), delivered = res_ if up_keys else (res_, [])
        gather_done(up_keys, delivered)
        res_ = mm_fwd(f"mlp_down{l}", act_b, G[f"w2_{l}"], colsharded=False, epilogue=_ep_resid(alpha, False),
                      outs=[F32], tiles=[xin], comm=gather_start(*down_keys) if down_keys else None)
        ((z,), delivered) = res_ if down_keys else (res_, [])
        gather_done(down_keys, delivered)
        y, y_b = ln_fwd(f"ln_mlp{l}", z, row(ln_mlp_g, l), row(ln_mlp_b, l))
        return act_b, r, z, y, y_b

    act0_b, r0, z2, x2_, x2_b = mlp_fwd(0, x1, x1_b, up_keys=("w2_0",), down_keys=("wkv", "wq", "wo"))

    (kv,) = mm_fwd("kv_proj", x2_b, G["wkv"], colsharded=True, epilogue=_ep_store, outs=[F32])
    (q,) = mm_fwd("q_proj", x2_b, G["wq"], colsharded=False, epilogue=_ep_store, outs=[F32])
    tables = _band_tables(n_buckets)
    onehot = jnp.concatenate([(jnp.arange(n_buckets)[:, None] == jnp.asarray(bucket).reshape(1, -1)).astype(F32)
                              for bucket, _ in tables], axis=1)
    KB = onehot.shape[1] // 8
    (btab,) = _mm("rel_bias_table", rel_bias, onehot, grid=(1, 8, 1),
                  a_spec=pl.BlockSpec((n_buckets, H), lambda i, j, k: (0, 0)),
                  b_spec=pl.BlockSpec((n_buckets, KB), lambda i, j, k: (0, j)), dims=TN, k_axis=2,
                  epilogue=_ep_store, out_shapes=[jax.ShapeDtypeStruct((H, 8 * KB), F32)],
                  out_specs=[pl.BlockSpec((H, KB), lambda i, j, k: (0, j))], acc_shape=(H, KB), exact=True)
    band_ok = jnp.asarray(np.stack([ok for _, ok in tables]))[:, None]
    bias = jnp.where(band_ok, jnp.transpose(btab.reshape(H, 3, BAND, 2 * BAND), (1, 0, 2, 3)), NEG)
    (o, lse), delivered = attn_fwd("attn_fwd", q, kv, bias, comm=gather_start("w1_1", "w2_1"))
    gather_done(("w1_1", "w2_1"), delivered)
    (z3,) = mm_fwd("o_proj", o, G["wo"], colsharded=False, epilogue=_ep_resid(alpha, False), outs=[F32], tiles=[x2_])
    x3, x3_b = ln_fwd("ln_mix1", z3, row(ln_mix_g, 1), row(ln_mix_b, 1))
    act1_b, r1, z4, x4, _ = mlp_fwd(1, x3, x3_b)

    loss_local, dy = loss_head("loss", x4, target)
    loss = lax.psum(loss_local[0, 0], MESH_AXES)

    parts = {}

    def reduce_start(tag, grads):
        g4 = [g.reshape((4, 2) + g.shape[1:]) for g in grads.values()]
        from_sibling = pair_exchange("grad_pair_exchange_" + tag, g4)
        return ChipExchange([pair_sum("grad_pair_sum_" + k, g, s, c_idx) for k, g, s in zip(grads, g4, from_sibling)])

    def reduce_done(grads, delivered):
        parts.update(zip(grads, delivered))

    def mlp_bwd(l, dy_out, z, xin_b, act_b, r, comm=None):
        dz, dz_b, dg, db, _ = ln_bwd(f"ln_mlp{l}_bwd", dy_out, z, row(ln_mlp_g, l))
        res_ = mm_dx(f"mlp_down{l}_dx", dz_b, G[f"w2_{l}"], colsharded=False, epilogue=_ep_relu2_bwd,
                     outs=[MXU_DTYPE], tiles=[r], comm=comm)
        ((dhm_b,), delivered) = res_ if comm is not None else (res_, [])
        dw2 = mm_dw(f"mlp_down{l}_dw", act_b, dz_b, colsharded=False)
        (dxin,) = mm_dx(f"mlp_up{l}_dx", dhm_b, G[f"w1_{l}"], colsharded=True, epilogue=_ep_resid(alpha, False),
                        outs=[F32], tiles=[dz])
        dw1 = mm_dw(f"mlp_up{l}_dw", xin_b, dhm_b, colsharded=True)
        return dxin, dw1, dw2, dg, db, delivered

    dx3, dw1_1, dw2_1, dg_mlp1, db_mlp1, _ = mlp_bwd(1, dy, z4, x3_b, act1_b, r1)
    grads1 = {"w1_1": dw1_1, "w2_1": dw2_1}
    comm1 = reduce_start("mlp1", grads1)

    dz3, dz3_b, dg_mix1, db_mix1, _ = ln_bwd("ln_mix1_bwd", dx3, z3, row(ln_mix_g, 1))
    (do,) = mm_dx("o_proj_dx", dz3_b, G["wo"], colsharded=False, epilogue=_ep_store, outs=[F32])
    dwo = mm_dw("o_proj_dw", o, dz3_b, colsharded=False)
    (dq, dk, dv, dbias), delivered = attn_bwd("attn_bwd", q, kv, bias, o, lse, do, comm=comm1)
    reduce_done(grads1, delivered)
    dkv = jnp.concatenate([dk, dv], axis=1)
    (dx2a,) = mm_dx("q_proj_dx", dq, G["wq"], colsharded=False, epilogue=_ep_resid(alpha, False), outs=[F32], tiles=[dz3])
    (dx2,) = mm_dx("kv_proj_dx", dkv, G["wkv"], colsharded=True, epilogue=_ep_add, outs=[F32], tiles=[dx2a])
    dwq = mm_dw("q_proj_dw", x2_b, dq, colsharded=False)
    dwkv = mm_dw("kv_proj_dw", x2_b, dkv, colsharded=True)
    grads2 = {"wo": dwo, "wq": dwq, "wkv": dwkv}
    comm2 = reduce_start("attn", grads2)

    dbias2 = jnp.transpose(dbias, (1, 0, 2, 3)).reshape(H, -1)
    (drel_t,) = _mm("rel_bias_grad", dbias2, onehot, grid=(1, 1, 8),
                    a_spec=pl.BlockSpec((H, KB), lambda i, j, k: (0, k)),
                    b_spec=pl.BlockSpec((n_buckets, KB), lambda i, j, k: (0, k)), dims=NT, k_axis=2,
                    epilogue=_ep_store, out_shapes=[jax.ShapeDtypeStruct((H, n_buckets), F32)],
                    out_specs=[pl.BlockSpec((H, n_buckets), lambda i, j, k: (0, 0))], acc_shape=(H, n_buckets), exact=True)

    dx1, dw1_0, dw2_0, dg_mlp0, db_mlp0, delivered = mlp_bwd(0, dx2, z2, x1_b, act0_b, r0, comm=comm2)
    reduce_done(grads2, delivered)

    dz1, dz1_b, dg_mix0, db_mix0, dpw2_b = ln_bwd("ln_mix0_bwd", dx1, z1, row(ln_mix_g, 0))
    (ds,) = mm_dx("pw2_dx", dz1_b, G["pw2"], colsharded=False, epilogue=_ep_store, outs=[F32])
    dwpw2 = mm_dw("pw2_dw", s_b, dz1_b, colsharded=False)
    grads3 = {"w1_0": dw1_0, "w2_0": dw2_0, "pw2": dwpw2}
    comm3 = reduce_start("mlp0", grads3)
    dc, dcln_g, dcln_b, ddw_b = conv_bwd_ln("conv_bwd_ln", ds, c_pre, cln_g, cln_b)
    (dh1_b, ddw_w, dpw1_b), delivered = conv_bwd_dw("conv_bwd_dw", dc, h1, dw_w, comm=comm3)
    reduce_done(grads3, delivered)
    grads4 = {"pw1": mm_dw("pw1_dw", x_b, dh1_b, colsharded=True)}
    (dx,), delivered = mm_dx("pw1_dx", dh1_b, G["pw1"], colsharded=True, epilogue=_ep_resid(alpha, False), outs=[F32],
                             tiles=[dz1], comm=reduce_start("pw1", grads4))
    reduce_done(grads4, delivered)
    grad_x = dx.reshape(1, T, D)

    vec_rows = [dg_mix0, dg_mix1, db_mix0, db_mix1, dg_mlp0, dg_mlp1, db_mlp0, db_mlp1,
                ddw_b, dcln_g, dcln_b, dpw2_b, dpw1_b.reshape(2, D), ddw_w[:taps],
                jnp.pad(jnp.transpose(drel_t).reshape(1, -1), ((0, 0), (0, D - H * n_buckets)))]
    pack = jnp.concatenate(vec_rows, axis=0)
    pack = jnp.pad(pack, ((0, (-pack.shape[0]) % 8), (0, 0)))
    tot = all_reduce_small("grad_small_all_reduce", pack)

    def mine(rows):
        return lax.dynamic_slice_in_dim(rows, me * ds_, ds_, axis=1)

    g_ln_mix_g, g_ln_mix_b, g_ln_mlp_g, g_ln_mlp_b = tot[0:2], tot[2:4], tot[4:6], tot[6:8]
    g_dw_b, g_cln_g, g_cln_b, g_pw2_b = (mine(tot[8 + k:9 + k]) for k in range(4))
    g_pw1_b = lax.dynamic_slice_in_dim(tot[12:14].reshape(1, 2 * D), me * 2 * ds_, 2 * ds_, axis=1)
    g_dw_w = mine(tot[14:14 + taps])
    g_rel = tot[14 + taps, :H * n_buckets].reshape(n_buckets, H)

    res = {}

    def upd(nm, parts_, w, m, v):
        shp = w.shape
        if not isinstance(parts_, list):
            parts_ = [parts_]
        parts_ = [p[None] if p.ndim == 2 else p for p in parts_]
        w3, m3, v3 = (a.reshape((len(parts_),) + parts_[0].shape[1:]) for a in (w, m, v))
        outs = adamw("adamw_" + nm, parts_, w3, m3, v3)
        res[nm] = tuple(o_.reshape(shp) for o_ in outs)

    upd("conv_pw1_w", parts["pw1"], conv_pw1_w, m_conv_pw1_w, v_conv_pw1_w)
    upd("conv_pw1_b", g_pw1_b, conv_pw1_b, m_conv_pw1_b, v_conv_pw1_b)
    upd("conv_dw_w", g_dw_w, conv_dw_w, m_conv_dw_w, v_conv_dw_w)
    upd("conv_dw_b", g_dw_b, conv_dw_b, m_conv_dw_b, v_conv_dw_b)
    upd("conv_ln_g", g_cln_g, conv_ln_g, m_conv_ln_g, v_conv_ln_g)
    upd("conv_ln_b", g_cln_b, conv_ln_b, m_conv_ln_b, v_conv_ln_b)
    upd("conv_pw2_w", parts["pw2"], conv_pw2_w, m_conv_pw2_w, v_conv_pw2_w)
    upd("conv_pw2_b", g_pw2_b, conv_pw2_b, m_conv_pw2_b, v_conv_pw2_b)
    upd("w_kv", parts["wkv"], w_kv, m_w_kv, v_w_kv)
    upd("attn_wq", parts["wq"], attn_wq, m_attn_wq, v_attn_wq)
    upd("attn_wo", parts["wo"], attn_wo, m_attn_wo, v_attn_wo)
    upd("rel_bias", g_rel, rel_bias, m_rel_bias, v_rel_bias)
    upd("mlp_w1", [parts["w1_0"], parts["w1_1"]], mlp_w1, m_mlp_w1, v_mlp_w1)
    upd("mlp_w2", [parts["w2_0"], parts["w2_1"]], mlp_w2, m_mlp_w2, v_mlp_w2)
    upd("ln_mix_g", g_ln_mix_g, ln_mix_g, m_ln_mix_g, v_ln_mix_g)
    upd("ln_mix_b", g_ln_mix_b, ln_mix_b, m_ln_mix_b, v_ln_mix_b)
    upd("ln_mlp_g", g_ln_mlp_g, ln_mlp_g, m_ln_mlp_g, v_ln_mlp_g)
    upd("ln_mlp_b", g_ln_mlp_b, ln_mlp_b, m_ln_mlp_b, v_ln_mlp_b)

    order = ["conv_pw1_w", "conv_pw1_b", "conv_dw_w", "conv_dw_b", "conv_ln_g", "conv_ln_b", "conv_pw2_w",
             "conv_pw2_b", "w_kv", "attn_wq", "attn_wo", "rel_bias", "mlp_w1", "mlp_w2", "ln_mix_g", "ln_mix_b",
             "ln_mlp_g", "ln_mlp_b"]
    return (loss, grad_x, *[res[n_][0] for n_ in order], *[res[n_][1] for n_ in order],
            *[res[n_][2] for n_ in order], *[res[n_][3] for n_ in order])
```

```python
import math

import numpy as np
import jax
import jax.numpy as jnp
from jax import lax
from jax.experimental import pallas as pl
from jax.experimental.pallas import tpu as pltpu

F32 = jnp.float32
MXU_DTYPE = jnp.bfloat16
GRAD_DTYPE = jnp.bfloat16
VMEM_LIMIT_BYTES = 56 * 2**20
LANE = 128
N_DEV = 8
MESH_AXES = ("x", "y", "c")
MESH = pl.DeviceIdType.MESH

HEAD_DIM = 128
BAND = 128
BRANCHES = ((128, 1), (512, 4), (2048, 16))
ATT_TB = BAND * 16
ATT_MERGE_ROWS = 256
REL_MAX_DIST = 2048
LN_EPS = 1e-5
NEG = -1e30
HALO = 32

ADAM_LR, ADAM_B1, ADAM_B2, ADAM_EPS, ADAM_WD, ADAM_STEP = 0.001, 0.9, 0.999, 1e-08, 0.01, 10

NN = (((1,), (0,)), ((), ()))
NT = (((1,), (1,)), ((), ()))
TN = (((0,), (0,)), ((), ()))


def _cp(sem=None):
    return pltpu.CompilerParams(dimension_semantics=sem, vmem_limit_bytes=VMEM_LIMIT_BYTES)


def _sig(v):
    return 1.0 / (1.0 + jnp.exp(-v))


def _call(body, *, name, grid, in_specs, out_specs, out_shape, args, sem, scratch_shapes=(), comm=None):
    if comm is None:
        res = pl.pallas_call(body, name=name, grid=grid, in_specs=list(in_specs), out_specs=list(out_specs),
                             out_shape=list(out_shape), scratch_shapes=list(scratch_shapes),
                             compiler_params=_cp(sem))(*args)
        return list(res), []
    n_in, n_out, n_sc, nc_in, nc_out = len(in_specs), len(out_specs), len(scratch_shapes), len(comm.arrays), len(comm.out_shapes)

    def wrapped(*refs):
        pos = 0
        parts = []
        for cnt in (n_in, nc_in, n_out, nc_out, n_sc):
            parts.append(refs[pos:pos + cnt])
            pos += cnt
        ins, cin, outs, cout, sc = parts
        csem = refs[pos:]
        step = pl.program_id(0)
        for ax in range(1, len(grid)):
            step = step * grid[ax] + pl.program_id(ax)
        n_steps = math.prod(grid)

        @pl.when(step == 0)
        def _():
            comm.start(cin, cout, csem)

        if comm.relay_at is not None:
            @pl.when(step == min(n_steps - 1, int(n_steps * comm.relay_at)))
            def _():
                comm.relay(cin, cout, csem)

        body(*ins, *outs, *sc)

        @pl.when(step == n_steps - 1)
        def _():
            comm.wait(cin, cout, csem)

    res = pl.pallas_call(wrapped, name=name, grid=grid, in_specs=[*in_specs, *[ANY] * nc_in],
                         out_specs=[*out_specs, *[ANY] * nc_out], out_shape=[*out_shape, *comm.out_shapes],
                         scratch_shapes=[*scratch_shapes, *comm.sems],
                         compiler_params=_cp(("arbitrary",) * len(grid)))(*args, *comm.arrays)
    return list(res[:n_out]), list(res[n_out:])


def _mm(name, a, b, *, grid, a_spec, b_spec, dims, k_axis, epilogue, out_shapes, out_specs,
        acc_shape, extra=(), extra_specs=(), exact=False, nsplit=1, ncat=1, comm=None):
    nk = grid[k_axis]
    n_extra, n_out = len(extra), len(out_shapes)

    def dot(av, bv):
        if exact:
            return lax.dot_general(av, bv, dims, precision=lax.Precision.HIGHEST, preferred_element_type=F32)
        return lax.dot_general(av.astype(MXU_DTYPE), bv.astype(MXU_DTYPE), dims, preferred_element_type=F32)

    def body(a_ref, b_ref, *rest):
        ex, outs = rest[:n_extra], rest[n_extra:n_extra + n_out]
        if ncat > 1:
            part = jnp.concatenate([dot(a_ref[...], b_ref[g]) for g in range(ncat)], axis=1)
        elif nsplit == 1:
            part = dot(a_ref[...], b_ref[...])
        else:
            w = a_ref.shape[1] // nsplit
            part = dot(a_ref[:, 0:w], b_ref[0])
            for s in range(1, nsplit):
                part = part + dot(a_ref[:, s * w:(s + 1) * w], b_ref[s])
        if nk == 1:
            epilogue(part, ex, outs)
        else:
            acc = rest[n_extra + n_out]
            k = pl.program_id(k_axis)

            @pl.when(k == 0)
            def _():
                acc[...] = jnp.zeros_like(acc)

            acc[...] += part

            @pl.when(k == nk - 1)
            def _():
                epilogue(acc[...], ex, outs)

    sem = tuple("arbitrary" if ax == k_axis else "parallel" for ax in range(len(grid)))
    outs, couts = _call(body, name=name, grid=grid, in_specs=[a_spec, b_spec, *extra_specs], out_specs=out_specs,
                        out_shape=out_shapes, args=(a, b, *extra), sem=sem,
                        scratch_shapes=[pltpu.VMEM(acc_shape, F32)] if nk > 1 else [], comm=comm)
    return outs if comm is None else (outs, couts)


def _blk(n, want):
    return min(n, want)


MM_BLOCK = 1024
MM_K = 2048


def mm_fwd(name, x, w, *, colsharded, epilogue, outs, rowvec=(), tiles=(), comm=None):
    T, K = x.shape
    bm = _blk(T, MM_BLOCK)
    ncat = 1
    if colsharded:
        n_s = w.shape[2]
        N = N_DEV * n_s
        bk = K
        if n_s < MM_BLOCK and (N_DEV * n_s) % MM_BLOCK == 0:
            ncat = MM_BLOCK // n_s
            bn = MM_BLOCK
            b_spec = pl.BlockSpec((ncat, K, n_s), lambda i, j, k: (j, 0, 0))
        else:
            bn = _blk(n_s, MM_BLOCK)
            per = n_s // bn
            b_spec = pl.BlockSpec((None, K, bn), lambda i, j, k: (j // per, 0, j % per))
    else:
        N = w.shape[2]
        w = w.reshape(K, N)
        bn = _blk(N, MM_BLOCK)
        bk = _blk(K, MM_K)
        b_spec = pl.BlockSpec((bk, bn), lambda i, j, k: (k, j))
    grid = (T // bm, N // bn, K // bk)
    a_spec = pl.BlockSpec((bm, bk), lambda i, j, k: (i, k))
    tile_spec = pl.BlockSpec((bm, bn), lambda i, j, k: (i, j))
    vec_spec = pl.BlockSpec((1, bn), lambda i, j, k: (0, j))
    return _mm(name, x, w, grid=grid, a_spec=a_spec, b_spec=b_spec, dims=NN, k_axis=2, epilogue=epilogue,
               out_shapes=[jax.ShapeDtypeStruct((T, N), dt) for dt in outs], out_specs=[tile_spec] * len(outs),
               acc_shape=(bm, bn), extra=(*rowvec, *tiles),
               extra_specs=[vec_spec] * len(rowvec) + [tile_spec] * len(tiles), ncat=ncat, comm=comm)


def mm_dx(name, dy, w, *, colsharded, epilogue, outs, tiles=(), comm=None):
    T, N = dy.shape
    bm = _blk(T, MM_BLOCK)
    if colsharded:
        K, n_s = w.shape[1], w.shape[2]
        bko = _blk(K, MM_BLOCK)
        spk = max(1, min(N_DEV, MM_K // n_s))
        grid = (T // bm, K // bko, N_DEV // spk)
        a_spec = pl.BlockSpec((bm, spk * n_s), lambda i, j, s: (i, s))
        b_spec = pl.BlockSpec((spk, bko, n_s), lambda i, j, s: (s, j, 0))
    else:
        K = w.shape[1] * N_DEV
        w = w.reshape(K, N)
        bko = _blk(K, MM_BLOCK)
        spk = 1
        grid = (T // bm, K // bko, 1)
        a_spec = pl.BlockSpec((bm, N), lambda i, j, s: (i, 0))
        b_spec = pl.BlockSpec((bko, N), lambda i, j, s: (j, 0))
    tile_spec = pl.BlockSpec((bm, bko), lambda i, j, s: (i, j))
    if spk == 1 and colsharded:
        b_spec = pl.BlockSpec((None, bko, n_s), lambda i, j, s: (s, j, 0))
    return _mm(name, dy, w, grid=grid, a_spec=a_spec, b_spec=b_spec, dims=NT, k_axis=2, epilogue=epilogue,
               out_shapes=[jax.ShapeDtypeStruct((T, K), dt) for dt in outs], out_specs=[tile_spec] * len(outs),
               acc_shape=(bm, bko), extra=tuple(tiles), extra_specs=[tile_spec] * len(tiles), nsplit=spk, comm=comm)


def mm_dw(name, x, dy, *, colsharded, comm=None):
    T, K = x.shape
    N = dy.shape[1]
    narrow = x.dtype.itemsize == 2 and dy.dtype.itemsize == 2
    bt = _blk(T, 2 * MM_K if narrow else MM_K)
    bmo = _blk(K, MM_BLOCK)
    epilogue = _ep_store
    if colsharded:
        n_s = N // N_DEV
        out_shape = jax.ShapeDtypeStruct((N_DEV, K, n_s), GRAD_DTYPE)
        if n_s < MM_BLOCK and N % MM_BLOCK == 0:
            group = MM_BLOCK // n_s
            bno = MM_BLOCK
            out_spec = pl.BlockSpec((group, bmo, n_s), lambda i, j, t: (j, i, 0))

            def epilogue(acc, ex, outs):
                for g in range(group):
                    outs[0][g] = acc[:, g * n_s:(g + 1) * n_s].astype(outs[0].dtype)
        else:
            bno = _blk(n_s, MM_BLOCK)
            per = n_s // bno
            out_spec = pl.BlockSpec((None, bmo, bno), lambda i, j, t: (j // per, i, j % per))
    else:
        bno = _blk(N, MM_BLOCK)
        out_shape = jax.ShapeDtypeStruct((K, N), GRAD_DTYPE)
        out_spec = pl.BlockSpec((bmo, bno), lambda i, j, t: (i, j))
    grid = (K // bmo, N // bno, T // bt)
    a_spec = pl.BlockSpec((bt, bmo), lambda i, j, t: (t, i))
    b_spec = pl.BlockSpec((bt, bno), lambda i, j, t: (t, j))
    res = _mm(name, x, dy, grid=grid, a_spec=a_spec, b_spec=b_spec, dims=TN, k_axis=2, epilogue=epilogue,
              out_shapes=[out_shape], out_specs=[out_spec], acc_shape=(bmo, bno), comm=comm)
    (out,), couts = res if comm is not None else (res, [])
    out = out if colsharded else out.reshape(N_DEV, K // N_DEV, N)
    return out if comm is None else (out, couts)


def cast_bf16(name, a, layer=None):
    R, C = a.shape[-2:]
    br = _blk(R, 512)

    def body(a_ref, o_ref):
        o_ref[...] = a_ref[...].astype(MXU_DTYPE)

    spec = pl.BlockSpec((br, C), lambda i: (i, 0))
    in_spec = spec if layer is None else pl.BlockSpec((None, br, C), lambda i: (layer, i, 0))
    return pl.pallas_call(body, name=name, grid=(R // br,), in_specs=[in_spec], out_specs=spec,
                          out_shape=jax.ShapeDtypeStruct((R, C), MXU_DTYPE), compiler_params=_cp(("parallel",)))(a)


def _ln_stats(z):
    mu = jnp.mean(z, axis=-1, keepdims=True)
    zc = z - mu
    var = jnp.mean(zc * zc, axis=-1, keepdims=True)
    return zc * lax.rsqrt(var + LN_EPS)


def ln_fwd(name, z, g, b):
    T, D = z.shape
    br = _blk(T, 512)

    def body(z_ref, g_ref, b_ref, y_ref, yb_ref):
        y = _ln_stats(z_ref[...]) * g_ref[...] + b_ref[...]
        y_ref[...] = y
        yb_ref[...] = y.astype(MXU_DTYPE)

    row = pl.BlockSpec((br, D), lambda i: (i, 0))
    vec = pl.BlockSpec((1, D), lambda i: (0, 0))
    return pl.pallas_call(body, name=name, grid=(T // br,), in_specs=[row, vec, vec], out_specs=[row, row],
                          out_shape=[jax.ShapeDtypeStruct((T, D), F32), jax.ShapeDtypeStruct((T, D), MXU_DTYPE)],
                          compiler_params=_cp(("parallel",)))(z, g, b)


def ln_bwd(name, dy, z, g, comm=None):
    T, D = z.shape
    br = _blk(T, 512)

    def body(dy_ref, z_ref, g_ref, dz_ref, dzb_ref, dg_ref, db_ref, ds_ref):
        i = pl.program_id(0)
        z = z_ref[...]
        dy = dy_ref[...]
        mu = jnp.mean(z, axis=-1, keepdims=True)
        zc = z - mu
        var = jnp.mean(zc * zc, axis=-1, keepdims=True)
        rstd = lax.rsqrt(var + LN_EPS)
        xhat = zc * rstd
        dxh = dy * g_ref[...]
        m1 = jnp.mean(dxh, axis=-1, keepdims=True)
        m2 = jnp.mean(dxh * xhat, axis=-1, keepdims=True)
        dz = rstd * (dxh - m1 - xhat * m2)
        dz_ref[...] = dz
        dzb_ref[...] = dz.astype(MXU_DTYPE)

        @pl.when(i == 0)
        def _():
            dg_ref[...] = jnp.zeros_like(dg_ref)
            db_ref[...] = jnp.zeros_like(db_ref)
            ds_ref[...] = jnp.zeros_like(ds_ref)

        dg_ref[...] += jnp.sum(dy * xhat, axis=0, keepdims=True)
        db_ref[...] += jnp.sum(dy, axis=0, keepdims=True)
        ds_ref[...] += jnp.sum(dz, axis=0, keepdims=True)

    row = pl.BlockSpec((br, D), lambda i: (i, 0))
    vec = pl.BlockSpec((1, D), lambda i: (0, 0))
    vshape = jax.ShapeDtypeStruct((1, D), F32)
    outs, delivered = _call(body, name=name, grid=(T // br,), in_specs=[row, row, vec],
                            out_specs=[row, row, vec, vec, vec],
                            out_shape=[jax.ShapeDtypeStruct((T, D), F32), jax.ShapeDtypeStruct((T, D), MXU_DTYPE),
                                       vshape, vshape, vshape],
                            args=(dy, z, g), sem=("arbitrary",), comm=comm)
    return outs if comm is None else (outs, delivered)


def loss_head(name, y, target):
    T, D = y.shape
    br = _blk(T, 512)
    n = T // br

    def body(y_ref, t_ref, loss_ref, dy_ref, acc_ref):
        i = pl.program_id(0)
        err = y_ref[...] - t_ref[...]
        dy_ref[...] = err * (1.0 / D)

        @pl.when(i == 0)
        def _():
            acc_ref[...] = jnp.zeros_like(acc_ref)

        acc_ref[...] += jnp.sum(err * err, axis=0, keepdims=True)

        @pl.when(i == n - 1)
        def _():
            loss_ref[...] = (0.5 / D) * jnp.sum(acc_ref[...], axis=1, keepdims=True)

    row = pl.BlockSpec((br, D), lambda i: (i, 0))
    return pl.pallas_call(body, name=name, grid=(n,), in_specs=[row, row],
                          out_specs=[pl.BlockSpec((1, 1), lambda i: (0, 0)), row],
                          out_shape=[jax.ShapeDtypeStruct((1, 1), F32), jax.ShapeDtypeStruct((T, D), F32)],
                          scratch_shapes=[pltpu.VMEM((1, D), F32)], compiler_params=_cp(("arbitrary",)))(y, target)


CONV_BT = 128
SUBLANES = 8


def _shifted(ref, cs, bt, offsets, stage):
    for r in range(SUBLANES):
        offs = [off for off in offsets if off % SUBLANES == r]
        if offs:
            rows = bt + max(offs) - r
            stage[r, 0:rows, :] = ref[pl.ds(r, rows), cs]
    for k, off in enumerate(offsets):
        r = off % SUBLANES
        yield k, stage[r, off - r:off - r + bt, :]


def conv_fwd(name, h1, dw_w, dw_b, ln_g, ln_b, comm=None):
    T, D2 = h1.shape
    D = D2 // 2
    W = dw_w.shape[0]
    taps = W - 1
    bt = _blk(T, CONV_BT)
    hb = bt // HALO
    u_off = [HALO - (taps - 1) + k for k in range(taps)]

    def body(h_ref, hp_ref, w_ref, b_ref, g_ref, be_ref, c_ref, s_ref, ux, stage):
        i = pl.program_id(0)
        up = hp_ref[:, :D] * _sig(hp_ref[:, D:])
        ux[0:HALO, :] = jnp.where(i == 0, 0.0, up)
        ux[HALO:HALO + bt, :] = h_ref[:, :D] * _sig(h_ref[:, D:])
        for cb in range(D // LANE):
            cs = slice(cb * LANE, (cb + 1) * LANE)
            acc = jnp.broadcast_to(b_ref[:, cs], (bt, LANE))
            for k, u_k in _shifted(ux, cs, bt, u_off, stage.at[cb % 2]):
                acc = acc + w_ref[k:k + 1, cs] * u_k
            c_ref[:, cs] = acc
        n = _ln_stats(c_ref[...]) * g_ref[...] + be_ref[...]
        s_ref[...] = (n * _sig(n)).astype(MXU_DTYPE)

    main = pl.BlockSpec((bt, D2), lambda i: (i, 0))
    prev = pl.BlockSpec((HALO, D2), lambda i: (jnp.maximum(i * hb - 1, 0), 0))
    wspec = pl.BlockSpec((W, D), lambda i: (0, 0))
    vec = pl.BlockSpec((1, D), lambda i: (0, 0))
    row = pl.BlockSpec((bt, D), lambda i: (i, 0))
    return _call(body, name=name, grid=(T // bt,), in_specs=[main, prev, wspec, vec, vec, vec], out_specs=[row, row],
                 out_shape=[jax.ShapeDtypeStruct((T, D), F32), jax.ShapeDtypeStruct((T, D), MXU_DTYPE)],
                 scratch_shapes=[pltpu.VMEM((HALO + bt, D), F32), pltpu.VMEM((2, SUBLANES, HALO + bt, LANE), F32)],
                 args=(h1, h1, dw_w, dw_b, ln_g, ln_b), sem=("parallel",), comm=comm)


def conv_bwd_ln(name, ds, c, ln_g, ln_b, comm=None):
    T, D = c.shape
    br = _blk(T, 512)

    def body(ds_ref, c_ref, g_ref, be_ref, dc_ref, dg_ref, db_ref, dcs_ref):
        i = pl.program_id(0)
        c = c_ref[...]
        mu = jnp.mean(c, axis=-1, keepdims=True)
        cc = c - mu
        var = jnp.mean(cc * cc, axis=-1, keepdims=True)
        rstd = lax.rsqrt(var + LN_EPS)
        xhat = cc * rstd
        n = xhat * g_ref[...] + be_ref[...]
        sg = _sig(n)
        dn = ds_ref[...] * (sg * (1.0 + n * (1.0 - sg)))
        dxh = dn * g_ref[...]
        m1 = jnp.mean(dxh, axis=-1, keepdims=True)
        m2 = jnp.mean(dxh * xhat, axis=-1, keepdims=True)
        dc = rstd * (dxh - m1 - xhat * m2)
        dc_ref[...] = dc

        @pl.when(i == 0)
        def _():
            dg_ref[...] = jnp.zeros_like(dg_ref)
            db_ref[...] = jnp.zeros_like(db_ref)
            dcs_ref[...] = jnp.zeros_like(dcs_ref)

        dg_ref[...] += jnp.sum(dn * xhat, axis=0, keepdims=True)
        db_ref[...] += jnp.sum(dn, axis=0, keepdims=True)
        dcs_ref[...] += jnp.sum(dc, axis=0, keepdims=True)

    row = pl.BlockSpec((br, D), lambda i: (i, 0))
    vec = pl.BlockSpec((1, D), lambda i: (0, 0))
    vshape = jax.ShapeDtypeStruct((1, D), F32)
    outs, delivered = _call(body, name=name, grid=(T // br,), in_specs=[row, row, vec, vec],
                            out_specs=[row, vec, vec, vec],
                            out_shape=[jax.ShapeDtypeStruct((T, D), F32), vshape, vshape, vshape],
                            args=(ds, c, ln_g, ln_b), sem=("arbitrary",), comm=comm)
    return outs if comm is None else (outs, delivered)


def conv_bwd_dw(name, dc, h1, dw_w, comm=None):
    T, D2 = h1.shape
    D = D2 // 2
    W = dw_w.shape[0]
    taps = W - 1
    bt = _blk(T, CONV_BT)
    hb = bt // HALO
    n = T // bt

    def body(dc_ref, dcn_ref, h_ref, hp_ref, w_ref, dh_ref, dw_ref, dhs_ref, ux, dcx, du, stage_dc, stage_u):
        i = pl.program_id(0)
        a = h_ref[:, :D]
        sg = _sig(h_ref[:, D:])
        up = hp_ref[:, :D] * _sig(hp_ref[:, D:])
        ux[0:HALO, :] = jnp.where(i == 0, 0.0, up)
        ux[HALO:HALO + bt, :] = a * sg
        dcx[0:bt, :] = dc_ref[...]
        dcx[bt:bt + HALO, :] = jnp.where(i == n - 1, 0.0, dcn_ref[...])

        @pl.when(i == 0)
        def _():
            dw_ref[...] = jnp.zeros_like(dw_ref)
            dhs_ref[...] = jnp.zeros_like(dhs_ref)

        for cb in range(D // LANE):
            cs = slice(cb * LANE, (cb + 1) * LANE)
            dcb = dcx[0:bt, cs]
            acc = jnp.zeros((bt, LANE), F32)
            for k, dc_k in _shifted(dcx, cs, bt, [taps - 1 - k for k in range(taps)], stage_dc.at[cb % 2]):
                acc = acc + w_ref[k:k + 1, cs] * dc_k
            du[:, cs] = acc
            for k, u_k in _shifted(ux, cs, bt, [HALO - (taps - 1) + k for k in range(taps)], stage_u.at[cb % 2]):
                dw_ref[k:k + 1, cs] += jnp.sum(dcb * u_k, axis=0, keepdims=True)
        d_u = du[...]
        da = d_u * sg
        dg = d_u * a * sg * (1.0 - sg)
        dh_ref[:, :D] = da.astype(MXU_DTYPE)
        dh_ref[:, D:] = dg.astype(MXU_DTYPE)
        dhs_ref[:, :D] += jnp.sum(da, axis=0, keepdims=True)
        dhs_ref[:, D:] += jnp.sum(dg, axis=0, keepdims=True)

    row = pl.BlockSpec((bt, D), lambda i: (i, 0))
    nxt = pl.BlockSpec((HALO, D), lambda i: (jnp.minimum((i + 1) * hb, T // HALO - 1), 0))
    main = pl.BlockSpec((bt, D2), lambda i: (i, 0))
    prev = pl.BlockSpec((HALO, D2), lambda i: (jnp.maximum(i * hb - 1, 0), 0))
    wspec = pl.BlockSpec((W, D), lambda i: (0, 0))
    return _call(body, name=name, grid=(n,), in_specs=[row, nxt, main, prev, wspec],
                 out_specs=[main, wspec, pl.BlockSpec((1, D2), lambda i: (0, 0))],
                 out_shape=[jax.ShapeDtypeStruct((T, D2), MXU_DTYPE), jax.ShapeDtypeStruct((W, D), F32),
                            jax.ShapeDtypeStruct((1, D2), F32)],
                 scratch_shapes=[pltpu.VMEM((HALO + bt, D), F32), pltpu.VMEM((bt + HALO, D), F32),
                                 pltpu.VMEM((bt, D), F32), pltpu.VMEM((2, SUBLANES, HALO + bt, LANE), F32),
                                 pltpu.VMEM((2, SUBLANES, HALO + bt, LANE), F32)],
                 args=(dc, dc, h1, h1, dw_w), sem=("arbitrary",), comm=comm)


def _t5_bucket(dist, n_buckets):
    max_exact = n_buckets // 2
    large = max_exact + (np.log(np.maximum(dist, 1) / max_exact) / math.log(REL_MAX_DIST / max_exact)
                         * (n_buckets - max_exact)).astype(np.int32)
    large = np.minimum(large, n_buckets - 1)
    return np.where(dist < max_exact, dist, large).astype(np.int32)


def _band_tables(n_buckets):
    i = np.arange(BAND)[:, None]
    j = np.arange(2 * BAND)[None, :]
    delta = i - j + BAND
    out = []
    for window, dil in BRANCHES:
        ok = (delta >= 0) & (delta <= window // dil)
        out.append((_t5_bucket(np.clip(delta, 0, None) * dil, n_buckets), ok))
    return out


def _units():
    for bi in (2, 1, 0):
        d = BRANCHES[bi][1]
        for r in range(d):
            for nb in range(ATT_TB // (BAND * d)):
                yield bi, d, r, nb


def _rows(start, size, d):
    return pl.ds(start, size) if d == 1 else pl.ds(start, size, stride=d)


def _bc(v):
    return jnp.broadcast_to(v, (BAND, LANE))


def attn_fwd(name, q, kv, bias, comm=None):
    T, D = q.shape
    H = D // HEAD_DIM
    TB = ATT_TB
    scale = HEAD_DIM ** -0.5

    def body(q_ref, kc_ref, kp_ref, vc_ref, vp_ref, b_ref, o_ref, l_ref, kx, vx, m_sc, s_sc, a_sc):
        first = (pl.program_id(1) == 0).astype(jnp.int32)
        kx[0:TB, :] = kp_ref[...]
        kx[TB:2 * TB, :] = kc_ref[...]
        vx[0:TB, :] = vp_ref[...]
        vx[TB:2 * TB, :] = vc_ref[...]
        for bi, d, r, nb in _units():
            qs = _rows(nb * BAND * d + r, BAND, d)
            ks = _rows(TB + (nb - 1) * BAND * d + r, 2 * BAND, d)
            qb = q_ref[qs, :].astype(MXU_DTYPE)
            kb = kx[ks, :].astype(MXU_DTYPE)
            vb = vx[ks, :].astype(MXU_DTYPE)
            b = b_ref[bi + 3 * first] if nb == 0 else b_ref[bi]
            s = lax.dot_general(qb, kb, NT, preferred_element_type=F32) * scale + b
            mrow = jnp.max(s, axis=1, keepdims=True)
            p = jnp.exp(s - mrow)
            m_sc.at[bi][qs, :] = _bc(mrow)
            s_sc.at[bi][qs, :] = _bc(jnp.sum(p, axis=1, keepdims=True))
            a_sc.at[bi][qs, :] = lax.dot_general(p.astype(MXU_DTYPE), vb, NN, preferred_element_type=F32)
        for c0 in range(0, TB, ATT_MERGE_ROWS):
            rows = slice(c0, c0 + ATT_MERGE_ROWS)
            ms = [m_sc[bi, rows, :] for bi in range(3)]
            m = jnp.maximum(jnp.maximum(ms[0], ms[1]), ms[2])
            ws = [jnp.exp(mi - m) for mi in ms]
            den = ws[0] * s_sc[0, rows, :] + ws[1] * s_sc[1, rows, :] + ws[2] * s_sc[2, rows, :]
            num = ws[0] * a_sc[0, rows, :] + ws[1] * a_sc[1, rows, :] + ws[2] * a_sc[2, rows, :]
            o_ref[rows, :] = num / den
            l_ref[rows, :] = m + jnp.log(den)

    cur = lambda off: pl.BlockSpec((TB, HEAD_DIM), lambda h, i: (i, off + h))
    prv = lambda off: pl.BlockSpec((TB, HEAD_DIM), lambda h, i: (jnp.maximum(i - 1, 0), off + h))
    bspec = pl.BlockSpec((6, None, BAND, 2 * BAND), lambda h, i: (0, h, 0, 0))
    sc = lambda *shape: pltpu.VMEM(shape + (HEAD_DIM,), F32)
    return _call(body, name=name, grid=(H, T // TB), in_specs=[cur(0), cur(0), prv(0), cur(H), prv(H), bspec],
                 out_specs=[cur(0), cur(0)],
                 out_shape=[jax.ShapeDtypeStruct((T, D), F32), jax.ShapeDtypeStruct((T, D), F32)],
                 scratch_shapes=[sc(2 * TB), sc(2 * TB), sc(3, TB), sc(3, TB), sc(3, TB)],
                 args=(q, kv, kv, kv, kv, bias), sem=("parallel", "parallel"), comm=comm)


def attn_bwd(name, q, kv, bias, o, lse, do, comm=None):
    T, D = q.shape
    H = D // HEAD_DIM
    TB = ATT_TB
    nI = T // TB
    scale = HEAD_DIM ** -0.5

    def body(q_ref, kc_ref, kp_ref, vc_ref, vp_ref, b_ref, o_ref, l_ref, do_ref,
             dq_ref, dk_ref, dv_ref, db_ref, kx, vx, dkx, dvx, ck, cv, dl_sc, dq_sc):
        step = pl.program_id(1)
        first = (step == nI - 1).astype(jnp.int32)
        kx[0:TB, :] = kp_ref[...]
        kx[TB:2 * TB, :] = kc_ref[...]
        vx[0:TB, :] = vp_ref[...]
        vx[TB:2 * TB, :] = vc_ref[...]
        dl_sc[...] = jnp.broadcast_to(jnp.sum(do_ref[...] * o_ref[...], axis=1, keepdims=True), (TB, LANE))

        @pl.when(step == 0)
        def _():
            db_ref[...] = jnp.zeros_like(db_ref)
            ck[...] = jnp.zeros_like(ck)
            cv[...] = jnp.zeros_like(cv)

        held_k = held_v = None
        for bi, d, r, nb in _units():
            band = lambda b: _rows(TB + b * BAND * d + r, BAND, d)
            qs = _rows(nb * BAND * d + r, BAND, d)
            ks = _rows(TB + (nb - 1) * BAND * d + r, 2 * BAND, d)
            qb = q_ref[qs, :].astype(MXU_DTYPE)
            kb = kx[ks, :].astype(MXU_DTYPE)
            vb = vx[ks, :].astype(MXU_DTYPE)
            dob = do_ref[qs, :].astype(MXU_DTYPE)
            b = b_ref[bi + 3 * first] if nb == 0 else b_ref[bi]
            s = lax.dot_general(qb, kb, NT, preferred_element_type=F32) * scale + b
            p = jnp.exp(s - l_ref[qs, :][:, :1])
            dp = lax.dot_general(dob, vb, NT, preferred_element_type=F32)
            dsv = p * (dp - dl_sc[qs, :][:, :1])
            db_ref[bi] += dsv
            dsb = dsv.astype(MXU_DTYPE)
            dv_blk = lax.dot_general(p.astype(MXU_DTYPE), dob, TN, preferred_element_type=F32)
            dk_blk = lax.dot_general(dsb, qb, TN, preferred_element_type=F32) * scale
            dq_sc.at[bi][qs, :] = lax.dot_general(dsb, kb, NN, preferred_element_type=F32) * scale
            dkb, dvb = dkx.at[bi], dvx.at[bi]
            if nb == 0:
                dkb[band(-1), :] = dk_blk[:BAND]
                dvb[band(-1), :] = dv_blk[:BAND]
            else:
                dkb[band(nb - 1), :] = held_k + dk_blk[:BAND]
                dvb[band(nb - 1), :] = held_v + dv_blk[:BAND]
            held_k, held_v = dk_blk[BAND:], dv_blk[BAND:]
            if nb == TB // (BAND * d) - 1:
                dkb[band(nb), :] = held_k
                dvb[band(nb), :] = held_v

        dq_ref[...] = (dq_sc[0] + dq_sc[1] + dq_sc[2]).astype(dq_ref.dtype)
        for dx_ref, dxx, cx in ((dk_ref, dkx, ck), (dv_ref, dvx, cv)):
            dx_ref[...] = (dxx[0, TB:2 * TB, :] + dxx[1, TB:2 * TB, :] + dxx[2, TB:2 * TB, :] + cx[...]).astype(dx_ref.dtype)
            cx[...] = dxx[2, 0:TB, :]
            for bi in (1, 0):
                lo = TB - BAND * BRANCHES[bi][1]
                cx[lo:TB, :] += dxx[bi, lo:TB, :]

    blk = lambda h, i: nI - 1 - i
    cur = lambda off: pl.BlockSpec((TB, HEAD_DIM), lambda h, i: (blk(h, i), off + h))
    prv = lambda off: pl.BlockSpec((TB, HEAD_DIM), lambda h, i: (jnp.maximum(blk(h, i) - 1, 0), off + h))
    bspec = pl.BlockSpec((3, None, BAND, 2 * BAND), lambda h, i: (0, h, 0, 0))
    bspec_in = pl.BlockSpec((6, None, BAND, 2 * BAND), lambda h, i: (0, h, 0, 0))
    sc = lambda *shape: pltpu.VMEM(shape + (HEAD_DIM,), F32)
    return _call(body, name=name, grid=(H, nI),
                 in_specs=[cur(0), cur(0), prv(0), cur(H), prv(H), bspec_in, cur(0), cur(0), cur(0)],
                 out_specs=[cur(0), cur(0), cur(0), bspec],
                 out_shape=[jax.ShapeDtypeStruct((T, D), MXU_DTYPE), jax.ShapeDtypeStruct((T, D), MXU_DTYPE),
                            jax.ShapeDtypeStruct((T, D), MXU_DTYPE), jax.ShapeDtypeStruct((3, H, BAND, 2 * BAND), F32)],
                 scratch_shapes=[sc(2 * TB), sc(2 * TB), sc(3, 2 * TB), sc(3, 2 * TB), sc(TB), sc(TB), sc(TB), sc(3, TB)],
                 args=(q, kv, kv, kv, kv, bias, o, lse, do), sem=("arbitrary", "arbitrary"), comm=comm)


def adamw(name, parts, w, m, v):
    L, R, C = w.shape
    P = parts[0].shape[0]
    br = R if R % 8 else _blk(R, max(8, (1 << 18) // C))
    c1 = 1.0 / (1.0 - ADAM_B1 ** ADAM_STEP)
    c2 = 1.0 / (1.0 - ADAM_B2 ** ADAM_STEP)

    def body(*refs):
        p_refs = refs[:L]
        w_ref, m_ref, v_ref, g_ref, d_ref, nm_ref, nv_ref = refs[L:]
        lay = pl.program_id(0)

        def total(p_ref):
            g = p_ref[0].astype(F32)
            for k in range(1, P):
                g = g + p_ref[k].astype(F32)
            return g

        g = total(p_refs[0])
        for j in range(1, L):
            g = jnp.where(lay == j, total(p_refs[j]), g)
        nm = ADAM_B1 * m_ref[...] + (1.0 - ADAM_B1) * g
        nv = ADAM_B2 * v_ref[...] + (1.0 - ADAM_B2) * (g * g)
        g_ref[...] = g
        nm_ref[...] = nm
        nv_ref[...] = nv
        d_ref[...] = -ADAM_LR * ((nm * c1) / (jnp.sqrt(nv * c2) + ADAM_EPS) + ADAM_WD * w_ref[...])

    row = pl.BlockSpec((None, br, C), lambda l, i: (l, i, 0))
    pspecs = [pl.BlockSpec((P, br, C), lambda l, i, j=j: (0, jnp.where(l == j, i, 0), 0)) for j in range(L)]
    shp = jax.ShapeDtypeStruct((L, R, C), F32)
    return pl.pallas_call(body, name=name, grid=(L, R // br), in_specs=[*pspecs, row, row, row],
                          out_specs=[row] * 4, out_shape=[shp] * 4,
                          compiler_params=_cp(("parallel", "parallel")))(*parts, w, m, v)


def pair_sum(name, g, s1, c_idx):
    _, _, R, C = g.shape
    br = _blk(R, max(16, (1 << 19) // C))

    def body(c_ref, g_ref, s_ref, t_ref):
        t_ref[...] = (g_ref[...].astype(F32) + s_ref[...].astype(F32)).astype(t_ref.dtype)

    return pl.pallas_call(
        body, name=name,
        grid_spec=pltpu.PrefetchScalarGridSpec(
            num_scalar_prefetch=1, grid=(4, R // br),
            in_specs=[pl.BlockSpec((None, None, br, C), lambda j, i, c: (j, c[0], i, 0)),
                      pl.BlockSpec((None, br, C), lambda j, i, c: (j, i, 0))],
            out_specs=pl.BlockSpec((None, br, C), lambda j, i, c: (j, i, 0))),
        out_shape=jax.ShapeDtypeStruct((4, R, C), g.dtype), compiler_params=_cp(("parallel", "parallel")))(c_idx, g, s1)


def _me():
    return lax.axis_index("x"), lax.axis_index("y"), lax.axis_index("c")


ANY = pl.BlockSpec(memory_space=pl.ANY)


class Gather:
    relay_at = 0.7

    def __init__(self, shards):
        n = len(shards)
        self.arrays = list(shards)
        self.out_shapes = [jax.ShapeDtypeStruct((N_DEV,) + s.shape, s.dtype) for s in shards]
        self.sems = [pltpu.SemaphoreType.DMA((n, 7)), pltpu.SemaphoreType.DMA((n, 7)), pltpu.SemaphoreType.DMA((n,))]

    def _ctx(self, ins, outs, sems):
        send_sems, recv_sems, local_sems = sems
        x, y, c = _me()
        chips = [(1 - x, y), (x, 1 - y), (1 - x, 1 - y)]

        def copy(a, k, block, to, src=None):
            px, py, pc = block
            dst = outs[a].at[4 * px + 2 * py + pc]
            return pltpu.make_async_remote_copy(src_ref=dst if src is None else src, dst_ref=dst,
                                                send_sem=send_sems.at[a, k], recv_sem=recv_sems.at[a, k],
                                                device_id=to, device_id_type=MESH)

        def own(a):
            return pltpu.make_async_copy(ins[a], outs[a].at[4 * x + 2 * y + c], local_sems.at[a])

        def first(a):
            return [copy(a, 0, (x, y, c), (x, y, 1 - c), src=ins[a])] + \
                   [copy(a, 1 + j, (x, y, c), (*chip, c), src=ins[a]) for j, chip in enumerate(chips)]

        def passed(a):
            return [copy(a, 4 + j, (*chip, c), (x, y, 1 - c)) for j, chip in enumerate(chips)]

        return (x, y, c), chips, copy, own, first, passed

    def start(self, ins, outs, sems):
        _, _, _, own, first, _ = self._ctx(ins, outs, sems)
        for a in range(len(ins)):
            own(a).start()
            for cp in first(a):
                cp.start()

    def relay(self, ins, outs, sems):
        me, chips, copy, _, _, passed = self._ctx(ins, outs, sems)
        for a in range(len(ins)):
            fwd = passed(a)
            for j, chip in enumerate(chips):
                copy(a, 1 + j, (*chip, me[2]), me).wait_recv()
                fwd[j].start()

    def wait(self, ins, outs, sems):
        (x, y, c), chips, copy, own, first, passed = self._ctx(ins, outs, sems)
        for a in range(len(ins)):
            copy(a, 0, (x, y, 1 - c), (x, y, c)).wait_recv()
            for j, chip in enumerate(chips):
                copy(a, 4 + j, (*chip, 1 - c), (x, y, c)).wait_recv()
            for cp in first(a) + passed(a):
                cp.wait_send()
            own(a).wait()


def all_gather(name, shards):
    comm = Gather(shards)
    n = len(shards)

    def body(*refs):
        comm.start(refs[:n], refs[n:2 * n], refs[2 * n:])
        comm.relay(refs[:n], refs[n:2 * n], refs[2 * n:])
        comm.wait(refs[:n], refs[n:2 * n], refs[2 * n:])

    return pl.pallas_call(body, name=name, in_specs=[ANY] * n, out_specs=[ANY] * n, out_shape=comm.out_shapes,
                          scratch_shapes=comm.sems)(*shards)


class PairExchange:
    relay_at = None

    def __init__(self, grads):
        n = len(grads)
        self.arrays = list(grads)
        self.out_shapes = [jax.ShapeDtypeStruct((4,) + g.shape[2:], g.dtype) for g in grads]
        self.sems = [pltpu.SemaphoreType.DMA((n, 4)), pltpu.SemaphoreType.DMA((n, 4))]

    def _copies(self, ins, outs, sems):
        send_sems, recv_sems = sems
        x, y, c = _me()
        return [pltpu.make_async_remote_copy(
            src_ref=ins[a].at[j, 1 - c], dst_ref=outs[a].at[j], send_sem=send_sems.at[a, j],
            recv_sem=recv_sems.at[a, j], device_id=(x, y, 1 - c), device_id_type=MESH)
            for a in range(len(ins)) for j in range(4)]

    def start(self, ins, outs, sems):
        for cp in self._copies(ins, outs, sems):
            cp.start()

    def wait(self, ins, outs, sems):
        for cp in self._copies(ins, outs, sems):
            cp.wait()


def pair_exchange(name, grads):
    comm = PairExchange(grads)
    n = len(grads)

    def body(*refs):
        comm.start(refs[:n], refs[n:2 * n], refs[2 * n:])
        comm.wait(refs[:n], refs[n:2 * n], refs[2 * n:])

    return pl.pallas_call(body, name=name, in_specs=[ANY] * n, out_specs=[ANY] * n, out_shape=comm.out_shapes,
                          scratch_shapes=comm.sems)(*grads)


class ChipExchange:
    relay_at = None

    def __init__(self, sums):
        n = len(sums)
        self.arrays = list(sums)
        self.out_shapes = [jax.ShapeDtypeStruct(s.shape, s.dtype) for s in sums]
        self.sems = [pltpu.SemaphoreType.DMA((n, 3)), pltpu.SemaphoreType.DMA((n, 3)), pltpu.SemaphoreType.DMA((n,))]

    def _copies(self, ins, outs, sems, arrivals):
        send_sems, recv_sems, local_sems = sems
        x, y, c = _me()
        my = 2 * x + y
        chips = [(1 - x, y), (x, 1 - y), (1 - x, 1 - y)]
        mine, sends, recvs = [], [], []
        for a in range(len(ins)):
            mine.append(pltpu.make_async_copy(ins[a].at[my], outs[a].at[my], local_sems.at[a]))
            for j, (px, py) in enumerate(chips):
                sends.append(pltpu.make_async_remote_copy(
                    src_ref=ins[a].at[2 * px + py], dst_ref=outs[a].at[my],
                    send_sem=send_sems.at[a, j], recv_sem=recv_sems.at[a, j],
                    device_id=(px, py, c), device_id_type=MESH))
                if arrivals:
                    recvs.append(pltpu.make_async_remote_copy(
                        src_ref=ins[a].at[my], dst_ref=outs[a].at[2 * px + py],
                        send_sem=send_sems.at[a, j], recv_sem=recv_sems.at[a, j],
                        device_id=(px, py, c), device_id_type=MESH))
        return mine, sends, recvs

    def start(self, ins, outs, sems):
        mine, sends, _ = self._copies(ins, outs, sems, arrivals=False)
        for cp in mine + sends:
            cp.start()

    def wait(self, ins, outs, sems):
        mine, sends, recvs = self._copies(ins, outs, sems, arrivals=True)
        for cp in recvs:
            cp.wait_recv()
        for cp in sends:
            cp.wait_send()
        for cp in mine:
            cp.wait()


def all_reduce_small(name, pack):
    R, C = pack.shape

    def body(p_ref, o_ref, buf, send_sems, recv_sems):
        x, y, c = _me()
        me = 4 * x + 2 * y + c
        buf[me] = p_ref[...]
        copies = []
        for k in range(1, N_DEV):
            px, py, pc = x ^ (k >> 2), y ^ ((k >> 1) & 1), c ^ (k & 1)
            copies.append(pltpu.make_async_remote_copy(
                src_ref=p_ref, dst_ref=buf.at[me], send_sem=send_sems.at[k - 1], recv_sem=recv_sems.at[k - 1],
                device_id=(px, py, pc), device_id_type=MESH))
        for cp in copies:
            cp.start()
        for cp in copies:
            cp.wait()
        acc = buf[0]
        for d in range(1, N_DEV):
            acc = acc + buf[d]
        o_ref[...] = acc

    vm = pl.BlockSpec(memory_space=pltpu.VMEM)
    return pl.pallas_call(
        body, name=name, in_specs=[vm], out_specs=vm, out_shape=jax.ShapeDtypeStruct((R, C), F32),
        scratch_shapes=[pltpu.VMEM((N_DEV, R, C), F32), pltpu.SemaphoreType.DMA((N_DEV - 1,)),
                        pltpu.SemaphoreType.DMA((N_DEV - 1,))],
    )(pack)


def _ep_bias(acc, ex, outs):
    outs[0][...] = acc + ex[0][...]


def _ep_store(acc, ex, outs):
    outs[0][...] = acc.astype(outs[0].dtype)


def _ep_resid(alpha, bias):
    def ep(acc, ex, outs):
        if bias:
            outs[0][...] = alpha * ex[1][...] + (acc + ex[0][...])
        else:
            outs[0][...] = alpha * ex[0][...] + acc
    return ep


def _ep_relu2(acc, ex, outs):
    r = jnp.maximum(acc, 0.0)
    outs[0][...] = (r * r).astype(outs[0].dtype)
    outs[1][...] = r


def _ep_relu2_bwd(acc, ex, outs):
    outs[0][...] = (acc * (2.0 * ex[0][...])).astype(outs[0].dtype)


def _ep_add(acc, ex, outs):
    outs[0][...] = ex[0][...] + acc


def kernel(x, conv_pw1_w, conv_pw1_b, conv_dw_w, conv_dw_b, conv_ln_g, conv_ln_b, conv_pw2_w, conv_pw2_b, w_kv, attn_wq, attn_wo, rel_bias, mlp_w1, mlp_w2, ln_mix_g, ln_mix_b, ln_mlp_g, ln_mlp_b, loss_target, m_conv_pw1_w, m_conv_pw1_b, m_conv_dw_w, m_conv_dw_b, m_conv_ln_g, m_conv_ln_b, m_conv_pw2_w, m_conv_pw2_b, m_w_kv, m_attn_wq, m_attn_wo, m_rel_bias, m_mlp_w1, m_mlp_w2, m_ln_mix_g, m_ln_mix_b, m_ln_mlp_g, m_ln_mlp_b, v_conv_pw1_w, v_conv_pw1_b, v_conv_dw_w, v_conv_dw_b, v_conv_ln_g, v_conv_ln_b, v_conv_pw2_w, v_conv_pw2_b, v_w_kv, v_attn_wq, v_attn_wo, v_rel_bias, v_mlp_w1, v_mlp_w2, v_ln_mix_g, v_ln_mix_b, v_ln_mlp_g, v_ln_mlp_b):
    T, D = x.shape[1], x.shape[2]
    H = D // HEAD_DIM
    depth = mlp_w1.shape[0]
    assert depth == 2 and T % ATT_TB == 0
    alpha = (2 * depth) ** 0.25
    ds_ = D // N_DEV
    xi, yi, ci = _me()
    me = 4 * xi + 2 * yi + ci
    c_idx = ci.astype(jnp.int32).reshape(1)
    x2 = x.reshape(T, D)
    target = loss_target.reshape(T, D)
    n_buckets = rel_bias.shape[0]
    taps = conv_dw_w.shape[1]

    local = {
        "pw1": (conv_pw1_w, 0), "pw2": (conv_pw2_w, 0), "wkv": (w_kv, None), "wq": (attn_wq, 0), "wo": (attn_wo, 0),
        "w1_0": (mlp_w1, 0), "w1_1": (mlp_w1, 1), "w2_0": (mlp_w2, 0), "w2_1": (mlp_w2, 1),
    }
    names = list(local)
    small = jnp.concatenate([conv_dw_w[0], conv_dw_b, conv_ln_g, conv_ln_b, conv_pw2_b,
                             conv_pw1_b.reshape(2, ds_)], axis=0)
    small = jnp.pad(small, ((0, (-small.shape[0]) % 8), (0, 0)))
    shard = {k: cast_bf16("cast_" + k, *local[k]) for k in names}
    G = {}

    def gather_start(*keys):
        return Gather([shard[k] for k in keys])

    def gather_done(keys, delivered):
        G.update(zip(keys, delivered))

    g_pw1, g_small = all_gather("gather_pw1", [shard["pw1"], small])
    G["pw1"] = g_pw1
    sm = jnp.transpose(g_small, (1, 0, 2)).reshape(small.shape[0], D)
    dw_w = jnp.pad(sm[:taps], ((0, HALO - taps), (0, 0)))
    dw_b, cln_g, cln_b, pw2_b = (sm[taps + k:taps + k + 1] for k in range(4))
    pw1_b = g_small[:, taps + 4:taps + 6, :].reshape(1, 2 * D)

    row = lambda a, l: a[l:l + 1]

    x_b = cast_bf16("cast_x", x2)
    (h1,) = mm_fwd("pw1", x_b, G["pw1"], colsharded=True, epilogue=_ep_bias, outs=[F32], rowvec=[pw1_b])
    (c_pre, s_b), delivered = conv_fwd("conv_fwd", h1, dw_w, dw_b, cln_g, cln_b, comm=gather_start("pw2", "w1_0"))
    gather_done(("pw2", "w1_0"), delivered)
    (z1,) = mm_fwd("pw2", s_b, G["pw2"], colsharded=False, epilogue=_ep_resid(alpha, True), outs=[F32],
                   rowvec=[pw2_b], tiles=[x2])
    x1, x1_b = ln_fwd("ln_mix0", z1, row(ln_mix_g, 0), row(ln_mix_b, 0))

    def mlp_fwd(l, xin, xin_b, up_keys=(), down_keys=()):
        res_ = mm_fwd(f"mlp_up{l}", xin_b, G[f"w1_{l}"], colsharded=True, epilogue=_ep_relu2, outs=[MXU_DTYPE, F32],
                      comm=gather_start(*up_keys) if up_keys else None)
        (act_b, r), delivered = res_ if up_keys else (res_, [])
        gather_done(up_keys, delivered)
        res_ = mm_fwd(f"mlp_down{l}", act_b, G[f"w2_{l}"], colsharded=False, epilogue=_ep_resid(alpha, False),
                      outs=[F32], tiles=[xin], comm=gather_start(*down_keys) if down_keys else None)
        ((z,), delivered) = res_ if down_keys else (res_, [])
        gather_done(down_keys, delivered)
        y, y_b = ln_fwd(f"ln_mlp{l}", z, row(ln_mlp_g, l), row(ln_mlp_b, l))
        return act_b, r, z, y, y_b

    act0_b, r0, z2, x2_, x2_b = mlp_fwd(0, x1, x1_b, up_keys=("w2_0",), down_keys=("wkv", "wq", "wo"))

    (kv,) = mm_fwd("kv_proj", x2_b, G["wkv"], colsharded=True, epilogue=_ep_store, outs=[F32])
    (q,) = mm_fwd("q_proj", x2_b, G["wq"], colsharded=False, epilogue=_ep_store, outs=[F32])
    tables = _band_tables(n_buckets)
    onehot = jnp.concatenate([(jnp.arange(n_buckets)[:, None] == jnp.asarray(bucket).reshape(1, -1)).astype(F32)
                              for bucket, _ in tables], axis=1)
    KB = onehot.shape[1] // 8
    (btab,) = _mm("rel_bias_table", rel_bias, onehot, grid=(1, 8, 1),
                  a_spec=pl.BlockSpec((n_buckets, H), lambda i, j, k: (0, 0)),
                  b_spec=pl.BlockSpec((n_buckets, KB), lambda i, j, k: (0, j)), dims=TN, k_axis=2,
                  epilogue=_ep_store, out_shapes=[jax.ShapeDtypeStruct((H, 8 * KB), F32)],
                  out_specs=[pl.BlockSpec((H, KB), lambda i, j, k: (0, j))], acc_shape=(H, KB), exact=True)
    band_ok = jnp.asarray(np.stack([ok for _, ok in tables]))[:, None]
    bias = jnp.where(band_ok, jnp.transpose(btab.reshape(H, 3, BAND, 2 * BAND), (1, 0, 2, 3)), NEG)
    has_prev = jnp.asarray(np.arange(2 * BAND) >= BAND)
    bias = jnp.concatenate([bias, jnp.where(has_prev, bias, NEG)], axis=0)
    (o, lse), delivered = attn_fwd("attn_fwd", q, kv, bias, comm=gather_start("w1_1", "w2_1"))
    gather_done(("w1_1", "w2_1"), delivered)
    (z3,) = mm_fwd("o_proj", o, G["wo"], colsharded=False, epilogue=_ep_resid(alpha, False), outs=[F32], tiles=[x2_])
    x3, x3_b = ln_fwd("ln_mix1", z3, row(ln_mix_g, 1), row(ln_mix_b, 1))
    act1_b, r1, z4, x4, _ = mlp_fwd(1, x3, x3_b)

    loss_local, dy = loss_head("loss", x4, target)
    loss = lax.psum(loss_local[0, 0], MESH_AXES)

    parts = {}

    def pair_of(grads):
        g4 = [g.reshape((4, 2) + g.shape[1:]) for g in grads.values()]
        return g4, PairExchange(g4)

    def chip_of(grads, g4, from_sibling):
        return ChipExchange([pair_sum("grad_pair_sum_" + k, g, s, c_idx) for k, g, s in zip(grads, g4, from_sibling)])

    def reduce_done(grads, delivered):
        parts.update(zip(grads, delivered))

    def mlp_bwd(l, dy_out, z, xin_b, act_b, r, pending=None):
        if pending is None:
            dz, dz_b, dg, db, _ = ln_bwd(f"ln_mlp{l}_bwd", dy_out, z, row(ln_mlp_g, l))
            comm = None
        else:
            (dz, dz_b, dg, db, _), from_sibling = ln_bwd(f"ln_mlp{l}_bwd", dy_out, z, row(ln_mlp_g, l), comm=pending[2])
            comm = chip_of(pending[0], pending[1], from_sibling)
        res_ = mm_dx(f"mlp_down{l}_dx", dz_b, G[f"w2_{l}"], colsharded=False, epilogue=_ep_relu2_bwd,
                     outs=[MXU_DTYPE], tiles=[r], comm=comm)
        ((dhm_b,), delivered) = res_ if comm is not None else (res_, [])
        dw2 = mm_dw(f"mlp_down{l}_dw", act_b, dz_b, colsharded=False)
        (dxin,) = mm_dx(f"mlp_up{l}_dx", dhm_b, G[f"w1_{l}"], colsharded=True, epilogue=_ep_resid(alpha, False),
                        outs=[F32], tiles=[dz])
        dw1 = mm_dw(f"mlp_up{l}_dw", xin_b, dhm_b, colsharded=True)
        return dxin, dw1, dw2, dg, db, delivered

    dx3, dw1_1, dw2_1, dg_mlp1, db_mlp1, _ = mlp_bwd(1, dy, z4, x3_b, act1_b, r1)
    grads1 = {"w1_1": dw1_1, "w2_1": dw2_1}
    g4_1, pair1 = pair_of(grads1)

    (dz3, dz3_b, dg_mix1, db_mix1, _), from_sibling = ln_bwd("ln_mix1_bwd", dx3, z3, row(ln_mix_g, 1), comm=pair1)
    comm1 = chip_of(grads1, g4_1, from_sibling)
    (do,) = mm_dx("o_proj_dx", dz3_b, G["wo"], colsharded=False, epilogue=_ep_store, outs=[F32])
    dwo = mm_dw("o_proj_dw", o, dz3_b, colsharded=False)
    (dq, dk, dv, dbias), delivered = attn_bwd("attn_bwd", q, kv, bias, o, lse, do, comm=comm1)
    reduce_done(grads1, delivered)
    dkv = jnp.concatenate([dk, dv], axis=1)
    (dx2a,) = mm_dx("q_proj_dx", dq, G["wq"], colsharded=False, epilogue=_ep_resid(alpha, False), outs=[F32], tiles=[dz3])
    (dx2,) = mm_dx("kv_proj_dx", dkv, G["wkv"], colsharded=True, epilogue=_ep_add, outs=[F32], tiles=[dx2a])
    dwq = mm_dw("q_proj_dw", x2_b, dq, colsharded=False)
    dwkv = mm_dw("kv_proj_dw", x2_b, dkv, colsharded=True)
    grads2 = {"wo": dwo, "wq": dwq, "wkv": dwkv}
    g4_2, pair2 = pair_of(grads2)

    dbias2 = jnp.transpose(dbias, (1, 0, 2, 3)).reshape(H, -1)
    (drel_t,) = _mm("rel_bias_grad", dbias2, onehot, grid=(1, 1, 8),
                    a_spec=pl.BlockSpec((H, KB), lambda i, j, k: (0, k)),
                    b_spec=pl.BlockSpec((n_buckets, KB), lambda i, j, k: (0, k)), dims=NT, k_axis=2,
                    epilogue=_ep_store, out_shapes=[jax.ShapeDtypeStruct((H, n_buckets), F32)],
                    out_specs=[pl.BlockSpec((H, n_buckets), lambda i, j, k: (0, 0))], acc_shape=(H, n_buckets), exact=True)

    dx1, dw1_0, dw2_0, dg_mlp0, db_mlp0, delivered = mlp_bwd(0, dx2, z2, x1_b, act0_b, r0, pending=(grads2, g4_2, pair2))
    reduce_done(grads2, delivered)

    dz1, dz1_b, dg_mix0, db_mix0, dpw2_b = ln_bwd("ln_mix0_bwd", dx1, z1, row(ln_mix_g, 0))
    (ds,) = mm_dx("pw2_dx", dz1_b, G["pw2"], colsharded=False, epilogue=_ep_store, outs=[F32])
    dwpw2 = mm_dw("pw2_dw", s_b, dz1_b, colsharded=False)
    grads3 = {"w1_0": dw1_0, "w2_0": dw2_0, "pw2": dwpw2}
    g4_3, pair3 = pair_of(grads3)
    (dc, dcln_g, dcln_b, ddw_b), from_sibling = conv_bwd_ln("conv_bwd_ln", ds, c_pre, cln_g, cln_b, comm=pair3)
    (dh1_b, ddw_w, dpw1_b), delivered = conv_bwd_dw("conv_bwd_dw", dc, h1, dw_w,
                                                   comm=chip_of(grads3, g4_3, from_sibling))
    reduce_done(grads3, delivered)
    grads4 = {"pw1": mm_dw("pw1_dw", x_b, dh1_b, colsharded=True)}
    g4_4, _ = pair_of(grads4)
    comm4 = chip_of(grads4, g4_4, pair_exchange("grad_pair_exchange_pw1", g4_4))
    (dx,), delivered = mm_dx("pw1_dx", dh1_b, G["pw1"], colsharded=True, epilogue=_ep_resid(alpha, False), outs=[F32],
                             tiles=[dz1], comm=comm4)
    reduce_done(grads4, delivered)
    grad_x = dx.reshape(1, T, D)

    vec_rows = [dg_mix0, dg_mix1, db_mix0, db_mix1, dg_mlp0, dg_mlp1, db_mlp0, db_mlp1,
                ddw_b, dcln_g, dcln_b, dpw2_b, dpw1_b.reshape(2, D), ddw_w[:taps],
                jnp.pad(jnp.transpose(drel_t).reshape(1, -1), ((0, 0), (0, D - H * n_buckets)))]
    pack = jnp.concatenate(vec_rows, axis=0)
    pack = jnp.pad(pack, ((0, (-pack.shape[0]) % 8), (0, 0)))
    tot = all_reduce_small("grad_small_all_reduce", pack)

    def mine(rows):
        return lax.dynamic_slice_in_dim(rows, me * ds_, ds_, axis=1)

    g_ln_mix_g, g_ln_mix_b, g_ln_mlp_g, g_ln_mlp_b = tot[0:2], tot[2:4], tot[4:6], tot[6:8]
    g_dw_b, g_cln_g, g_cln_b, g_pw2_b = (mine(tot[8 + k:9 + k]) for k in range(4))
    g_pw1_b = lax.dynamic_slice_in_dim(tot[12:14].reshape(1, 2 * D), me * 2 * ds_, 2 * ds_, axis=1)
    g_dw_w = mine(tot[14:14 + taps])
    g_rel = tot[14 + taps, :H * n_buckets].reshape(n_buckets, H)

    res = {}

    def upd(nm, parts_, w, m, v):
        shp = w.shape
        if not isinstance(parts_, list):
            parts_ = [parts_]
        parts_ = [p[None] if p.ndim == 2 else p for p in parts_]
        w3, m3, v3 = (a.reshape((len(parts_),) + parts_[0].shape[1:]) for a in (w, m, v))
        outs = adamw("adamw_" + nm, parts_, w3, m3, v3)
        res[nm] = tuple(o_.reshape(shp) for o_ in outs)

    upd("conv_pw1_w", parts["pw1"], conv_pw1_w, m_conv_pw1_w, v_conv_pw1_w)
    upd("conv_pw1_b", g_pw1_b, conv_pw1_b, m_conv_pw1_b, v_conv_pw1_b)
    upd("conv_dw_w", g_dw_w, conv_dw_w, m_conv_dw_w, v_conv_dw_w)
    upd("conv_dw_b", g_dw_b, conv_dw_b, m_conv_dw_b, v_conv_dw_b)
    upd("conv_ln_g", g_cln_g, conv_ln_g, m_conv_ln_g, v_conv_ln_g)
    upd("conv_ln_b", g_cln_b, conv_ln_b, m_conv_ln_b, v_conv_ln_b)
    upd("conv_pw2_w", parts["pw2"], conv_pw2_w, m_conv_pw2_w, v_conv_pw2_w)
    upd("conv_pw2_b", g_pw2_b, conv_pw2_b, m_conv_pw2_b, v_conv_pw2_b)
    upd("w_kv", parts["wkv"], w_kv, m_w_kv, v_w_kv)
    upd("attn_wq", parts["wq"], attn_wq, m_attn_wq, v_attn_wq)
    upd("attn_wo", parts["wo"], attn_wo, m_attn_wo, v_attn_wo)
    upd("rel_bias", g_rel, rel_bias, m_rel_bias, v_rel_bias)
    upd("mlp_w1", [parts["w1_0"], parts["w1_1"]], mlp_w1, m_mlp_w1, v_mlp_w1)
    upd("mlp_w2", [parts["w2_0"], parts["w2_1"]], mlp_w2, m_mlp_w2, v_mlp_w2)
    upd("ln_mix_g", g_ln_mix_g, ln_mix_g, m_ln_mix_g, v_ln_mix_g)
    upd("ln_mix_b", g_ln_mix_b, ln_mix_b, m_ln_mix_b, v_ln_mix_b)
    upd("ln_mlp_g", g_ln_mlp_g, ln_mlp_g, m_ln_mlp_g, v_ln_mlp_g)
    upd("ln_mlp_b", g_ln_mlp_b, ln_mlp_b, m_ln_mlp_b, v_ln_mlp_b)

    order = ["conv_pw1_w", "conv_pw1_b", "conv_dw_w", "conv_dw_b", "conv_ln_g", "conv_ln_b", "conv_pw2_w",
             "conv_pw2_b", "w_kv", "attn_wq", "attn_wo", "rel_bias", "mlp_w1", "mlp_w2", "ln_mix_g", "ln_mix_b",
             "ln_mlp_g", "ln_mlp_b"]
    return (loss, grad_x, *[res[n_][0] for n_ in order], *[res[n_][1] for n_ in order],
            *[res[n_][2] for n_ in order], *[res[n_][3] for n_ in order])
```

```python
import math

import numpy as np
import jax
import jax.numpy as jnp
from jax import lax
from jax.experimental import pallas as pl
from jax.experimental.pallas import tpu as pltpu

F32 = jnp.float32
MXU_DTYPE = jnp.bfloat16
GRAD_DTYPE = jnp.bfloat16
VMEM_LIMIT_BYTES = 56 * 2**20
LANE = 128
N_DEV = 8
MESH_AXES = ("x", "y", "c")
MESH = pl.DeviceIdType.MESH

HEAD_DIM = 128
BAND = 128
BRANCHES = ((128, 1), (512, 4), (2048, 16))
ATT_TB = BAND * 16
ATT_MERGE_ROWS = 256
REL_MAX_DIST = 2048
LN_EPS = 1e-5
NEG = -1e30
HALO = 32

ADAM_LR, ADAM_B1, ADAM_B2, ADAM_EPS, ADAM_WD, ADAM_STEP = 0.001, 0.9, 0.999, 1e-08, 0.01, 10

NN = (((1,), (0,)), ((), ()))
NT = (((1,), (1,)), ((), ()))
TN = (((0,), (0,)), ((), ()))


def _cp(sem=None):
    return pltpu.CompilerParams(dimension_semantics=sem, vmem_limit_bytes=VMEM_LIMIT_BYTES)


def _sig(v):
    return 1.0 / (1.0 + jnp.exp(-v))


def _call(body, *, name, grid, in_specs, out_specs, out_shape, args, sem, scratch_shapes=(), comm=None):
    if comm is None:
        res = pl.pallas_call(body, name=name, grid=grid, in_specs=list(in_specs), out_specs=list(out_specs),
                             out_shape=list(out_shape), scratch_shapes=list(scratch_shapes),
                             compiler_params=_cp(sem))(*args)
        return list(res), []
    n_in, n_out, n_sc, nc_in, nc_out = len(in_specs), len(out_specs), len(scratch_shapes), len(comm.arrays), len(comm.out_shapes)

    def wrapped(*refs):
        pos = 0
        parts = []
        for cnt in (n_in, nc_in, n_out, nc_out, n_sc):
            parts.append(refs[pos:pos + cnt])
            pos += cnt
        ins, cin, outs, cout, sc = parts
        csem = refs[pos:]
        step = pl.program_id(0)
        for ax in range(1, len(grid)):
            step = step * grid[ax] + pl.program_id(ax)
        n_steps = math.prod(grid)

        @pl.when(step == 0)
        def _():
            comm.start(cin, cout, csem)

        if comm.relay_at is not None:
            @pl.when(step == min(n_steps - 1, int(n_steps * comm.relay_at)))
            def _():
                comm.relay(cin, cout, csem)

        body(*ins, *outs, *sc)

        @pl.when(step == n_steps - 1)
        def _():
            comm.wait(cin, cout, csem)

    res = pl.pallas_call(wrapped, name=name, grid=grid, in_specs=[*in_specs, *[ANY] * nc_in],
                         out_specs=[*out_specs, *[ANY] * nc_out], out_shape=[*out_shape, *comm.out_shapes],
                         scratch_shapes=[*scratch_shapes, *comm.sems],
                         compiler_params=_cp(("arbitrary",) * len(grid)))(*args, *comm.arrays)
    return list(res[:n_out]), list(res[n_out:])


def _mm(name, a, b, *, grid, a_spec, b_spec, dims, k_axis, epilogue, out_shapes, out_specs,
        acc_shape, extra=(), extra_specs=(), exact=False, nsplit=1, ncat=1, comm=None):
    nk = grid[k_axis]
    n_extra, n_out = len(extra), len(out_shapes)

    def dot(av, bv):
        if exact:
            return lax.dot_general(av, bv, dims, precision=lax.Precision.HIGHEST, preferred_element_type=F32)
        return lax.dot_general(av.astype(MXU_DTYPE), bv.astype(MXU_DTYPE), dims, preferred_element_type=F32)

    def product(a_ref, b_ref):
        if ncat > 1:
            return jnp.concatenate([dot(a_ref[...], b_ref[g]) for g in range(ncat)], axis=1)
        if nsplit == 1:
            return dot(a_ref[...], b_ref[...])
        w = a_ref.shape[1] // nsplit
        part = dot(a_ref[:, 0:w], b_ref[0])
        for s in range(1, nsplit):
            part = part + dot(a_ref[:, s * w:(s + 1) * w], b_ref[s])
        return part

    def body(a_ref, b_ref, *rest):
        ex, outs = rest[:n_extra], rest[n_extra:n_extra + n_out]
        if nk == 1:
            epilogue(product(a_ref, b_ref), ex, outs)
        else:
            acc = rest[n_extra + n_out]
            k = pl.program_id(k_axis)

            @pl.when(k == 0)
            def _():
                acc[...] = product(a_ref, b_ref)

            if nk > 2:
                @pl.when(jnp.logical_and(k > 0, k < nk - 1))
                def _():
                    acc[...] += product(a_ref, b_ref)

            @pl.when(k == nk - 1)
            def _():
                epilogue(acc[...] + product(a_ref, b_ref), ex, outs)

    sem = tuple("arbitrary" if ax == k_axis else "parallel" for ax in range(len(grid)))
    outs, couts = _call(body, name=name, grid=grid, in_specs=[a_spec, b_spec, *extra_specs], out_specs=out_specs,
                        out_shape=out_shapes, args=(a, b, *extra), sem=sem,
                        scratch_shapes=[pltpu.VMEM(acc_shape, F32)] if nk > 1 else [], comm=comm)
    return outs if comm is None else (outs, couts)


def _blk(n, want):
    return min(n, want)


MM_BLOCK = 1024
MM_K = 2048


def mm_fwd(name, x, w, *, colsharded, epilogue, outs, rowvec=(), tiles=(), comm=None):
    T, K = x.shape
    bm = _blk(T, MM_BLOCK)
    ncat = 1
    if colsharded:
        n_s = w.shape[2]
        N = N_DEV * n_s
        bk = K
        if n_s < MM_BLOCK and (N_DEV * n_s) % MM_BLOCK == 0:
            ncat = MM_BLOCK // n_s
            bn = MM_BLOCK
            b_spec = pl.BlockSpec((ncat, K, n_s), lambda i, j, k: (j, 0, 0))
        else:
            bn = _blk(n_s, MM_BLOCK)
            per = n_s // bn
            b_spec = pl.BlockSpec((None, K, bn), lambda i, j, k: (j // per, 0, j % per))
    else:
        N = w.shape[2]
        w = w.reshape(K, N)
        bn = _blk(N, MM_BLOCK)
        bk = _blk(K, MM_K)
        b_spec = pl.BlockSpec((bk, bn), lambda i, j, k: (k, j))
    grid = (T // bm, N // bn, K // bk)
    a_spec = pl.BlockSpec((bm, bk), lambda i, j, k: (i, k))
    tile_spec = pl.BlockSpec((bm, bn), lambda i, j, k: (i, j))
    vec_spec = pl.BlockSpec((1, bn), lambda i, j, k: (0, j))
    return _mm(name, x, w, grid=grid, a_spec=a_spec, b_spec=b_spec, dims=NN, k_axis=2, epilogue=epilogue,
               out_shapes=[jax.ShapeDtypeStruct((T, N), dt) for dt in outs], out_specs=[tile_spec] * len(outs),
               acc_shape=(bm, bn), extra=(*rowvec, *tiles),
               extra_specs=[vec_spec] * len(rowvec) + [tile_spec] * len(tiles), ncat=ncat, comm=comm)


def mm_dx(name, dy, w, *, colsharded, epilogue, outs, tiles=(), comm=None):
    T, N = dy.shape
    bm = _blk(T, MM_BLOCK)
    if colsharded:
        K, n_s = w.shape[1], w.shape[2]
        bko = _blk(K, MM_BLOCK)
        spk = max(1, min(N_DEV, MM_K // n_s))
        grid = (T // bm, K // bko, N_DEV // spk)
        a_spec = pl.BlockSpec((bm, spk * n_s), lambda i, j, s: (i, s))
        b_spec = pl.BlockSpec((spk, bko, n_s), lambda i, j, s: (s, j, 0))
    else:
        K = w.shape[1] * N_DEV
        w = w.reshape(K, N)
        bko = _blk(K, MM_BLOCK)
        spk = 1
        grid = (T // bm, K // bko, 1)
        a_spec = pl.BlockSpec((bm, N), lambda i, j, s: (i, 0))
        b_spec = pl.BlockSpec((bko, N), lambda i, j, s: (j, 0))
    tile_spec = pl.BlockSpec((bm, bko), lambda i, j, s: (i, j))
    if spk == 1 and colsharded:
        b_spec = pl.BlockSpec((None, bko, n_s), lambda i, j, s: (s, j, 0))
    return _mm(name, dy, w, grid=grid, a_spec=a_spec, b_spec=b_spec, dims=NT, k_axis=2, epilogue=epilogue,
               out_shapes=[jax.ShapeDtypeStruct((T, K), dt) for dt in outs], out_specs=[tile_spec] * len(outs),
               acc_shape=(bm, bko), extra=tuple(tiles), extra_specs=[tile_spec] * len(tiles), nsplit=spk, comm=comm)


def mm_dw(name, x, dy, *, colsharded, comm=None):
    T, K = x.shape
    N = dy.shape[1]
    narrow = x.dtype.itemsize == 2 and dy.dtype.itemsize == 2
    bt = _blk(T, 2 * MM_K if narrow else MM_K)
    bmo = _blk(K, MM_BLOCK)
    epilogue = _ep_store
    if colsharded:
        n_s = N // N_DEV
        out_shape = jax.ShapeDtypeStruct((N_DEV, K, n_s), GRAD_DTYPE)
        if n_s < MM_BLOCK and N % MM_BLOCK == 0:
            group = MM_BLOCK // n_s
            bno = MM_BLOCK
            out_spec = pl.BlockSpec((group, bmo, n_s), lambda i, j, t: (j, i, 0))

            def epilogue(acc, ex, outs):
                for g in range(group):
                    outs[0][g] = acc[:, g * n_s:(g + 1) * n_s].astype(outs[0].dtype)
        else:
            bno = _blk(n_s, MM_BLOCK)
            per = n_s // bno
            out_spec = pl.BlockSpec((None, bmo, bno), lambda i, j, t: (j // per, i, j % per))
    else:
        bno = _blk(N, MM_BLOCK)
        out_shape = jax.ShapeDtypeStruct((K, N), GRAD_DTYPE)
        out_spec = pl.BlockSpec((bmo, bno), lambda i, j, t: (i, j))
    grid = (K // bmo, N // bno, T // bt)
    a_spec = pl.BlockSpec((bt, bmo), lambda i, j, t: (t, i))
    b_spec = pl.BlockSpec((bt, bno), lambda i, j, t: (t, j))
    res = _mm(name, x, dy, grid=grid, a_spec=a_spec, b_spec=b_spec, dims=TN, k_axis=2, epilogue=epilogue,
              out_shapes=[out_shape], out_specs=[out_spec], acc_shape=(bmo, bno), comm=comm)
    (out,), couts = res if comm is not None else (res, [])
    out = out if colsharded else out.reshape(N_DEV, K // N_DEV, N)
    return out if comm is None else (out, couts)


def cast_bf16(name, a, layer=None):
    R, C = a.shape[-2:]
    br = _blk(R, 512)

    def body(a_ref, o_ref):
        o_ref[...] = a_ref[...].astype(MXU_DTYPE)

    spec = pl.BlockSpec((br, C), lambda i: (i, 0))
    in_spec = spec if layer is None else pl.BlockSpec((None, br, C), lambda i: (layer, i, 0))
    return pl.pallas_call(body, name=name, grid=(R // br,), in_specs=[in_spec], out_specs=spec,
                          out_shape=jax.ShapeDtypeStruct((R, C), MXU_DTYPE), compiler_params=_cp(("parallel",)))(a)


def _ln_stats(z):
    mu = jnp.mean(z, axis=-1, keepdims=True)
    zc = z - mu
    var = jnp.mean(zc * zc, axis=-1, keepdims=True)
    return zc * lax.rsqrt(var + LN_EPS)


def ln_fwd(name, z, g, b):
    T, D = z.shape
    br = _blk(T, 512)

    def body(z_ref, g_ref, b_ref, y_ref, yb_ref):
        y = _ln_stats(z_ref[...]) * g_ref[...] + b_ref[...]
        y_ref[...] = y
        yb_ref[...] = y.astype(MXU_DTYPE)

    row = pl.BlockSpec((br, D), lambda i: (i, 0))
    vec = pl.BlockSpec((1, D), lambda i: (0, 0))
    return pl.pallas_call(body, name=name, grid=(T // br,), in_specs=[row, vec, vec], out_specs=[row, row],
                          out_shape=[jax.ShapeDtypeStruct((T, D), F32), jax.ShapeDtypeStruct((T, D), MXU_DTYPE)],
                          compiler_params=_cp(("parallel",)))(z, g, b)


def ln_bwd(name, dy, z, g, comm=None):
    T, D = z.shape
    br = _blk(T, 512)

    def body(dy_ref, z_ref, g_ref, dz_ref, dzb_ref, dg_ref, db_ref, ds_ref):
        i = pl.program_id(0)
        z = z_ref[...]
        dy = dy_ref[...]
        mu = jnp.mean(z, axis=-1, keepdims=True)
        zc = z - mu
        var = jnp.mean(zc * zc, axis=-1, keepdims=True)
        rstd = lax.rsqrt(var + LN_EPS)
        xhat = zc * rstd
        dxh = dy * g_ref[...]
        m1 = jnp.mean(dxh, axis=-1, keepdims=True)
        m2 = jnp.mean(dxh * xhat, axis=-1, keepdims=True)
        dz = rstd * (dxh - m1 - xhat * m2)
        dz_ref[...] = dz
        dzb_ref[...] = dz.astype(MXU_DTYPE)

        @pl.when(i == 0)
        def _():
            dg_ref[...] = jnp.zeros_like(dg_ref)
            db_ref[...] = jnp.zeros_like(db_ref)
            ds_ref[...] = jnp.zeros_like(ds_ref)

        dg_ref[...] += jnp.sum(dy * xhat, axis=0, keepdims=True)
        db_ref[...] += jnp.sum(dy, axis=0, keepdims=True)
        ds_ref[...] += jnp.sum(dz, axis=0, keepdims=True)

    row = pl.BlockSpec((br, D), lambda i: (i, 0))
    vec = pl.BlockSpec((1, D), lambda i: (0, 0))
    vshape = jax.ShapeDtypeStruct((1, D), F32)
    outs, delivered = _call(body, name=name, grid=(T // br,), in_specs=[row, row, vec],
                            out_specs=[row, row, vec, vec, vec],
                            out_shape=[jax.ShapeDtypeStruct((T, D), F32), jax.ShapeDtypeStruct((T, D), MXU_DTYPE),
                                       vshape, vshape, vshape],
                            args=(dy, z, g), sem=("arbitrary",), comm=comm)
    return outs if comm is None else (outs, delivered)


def loss_ln_bwd(name, z, g, b, target):
    T, D = z.shape
    br = _blk(T, 512)
    n = T // br

    def body(z_ref, g_ref, b_ref, t_ref, loss_ref, dz_ref, dzb_ref, dg_ref, db_ref, acc_ref):
        i = pl.program_id(0)
        z = z_ref[...]
        mu = jnp.mean(z, axis=-1, keepdims=True)
        zc = z - mu
        var = jnp.mean(zc * zc, axis=-1, keepdims=True)
        rstd = lax.rsqrt(var + LN_EPS)
        xhat = zc * rstd
        err = (xhat * g_ref[...] + b_ref[...]) - t_ref[...]
        dy = err * (1.0 / D)
        dxh = dy * g_ref[...]
        m1 = jnp.mean(dxh, axis=-1, keepdims=True)
        m2 = jnp.mean(dxh * xhat, axis=-1, keepdims=True)
        dz = rstd * (dxh - m1 - xhat * m2)
        dz_ref[...] = dz
        dzb_ref[...] = dz.astype(MXU_DTYPE)

        @pl.when(i == 0)
        def _():
            acc_ref[...] = jnp.zeros_like(acc_ref)
            dg_ref[...] = jnp.zeros_like(dg_ref)
            db_ref[...] = jnp.zeros_like(db_ref)

        acc_ref[...] += jnp.sum(err * err, axis=0, keepdims=True)
        dg_ref[...] += jnp.sum(dy * xhat, axis=0, keepdims=True)
        db_ref[...] += jnp.sum(dy, axis=0, keepdims=True)

        @pl.when(i == n - 1)
        def _():
            loss_ref[...] = (0.5 / D) * jnp.sum(acc_ref[...], axis=1, keepdims=True)

    row = pl.BlockSpec((br, D), lambda i: (i, 0))
    vec = pl.BlockSpec((1, D), lambda i: (0, 0))
    vshape = jax.ShapeDtypeStruct((1, D), F32)
    return pl.pallas_call(body, name=name, grid=(n,), in_specs=[row, vec, vec, row],
                          out_specs=[pl.BlockSpec((1, 1), lambda i: (0, 0)), row, row, vec, vec],
                          out_shape=[jax.ShapeDtypeStruct((1, 1), F32), jax.ShapeDtypeStruct((T, D), F32),
                                     jax.ShapeDtypeStruct((T, D), MXU_DTYPE), vshape, vshape],
                          scratch_shapes=[pltpu.VMEM((1, D), F32)],
                          compiler_params=_cp(("arbitrary",)))(z, g, b, target)


CONV_BT = 128
SUBLANES = 8


def _shifted(ref, cs, bt, offsets, stage):
    for r in range(SUBLANES):
        offs = [off for off in offsets if off % SUBLANES == r]
        if offs:
            rows = bt + max(offs) - r
            stage[r, 0:rows, :] = ref[pl.ds(r, rows), cs]
    for k, off in enumerate(offsets):
        r = off % SUBLANES
        yield k, stage[r, off - r:off - r + bt, :]


def conv_fwd(name, h1, dw_w, dw_b, ln_g, ln_b, comm=None):
    T, D2 = h1.shape
    D = D2 // 2
    W = dw_w.shape[0]
    taps = W - 1
    bt = _blk(T, CONV_BT)
    hb = bt // HALO
    u_off = [HALO - (taps - 1) + k for k in range(taps)]

    def body(h_ref, hp_ref, w_ref, b_ref, g_ref, be_ref, c_ref, s_ref, ux, stage):
        i = pl.program_id(0)
        up = hp_ref[:, :D] * _sig(hp_ref[:, D:])
        ux[0:HALO, :] = jnp.where(i == 0, 0.0, up)
        ux[HALO:HALO + bt, :] = h_ref[:, :D] * _sig(h_ref[:, D:])
        for cb in range(D // LANE):
            cs = slice(cb * LANE, (cb + 1) * LANE)
            acc = jnp.broadcast_to(b_ref[:, cs], (bt, LANE))
            for k, u_k in _shifted(ux, cs, bt, u_off, stage.at[cb % 2]):
                acc = acc + w_ref[k:k + 1, cs] * u_k
            c_ref[:, cs] = acc
        n = _ln_stats(c_ref[...]) * g_ref[...] + be_ref[...]
        s_ref[...] = (n * _sig(n)).astype(MXU_DTYPE)

    main = pl.BlockSpec((bt, D2), lambda i: (i, 0))
    prev = pl.BlockSpec((HALO, D2), lambda i: (jnp.maximum(i * hb - 1, 0), 0))
    wspec = pl.BlockSpec((W, D), lambda i: (0, 0))
    vec = pl.BlockSpec((1, D), lambda i: (0, 0))
    row = pl.BlockSpec((bt, D), lambda i: (i, 0))
    return _call(body, name=name, grid=(T // bt,), in_specs=[main, prev, wspec, vec, vec, vec], out_specs=[row, row],
                 out_shape=[jax.ShapeDtypeStruct((T, D), F32), jax.ShapeDtypeStruct((T, D), MXU_DTYPE)],
                 scratch_shapes=[pltpu.VMEM((HALO + bt, D), F32), pltpu.VMEM((2, SUBLANES, HALO + bt, LANE), F32)],
                 args=(h1, h1, dw_w, dw_b, ln_g, ln_b), sem=("parallel",), comm=comm)


def conv_bwd_ln(name, ds, c, ln_g, ln_b, comm=None):
    T, D = c.shape
    br = _blk(T, 512)

    def body(ds_ref, c_ref, g_ref, be_ref, dc_ref, dg_ref, db_ref, dcs_ref):
        i = pl.program_id(0)
        c = c_ref[...]
        mu = jnp.mean(c, axis=-1, keepdims=True)
        cc = c - mu
        var = jnp.mean(cc * cc, axis=-1, keepdims=True)
        rstd = lax.rsqrt(var + LN_EPS)
        xhat = cc * rstd
        n = xhat * g_ref[...] + be_ref[...]
        sg = _sig(n)
        dn = ds_ref[...] * (sg * (1.0 + n * (1.0 - sg)))
        dxh = dn * g_ref[...]
        m1 = jnp.mean(dxh, axis=-1, keepdims=True)
        m2 = jnp.mean(dxh * xhat, axis=-1, keepdims=True)
        dc = rstd * (dxh - m1 - xhat * m2)
        dc_ref[...] = dc

        @pl.when(i == 0)
        def _():
            dg_ref[...] = jnp.zeros_like(dg_ref)
            db_ref[...] = jnp.zeros_like(db_ref)
            dcs_ref[...] = jnp.zeros_like(dcs_ref)

        dg_ref[...] += jnp.sum(dn * xhat, axis=0, keepdims=True)
        db_ref[...] += jnp.sum(dn, axis=0, keepdims=True)
        dcs_ref[...] += jnp.sum(dc, axis=0, keepdims=True)

    row = pl.BlockSpec((br, D), lambda i: (i, 0))
    vec = pl.BlockSpec((1, D), lambda i: (0, 0))
    vshape = jax.ShapeDtypeStruct((1, D), F32)
    outs, delivered = _call(body, name=name, grid=(T // br,), in_specs=[row, row, vec, vec],
                            out_specs=[row, vec, vec, vec],
                            out_shape=[jax.ShapeDtypeStruct((T, D), F32), vshape, vshape, vshape],
                            args=(ds, c, ln_g, ln_b), sem=("arbitrary",), comm=comm)
    return outs if comm is None else (outs, delivered)


def conv_bwd_dw(name, dc, h1, dw_w, comm=None):
    T, D2 = h1.shape
    D = D2 // 2
    W = dw_w.shape[0]
    taps = W - 1
    bt = _blk(T, CONV_BT)
    hb = bt // HALO
    n = T // bt

    def body(dc_ref, dcn_ref, h_ref, hp_ref, w_ref, dh_ref, dw_ref, dhs_ref, ux, dcx, du, stage_dc, stage_u):
        i = pl.program_id(0)
        a = h_ref[:, :D]
        sg = _sig(h_ref[:, D:])
        up = hp_ref[:, :D] * _sig(hp_ref[:, D:])
        ux[0:HALO, :] = jnp.where(i == 0, 0.0, up)
        ux[HALO:HALO + bt, :] = a * sg
        dcx[0:bt, :] = dc_ref[...]
        dcx[bt:bt + HALO, :] = jnp.where(i == n - 1, 0.0, dcn_ref[...])

        @pl.when(i == 0)
        def _():
            dw_ref[...] = jnp.zeros_like(dw_ref)
            dhs_ref[...] = jnp.zeros_like(dhs_ref)

        for cb in range(D // LANE):
            cs = slice(cb * LANE, (cb + 1) * LANE)
            dcb = dcx[0:bt, cs]
            acc = jnp.zeros((bt, LANE), F32)
            for k, dc_k in _shifted(dcx, cs, bt, [taps - 1 - k for k in range(taps)], stage_dc.at[cb % 2]):
                acc = acc + w_ref[k:k + 1, cs] * dc_k
            du[:, cs] = acc
            for k, u_k in _shifted(ux, cs, bt, [HALO - (taps - 1) + k for k in range(taps)], stage_u.at[cb % 2]):
                dw_ref[k:k + 1, cs] += jnp.sum(dcb * u_k, axis=0, keepdims=True)
        d_u = du[...]
        da = d_u * sg
        dg = d_u * a * sg * (1.0 - sg)
        dh_ref[:, :D] = da.astype(MXU_DTYPE)
        dh_ref[:, D:] = dg.astype(MXU_DTYPE)
        dhs_ref[:, :D] += jnp.sum(da, axis=0, keepdims=True)
        dhs_ref[:, D:] += jnp.sum(dg, axis=0, keepdims=True)

    row = pl.BlockSpec((bt, D), lambda i: (i, 0))
    nxt = pl.BlockSpec((HALO, D), lambda i: (jnp.minimum((i + 1) * hb, T // HALO - 1), 0))
    main = pl.BlockSpec((bt, D2), lambda i: (i, 0))
    prev = pl.BlockSpec((HALO, D2), lambda i: (jnp.maximum(i * hb - 1, 0), 0))
    wspec = pl.BlockSpec((W, D), lambda i: (0, 0))
    return _call(body, name=name, grid=(n,), in_specs=[row, nxt, main, prev, wspec],
                 out_specs=[main, wspec, pl.BlockSpec((1, D2), lambda i: (0, 0))],
                 out_shape=[jax.ShapeDtypeStruct((T, D2), MXU_DTYPE), jax.ShapeDtypeStruct((W, D), F32),
                            jax.ShapeDtypeStruct((1, D2), F32)],
                 scratch_shapes=[pltpu.VMEM((HALO + bt, D), F32), pltpu.VMEM((bt + HALO, D), F32),
                                 pltpu.VMEM((bt, D), F32), pltpu.VMEM((2, SUBLANES, HALO + bt, LANE), F32),
                                 pltpu.VMEM((2, SUBLANES, HALO + bt, LANE), F32)],
                 args=(dc, dc, h1, h1, dw_w), sem=("arbitrary",), comm=comm)


def _t5_bucket(dist, n_buckets):
    max_exact = n_buckets // 2
    large = max_exact + (np.log(np.maximum(dist, 1) / max_exact) / math.log(REL_MAX_DIST / max_exact)
                         * (n_buckets - max_exact)).astype(np.int32)
    large = np.minimum(large, n_buckets - 1)
    return np.where(dist < max_exact, dist, large).astype(np.int32)


def _band_tables(n_buckets):
    i = np.arange(BAND)[:, None]
    j = np.arange(2 * BAND)[None, :]
    delta = i - j + BAND
    out = []
    for window, dil in BRANCHES:
        ok = (delta >= 0) & (delta <= window // dil)
        out.append((_t5_bucket(np.clip(delta, 0, None) * dil, n_buckets), ok))
    return out


def _units():
    for bi in (2, 1, 0):
        d = BRANCHES[bi][1]
        for r in range(d):
            for nb in range(ATT_TB // (BAND * d)):
                yield bi, d, r, nb


def _rows(start, size, d):
    return pl.ds(start, size) if d == 1 else pl.ds(start, size, stride=d)


def _bc(v):
    return jnp.broadcast_to(v, (BAND, LANE))


def attn_fwd(name, q, kv, bias, comm=None):
    T, D = q.shape
    H = D // HEAD_DIM
    TB = ATT_TB
    scale = HEAD_DIM ** -0.5

    def body(q_ref, kc_ref, kp_ref, vc_ref, vp_ref, b_ref, o_ref, l_ref, kx, vx, m_sc, s_sc, a_sc):
        first = (pl.program_id(1) == 0).astype(jnp.int32)
        kx[0:TB, :] = kp_ref[...]
        kx[TB:2 * TB, :] = kc_ref[...]
        vx[0:TB, :] = vp_ref[...]
        vx[TB:2 * TB, :] = vc_ref[...]
        for bi, d, r, nb in _units():
            qs = _rows(nb * BAND * d + r, BAND, d)
            ks = _rows(TB + (nb - 1) * BAND * d + r, 2 * BAND, d)
            qb = q_ref[qs, :].astype(MXU_DTYPE)
            kb = kx[ks, :].astype(MXU_DTYPE)
            vb = vx[ks, :].astype(MXU_DTYPE)
            b = b_ref[bi + 3 * first] if nb == 0 else b_ref[bi]
            s = lax.dot_general(qb, kb, NT, preferred_element_type=F32) * scale + b
            mrow = jnp.max(s, axis=1, keepdims=True)
            p = jnp.exp(s - mrow)
            m_sc.at[bi][qs, :] = _bc(mrow)
            s_sc.at[bi][qs, :] = _bc(jnp.sum(p, axis=1, keepdims=True))
            a_sc.at[bi][qs, :] = lax.dot_general(p.astype(MXU_DTYPE), vb, NN, preferred_element_type=F32)
        for c0 in range(0, TB, ATT_MERGE_ROWS):
            rows = slice(c0, c0 + ATT_MERGE_ROWS)
            ms = [m_sc[bi, rows, :] for bi in range(3)]
            m = jnp.maximum(jnp.maximum(ms[0], ms[1]), ms[2])
            ws = [jnp.exp(mi - m) for mi in ms]
            den = ws[0] * s_sc[0, rows, :] + ws[1] * s_sc[1, rows, :] + ws[2] * s_sc[2, rows, :]
            num = ws[0] * a_sc[0, rows, :] + ws[1] * a_sc[1, rows, :] + ws[2] * a_sc[2, rows, :]
            o_ref[rows, :] = num / den
            l_ref[rows, :] = m + jnp.log(den)

    cur = lambda off: pl.BlockSpec((TB, HEAD_DIM), lambda h, i: (i, off + h))
    prv = lambda off: pl.BlockSpec((TB, HEAD_DIM), lambda h, i: (jnp.maximum(i - 1, 0), off + h))
    bspec = pl.BlockSpec((6, None, BAND, 2 * BAND), lambda h, i: (0, h, 0, 0))
    sc = lambda *shape: pltpu.VMEM(shape + (HEAD_DIM,), F32)
    return _call(body, name=name, grid=(H, T // TB), in_specs=[cur(0), cur(0), prv(0), cur(H), prv(H), bspec],
                 out_specs=[cur(0), cur(0)],
                 out_shape=[jax.ShapeDtypeStruct((T, D), F32), jax.ShapeDtypeStruct((T, D), F32)],
                 scratch_shapes=[sc(2 * TB), sc(2 * TB), sc(3, TB), sc(3, TB), sc(3, TB)],
                 args=(q, kv, kv, kv, kv, bias), sem=("parallel", "parallel"), comm=comm)


def attn_bwd(name, q, kv, bias, o, lse, do, comm=None):
    T, D = q.shape
    H = D // HEAD_DIM
    TB = ATT_TB
    nI = T // TB
    scale = HEAD_DIM ** -0.5

    def body(q_ref, kc_ref, kp_ref, vc_ref, vp_ref, b_ref, o_ref, l_ref, do_ref,
             dq_ref, dk_ref, dv_ref, db_ref, kx, vx, dkx, dvx, ck, cv, dl_sc, dq_sc):
        step = pl.program_id(1)
        first = (step == nI - 1).astype(jnp.int32)
        kx[0:TB, :] = kp_ref[...]
        kx[TB:2 * TB, :] = kc_ref[...]
        vx[0:TB, :] = vp_ref[...]
        vx[TB:2 * TB, :] = vc_ref[...]
        dl_sc[...] = jnp.broadcast_to(jnp.sum(do_ref[...] * o_ref[...], axis=1, keepdims=True), (TB, LANE))

        @pl.when(step == 0)
        def _():
            db_ref[...] = jnp.zeros_like(db_ref)
            ck[...] = jnp.zeros_like(ck)
            cv[...] = jnp.zeros_like(cv)

        held_k = held_v = None
        for bi, d, r, nb in _units():
            band = lambda b: _rows(TB + b * BAND * d + r, BAND, d)
            qs = _rows(nb * BAND * d + r, BAND, d)
            ks = _rows(TB + (nb - 1) * BAND * d + r, 2 * BAND, d)
            qb = q_ref[qs, :].astype(MXU_DTYPE)
            kb = kx[ks, :].astype(MXU_DTYPE)
            vb = vx[ks, :].astype(MXU_DTYPE)
            dob = do_ref[qs, :].astype(MXU_DTYPE)
            b = b_ref[bi + 3 * first] if nb == 0 else b_ref[bi]
            s = lax.dot_general(qb, kb, NT, preferred_element_type=F32) * scale + b
            p = jnp.exp(s - l_ref[qs, :][:, :1])
            dp = lax.dot_general(dob, vb, NT, preferred_element_type=F32)
            dsv = p * (dp - dl_sc[qs, :][:, :1])
            db_ref[bi] += dsv
            dsb = dsv.astype(MXU_DTYPE)
            dv_blk = lax.dot_general(p.astype(MXU_DTYPE), dob, TN, preferred_element_type=F32)
            dk_blk = lax.dot_general(dsb, qb, TN, preferred_element_type=F32) * scale
            dq_sc.at[bi][qs, :] = lax.dot_general(dsb, kb, NN, preferred_element_type=F32) * scale
            dkb, dvb = dkx.at[bi], dvx.at[bi]
            if nb == 0:
                dkb[band(-1), :] = dk_blk[:BAND]
                dvb[band(-1), :] = dv_blk[:BAND]
            else:
                dkb[band(nb - 1), :] = held_k + dk_blk[:BAND]
                dvb[band(nb - 1), :] = held_v + dv_blk[:BAND]
            held_k, held_v = dk_blk[BAND:], dv_blk[BAND:]
            if nb == TB // (BAND * d) - 1:
                dkb[band(nb), :] = held_k
                dvb[band(nb), :] = held_v

        dq_ref[...] = (dq_sc[0] + dq_sc[1] + dq_sc[2]).astype(dq_ref.dtype)
        for dx_ref, dxx, cx in ((dk_ref, dkx, ck), (dv_ref, dvx, cv)):
            dx_ref[...] = (dxx[0, TB:2 * TB, :] + dxx[1, TB:2 * TB, :] + dxx[2, TB:2 * TB, :] + cx[...]).astype(dx_ref.dtype)
            cx[...] = dxx[2, 0:TB, :]
            for bi in (1, 0):
                lo = TB - BAND * BRANCHES[bi][1]
                cx[lo:TB, :] += dxx[bi, lo:TB, :]

    blk = lambda h, i: nI - 1 - i
    cur = lambda off: pl.BlockSpec((TB, HEAD_DIM), lambda h, i: (blk(h, i), off + h))
    prv = lambda off: pl.BlockSpec((TB, HEAD_DIM), lambda h, i: (jnp.maximum(blk(h, i) - 1, 0), off + h))
    bspec = pl.BlockSpec((3, None, BAND, 2 * BAND), lambda h, i: (0, h, 0, 0))
    bspec_in = pl.BlockSpec((6, None, BAND, 2 * BAND), lambda h, i: (0, h, 0, 0))
    sc = lambda *shape: pltpu.VMEM(shape + (HEAD_DIM,), F32)
    return _call(body, name=name, grid=(H, nI),
                 in_specs=[cur(0), cur(0), prv(0), cur(H), prv(H), bspec_in, cur(0), cur(0), cur(0)],
                 out_specs=[cur(0), cur(0), cur(0), bspec],
                 out_shape=[jax.ShapeDtypeStruct((T, D), MXU_DTYPE), jax.ShapeDtypeStruct((T, D), MXU_DTYPE),
                            jax.ShapeDtypeStruct((T, D), MXU_DTYPE), jax.ShapeDtypeStruct((3, H, BAND, 2 * BAND), F32)],
                 scratch_shapes=[sc(2 * TB), sc(2 * TB), sc(3, 2 * TB), sc(3, 2 * TB), sc(TB), sc(TB), sc(TB), sc(3, TB)],
                 args=(q, kv, kv, kv, kv, bias, o, lse, do), sem=("arbitrary", "arbitrary"), comm=comm)


def adamw(name, parts, w, m, v):
    L, R, C = w.shape
    P = parts[0].shape[0]
    br = R if R % 8 else _blk(R, max(8, (1 << 18) // C))
    c1 = 1.0 / (1.0 - ADAM_B1 ** ADAM_STEP)
    c2 = 1.0 / (1.0 - ADAM_B2 ** ADAM_STEP)

    def body(*refs):
        p_refs = refs[:L]
        w_ref, m_ref, v_ref, g_ref, d_ref, nm_ref, nv_ref = refs[L:]
        lay = pl.program_id(0)

        def total(p_ref):
            g = p_ref[0].astype(F32)
            for k in range(1, P):
                g = g + p_ref[k].astype(F32)
            return g

        g = total(p_refs[0])
        for j in range(1, L):
            g = jnp.where(lay == j, total(p_refs[j]), g)
        nm = ADAM_B1 * m_ref[...] + (1.0 - ADAM_B1) * g
        nv = ADAM_B2 * v_ref[...] + (1.0 - ADAM_B2) * (g * g)
        g_ref[...] = g
        nm_ref[...] = nm
        nv_ref[...] = nv
        d_ref[...] = -ADAM_LR * ((nm * c1) / (jnp.sqrt(nv * c2) + ADAM_EPS) + ADAM_WD * w_ref[...])

    row = pl.BlockSpec((None, br, C), lambda l, i: (l, i, 0))
    pspecs = [pl.BlockSpec((P, br, C), lambda l, i, j=j: (0, jnp.where(l == j, i, 0), 0)) for j in range(L)]
    shp = jax.ShapeDtypeStruct((L, R, C), F32)
    return pl.pallas_call(body, name=name, grid=(L, R // br), in_specs=[*pspecs, row, row, row],
                          out_specs=[row] * 4, out_shape=[shp] * 4,
                          compiler_params=_cp(("parallel", "parallel")))(*parts, w, m, v)


def pair_sum(name, g, s1, c_idx):
    _, _, R, C = g.shape
    br = _blk(R, max(16, (1 << 19) // C))

    def body(c_ref, g_ref, s_ref, t_ref):
        t_ref[...] = (g_ref[...].astype(F32) + s_ref[...].astype(F32)).astype(t_ref.dtype)

    return pl.pallas_call(
        body, name=name,
        grid_spec=pltpu.PrefetchScalarGridSpec(
            num_scalar_prefetch=1, grid=(4, R // br),
            in_specs=[pl.BlockSpec((None, None, br, C), lambda j, i, c: (j, c[0], i, 0)),
                      pl.BlockSpec((None, br, C), lambda j, i, c: (j, i, 0))],
            out_specs=pl.BlockSpec((None, br, C), lambda j, i, c: (j, i, 0))),
        out_shape=jax.ShapeDtypeStruct((4, R, C), g.dtype), compiler_params=_cp(("parallel", "parallel")))(c_idx, g, s1)


def _me():
    return lax.axis_index("x"), lax.axis_index("y"), lax.axis_index("c")


ANY = pl.BlockSpec(memory_space=pl.ANY)


class Gather:
    relay_at = 0.7

    def __init__(self, shards):
        n = len(shards)
        self.arrays = list(shards)
        self.out_shapes = [jax.ShapeDtypeStruct((N_DEV,) + s.shape, s.dtype) for s in shards]
        self.sems = [pltpu.SemaphoreType.DMA((n, 7)), pltpu.SemaphoreType.DMA((n, 7)), pltpu.SemaphoreType.DMA((n,))]

    def _ctx(self, ins, outs, sems):
        send_sems, recv_sems, local_sems = sems
        x, y, c = _me()
        chips = [(1 - x, y), (x, 1 - y), (1 - x, 1 - y)]

        def copy(a, k, block, to, src=None):
            px, py, pc = block
            dst = outs[a].at[4 * px + 2 * py + pc]
            return pltpu.make_async_remote_copy(src_ref=dst if src is None else src, dst_ref=dst,
                                                send_sem=send_sems.at[a, k], recv_sem=recv_sems.at[a, k],
                                                device_id=to, device_id_type=MESH)

        def own(a):
            return pltpu.make_async_copy(ins[a], outs[a].at[4 * x + 2 * y + c], local_sems.at[a])

        def first(a):
            return [copy(a, 0, (x, y, c), (x, y, 1 - c), src=ins[a])] + \
                   [copy(a, 1 + j, (x, y, c), (*chip, c), src=ins[a]) for j, chip in enumerate(chips)]

        def passed(a):
            return [copy(a, 4 + j, (*chip, c), (x, y, 1 - c)) for j, chip in enumerate(chips)]

        return (x, y, c), chips, copy, own, first, passed

    def start(self, ins, outs, sems):
        _, _, _, own, first, _ = self._ctx(ins, outs, sems)
        for a in range(len(ins)):
            own(a).start()
            for cp in first(a):
                cp.start()

    def relay(self, ins, outs, sems):
        me, chips, copy, _, _, passed = self._ctx(ins, outs, sems)
        for a in range(len(ins)):
            fwd = passed(a)
            for j, chip in enumerate(chips):
                copy(a, 1 + j, (*chip, me[2]), me).wait_recv()
                fwd[j].start()

    def wait(self, ins, outs, sems):
        (x, y, c), chips, copy, own, first, passed = self._ctx(ins, outs, sems)
        for a in range(len(ins)):
            copy(a, 0, (x, y, 1 - c), (x, y, c)).wait_recv()
            for j, chip in enumerate(chips):
                copy(a, 4 + j, (*chip, 1 - c), (x, y, c)).wait_recv()
            for cp in first(a) + passed(a):
                cp.wait_send()
            own(a).wait()


def all_gather(name, shards):
    comm = Gather(shards)
    n = len(shards)

    def body(*refs):
        comm.start(refs[:n], refs[n:2 * n], refs[2 * n:])
        comm.relay(refs[:n], refs[n:2 * n], refs[2 * n:])
        comm.wait(refs[:n], refs[n:2 * n], refs[2 * n:])

    return pl.pallas_call(body, name=name, in_specs=[ANY] * n, out_specs=[ANY] * n, out_shape=comm.out_shapes,
                          scratch_shapes=comm.sems)(*shards)


class PairExchange:
    relay_at = None

    def __init__(self, grads):
        n = len(grads)
        self.arrays = list(grads)
        self.out_shapes = [jax.ShapeDtypeStruct((4,) + g.shape[2:], g.dtype) for g in grads]
        self.sems = [pltpu.SemaphoreType.DMA((n, 4)), pltpu.SemaphoreType.DMA((n, 4))]

    def _copies(self, ins, outs, sems):
        send_sems, recv_sems = sems
        x, y, c = _me()
        return [pltpu.make_async_remote_copy(
            src_ref=ins[a].at[j, 1 - c], dst_ref=outs[a].at[j], send_sem=send_sems.at[a, j],
            recv_sem=recv_sems.at[a, j], device_id=(x, y, 1 - c), device_id_type=MESH)
            for a in range(len(ins)) for j in range(4)]

    def start(self, ins, outs, sems):
        for cp in self._copies(ins, outs, sems):
            cp.start()

    def wait(self, ins, outs, sems):
        for cp in self._copies(ins, outs, sems):
            cp.wait()


def pair_exchange(name, grads):
    comm = PairExchange(grads)
    n = len(grads)

    def body(*refs):
        comm.start(refs[:n], refs[n:2 * n], refs[2 * n:])
        comm.wait(refs[:n], refs[n:2 * n], refs[2 * n:])

    return pl.pallas_call(body, name=name, in_specs=[ANY] * n, out_specs=[ANY] * n, out_shape=comm.out_shapes,
                          scratch_shapes=comm.sems)(*grads)


class ChipExchange:
    relay_at = None

    def __init__(self, sums):
        n = len(sums)
        self.arrays = list(sums)
        self.out_shapes = [jax.ShapeDtypeStruct(s.shape, s.dtype) for s in sums]
        self.sems = [pltpu.SemaphoreType.DMA((n, 3)), pltpu.SemaphoreType.DMA((n, 3)), pltpu.SemaphoreType.DMA((n,))]

    def _copies(self, ins, outs, sems, arrivals):
        send_sems, recv_sems, local_sems = sems
        x, y, c = _me()
        my = 2 * x + y
        chips = [(1 - x, y), (x, 1 - y), (1 - x, 1 - y)]
        mine, sends, recvs = [], [], []
        for a in range(len(ins)):
            mine.append(pltpu.make_async_copy(ins[a].at[my], outs[a].at[my], local_sems.at[a]))
            for j, (px, py) in enumerate(chips):
                sends.append(pltpu.make_async_remote_copy(
                    src_ref=ins[a].at[2 * px + py], dst_ref=outs[a].at[my],
                    send_sem=send_sems.at[a, j], recv_sem=recv_sems.at[a, j],
                    device_id=(px, py, c), device_id_type=MESH))
                if arrivals:
                    recvs.append(pltpu.make_async_remote_copy(
                        src_ref=ins[a].at[my], dst_ref=outs[a].at[2 * px + py],
                        send_sem=send_sems.at[a, j], recv_sem=recv_sems.at[a, j],
                        device_id=(px, py, c), device_id_type=MESH))
        return mine, sends, recvs

    def start(self, ins, outs, sems):
        mine, sends, _ = self._copies(ins, outs, sems, arrivals=False)
        for cp in mine + sends:
            cp.start()

    def wait(self, ins, outs, sems):
        mine, sends, recvs = self._copies(ins, outs, sems, arrivals=True)
        for cp in recvs:
            cp.wait_recv()
        for cp in sends:
            cp.wait_send()
        for cp in mine:
            cp.wait()


def all_reduce_small(name, pack):
    R, C = pack.shape

    def body(p_ref, o_ref, buf, send_sems, recv_sems):
        x, y, c = _me()
        me = 4 * x + 2 * y + c
        buf[me] = p_ref[...]
        copies = []
        for k in range(1, N_DEV):
            px, py, pc = x ^ (k >> 2), y ^ ((k >> 1) & 1), c ^ (k & 1)
            copies.append(pltpu.make_async_remote_copy(
                src_ref=p_ref, dst_ref=buf.at[me], send_sem=send_sems.at[k - 1], recv_sem=recv_sems.at[k - 1],
                device_id=(px, py, pc), device_id_type=MESH))
        for cp in copies:
            cp.start()
        for cp in copies:
            cp.wait()
        acc = buf[0]
        for d in range(1, N_DEV):
            acc = acc + buf[d]
        o_ref[...] = acc

    vm = pl.BlockSpec(memory_space=pltpu.VMEM)
    return pl.pallas_call(
        body, name=name, in_specs=[vm], out_specs=vm, out_shape=jax.ShapeDtypeStruct((R, C), F32),
        scratch_shapes=[pltpu.VMEM((N_DEV, R, C), F32), pltpu.SemaphoreType.DMA((N_DEV - 1,)),
                        pltpu.SemaphoreType.DMA((N_DEV - 1,))],
    )(pack)


def _ep_bias(acc, ex, outs):
    outs[0][...] = acc + ex[0][...]


def _ep_store(acc, ex, outs):
    outs[0][...] = acc.astype(outs[0].dtype)


def _ep_resid(alpha, bias):
    def ep(acc, ex, outs):
        if bias:
            outs[0][...] = alpha * ex[1][...] + (acc + ex[0][...])
        else:
            outs[0][...] = alpha * ex[0][...] + acc
    return ep


def _ep_relu2(acc, ex, outs):
    r = jnp.maximum(acc, 0.0)
    outs[0][...] = (r * r).astype(outs[0].dtype)
    outs[1][...] = r


def _ep_relu2_bwd(acc, ex, outs):
    outs[0][...] = (acc * (2.0 * ex[0][...])).astype(outs[0].dtype)


def _ep_add(acc, ex, outs):
    outs[0][...] = ex[0][...] + acc


def kernel(x, conv_pw1_w, conv_pw1_b, conv_dw_w, conv_dw_b, conv_ln_g, conv_ln_b, conv_pw2_w, conv_pw2_b, w_kv, attn_wq, attn_wo, rel_bias, mlp_w1, mlp_w2, ln_mix_g, ln_mix_b, ln_mlp_g, ln_mlp_b, loss_target, m_conv_pw1_w, m_conv_pw1_b, m_conv_dw_w, m_conv_dw_b, m_conv_ln_g, m_conv_ln_b, m_conv_pw2_w, m_conv_pw2_b, m_w_kv, m_attn_wq, m_attn_wo, m_rel_bias, m_mlp_w1, m_mlp_w2, m_ln_mix_g, m_ln_mix_b, m_ln_mlp_g, m_ln_mlp_b, v_conv_pw1_w, v_conv_pw1_b, v_conv_dw_w, v_conv_dw_b, v_conv_ln_g, v_conv_ln_b, v_conv_pw2_w, v_conv_pw2_b, v_w_kv, v_attn_wq, v_attn_wo, v_rel_bias, v_mlp_w1, v_mlp_w2, v_ln_mix_g, v_ln_mix_b, v_ln_mlp_g, v_ln_mlp_b):
    T, D = x.shape[1], x.shape[2]
    H = D // HEAD_DIM
    depth = mlp_w1.shape[0]
    assert depth == 2 and T % ATT_TB == 0
    alpha = (2 * depth) ** 0.25
    ds_ = D // N_DEV
    xi, yi, ci = _me()
    me = 4 * xi + 2 * yi + ci
    c_idx = ci.astype(jnp.int32).reshape(1)
    x2 = x.reshape(T, D)
    target = loss_target.reshape(T, D)
    n_buckets = rel_bias.shape[0]
    taps = conv_dw_w.shape[1]

    local = {
        "pw1": (conv_pw1_w, 0), "pw2": (conv_pw2_w, 0), "wkv": (w_kv, None), "wq": (attn_wq, 0), "wo": (attn_wo, 0),
        "w1_0": (mlp_w1, 0), "w1_1": (mlp_w1, 1), "w2_0": (mlp_w2, 0), "w2_1": (mlp_w2, 1),
    }
    names = list(local)
    small = jnp.concatenate([conv_dw_w[0], conv_dw_b, conv_ln_g, conv_ln_b, conv_pw2_b,
                             conv_pw1_b.reshape(2, ds_)], axis=0)
    small = jnp.pad(small, ((0, (-small.shape[0]) % 8), (0, 0)))
    shard = {k: cast_bf16("cast_" + k, *local[k]) for k in names}
    G = {}

    def gather_start(*keys):
        return Gather([shard[k] for k in keys])

    def gather_done(keys, delivered):
        G.update(zip(keys, delivered))

    g_pw1, g_small = all_gather("gather_pw1", [shard["pw1"], small])
    G["pw1"] = g_pw1
    sm = jnp.transpose(g_small, (1, 0, 2)).reshape(small.shape[0], D)
    dw_w = jnp.pad(sm[:taps], ((0, HALO - taps), (0, 0)))
    dw_b, cln_g, cln_b, pw2_b = (sm[taps + k:taps + k + 1] for k in range(4))
    pw1_b = g_small[:, taps + 4:taps + 6, :].reshape(1, 2 * D)

    row = lambda a, l: a[l:l + 1]

    x_b = cast_bf16("cast_x", x2)
    (h1,) = mm_fwd("pw1", x_b, G["pw1"], colsharded=True, epilogue=_ep_bias, outs=[F32], rowvec=[pw1_b])
    (c_pre, s_b), delivered = conv_fwd("conv_fwd", h1, dw_w, dw_b, cln_g, cln_b, comm=gather_start("pw2", "w1_0"))
    gather_done(("pw2", "w1_0"), delivered)
    (z1,) = mm_fwd("pw2", s_b, G["pw2"], colsharded=False, epilogue=_ep_resid(alpha, True), outs=[F32],
                   rowvec=[pw2_b], tiles=[x2])
    x1, x1_b = ln_fwd("ln_mix0", z1, row(ln_mix_g, 0), row(ln_mix_b, 0))

    def mlp_fwd(l, xin, xin_b, up_keys=(), down_keys=()):
        res_ = mm_fwd(f"mlp_up{l}", xin_b, G[f"w1_{l}"], colsharded=True, epilogue=_ep_relu2, outs=[MXU_DTYPE, F32],
                      comm=gather_start(*up_keys) if up_keys else None)
        (act_b, r), delivered = res_ if up_keys else (res_, [])
        gather_done(up_keys, delivered)
        res_ = mm_fwd(f"mlp_down{l}", act_b, G[f"w2_{l}"], colsharded=False, epilogue=_ep_resid(alpha, False),
                      outs=[F32], tiles=[xin], comm=gather_start(*down_keys) if down_keys else None)
        ((z,), delivered) = res_ if down_keys else (res_, [])
        gather_done(down_keys, delivered)
        if l == depth - 1:
            return act_b, r, z, None, None
        y, y_b = ln_fwd(f"ln_mlp{l}", z, row(ln_mlp_g, l), row(ln_mlp_b, l))
        return act_b, r, z, y, y_b

    act0_b, r0, z2, x2_, x2_b = mlp_fwd(0, x1, x1_b, up_keys=("w2_0",), down_keys=("wkv", "wq", "wo"))

    (kv,) = mm_fwd("kv_proj", x2_b, G["wkv"], colsharded=True, epilogue=_ep_store, outs=[F32])
    (q,) = mm_fwd("q_proj", x2_b, G["wq"], colsharded=False, epilogue=_ep_store, outs=[F32])
    tables = _band_tables(n_buckets)
    onehot = jnp.concatenate([(jnp.arange(n_buckets)[:, None] == jnp.asarray(bucket).reshape(1, -1)).astype(F32)
                              for bucket, _ in tables], axis=1)
    KB = onehot.shape[1] // 8
    (btab,) = _mm("rel_bias_table", rel_bias, onehot, grid=(1, 8, 1),
                  a_spec=pl.BlockSpec((n_buckets, H), lambda i, j, k: (0, 0)),
                  b_spec=pl.BlockSpec((n_buckets, KB), lambda i, j, k: (0, j)), dims=TN, k_axis=2,
                  epilogue=_ep_store, out_shapes=[jax.ShapeDtypeStruct((H, 8 * KB), F32)],
                  out_specs=[pl.BlockSpec((H, KB), lambda i, j, k: (0, j))], acc_shape=(H, KB), exact=True)
    band_ok = jnp.asarray(np.stack([ok for _, ok in tables]))[:, None]
    bias = jnp.where(band_ok, jnp.transpose(btab.reshape(H, 3, BAND, 2 * BAND), (1, 0, 2, 3)), NEG)
    has_prev = jnp.asarray(np.arange(2 * BAND) >= BAND)
    bias = jnp.concatenate([bias, jnp.where(has_prev, bias, NEG)], axis=0)
    (o, lse), delivered = attn_fwd("attn_fwd", q, kv, bias, comm=gather_start("w1_1", "w2_1"))
    gather_done(("w1_1", "w2_1"), delivered)
    (z3,) = mm_fwd("o_proj", o, G["wo"], colsharded=False, epilogue=_ep_resid(alpha, False), outs=[F32], tiles=[x2_])
    x3, x3_b = ln_fwd("ln_mix1", z3, row(ln_mix_g, 1), row(ln_mix_b, 1))
    act1_b, r1, z4, _, _ = mlp_fwd(1, x3, x3_b)

    loss_local, *last_norm_bwd = loss_ln_bwd("loss_ln_mlp1_bwd", z4, row(ln_mlp_g, 1), row(ln_mlp_b, 1), target)
    loss = lax.psum(loss_local[0, 0], MESH_AXES)

    parts = {}

    def pair_of(grads):
        g4 = [g.reshape((4, 2) + g.shape[1:]) for g in grads.values()]
        return g4, PairExchange(g4)

    def chip_of(grads, g4, from_sibling):
        return ChipExchange([pair_sum("grad_pair_sum_" + k, g, s, c_idx) for k, g, s in zip(grads, g4, from_sibling)])

    def reduce_done(grads, delivered):
        parts.update(zip(grads, delivered))

    def mlp_bwd(l, dy_out, z, xin_b, act_b, r, pending=None, norm_bwd=None):
        if norm_bwd is not None:
            dz, dz_b, dg, db = norm_bwd
            comm = None
        elif pending is None:
            dz, dz_b, dg, db, _ = ln_bwd(f"ln_mlp{l}_bwd", dy_out, z, row(ln_mlp_g, l))
            comm = None
        else:
            (dz, dz_b, dg, db, _), from_sibling = ln_bwd(f"ln_mlp{l}_bwd", dy_out, z, row(ln_mlp_g, l), comm=pending[2])
            comm = chip_of(pending[0], pending[1], from_sibling)
        res_ = mm_dx(f"mlp_down{l}_dx", dz_b, G[f"w2_{l}"], colsharded=False, epilogue=_ep_relu2_bwd,
                     outs=[MXU_DTYPE], tiles=[r], comm=comm)
        ((dhm_b,), delivered) = res_ if comm is not None else (res_, [])
        dw2 = mm_dw(f"mlp_down{l}_dw", act_b, dz_b, colsharded=False)
        (dxin,) = mm_dx(f"mlp_up{l}_dx", dhm_b, G[f"w1_{l}"], colsharded=True, epilogue=_ep_resid(alpha, False),
                        outs=[F32], tiles=[dz])
        dw1 = mm_dw(f"mlp_up{l}_dw", xin_b, dhm_b, colsharded=True)
        return dxin, dw1, dw2, dg, db, delivered

    dx3, dw1_1, dw2_1, dg_mlp1, db_mlp1, _ = mlp_bwd(1, None, z4, x3_b, act1_b, r1, norm_bwd=last_norm_bwd)
    grads1 = {"w1_1": dw1_1, "w2_1": dw2_1}
    g4_1, pair1 = pair_of(grads1)

    (dz3, dz3_b, dg_mix1, db_mix1, _), from_sibling = ln_bwd("ln_mix1_bwd", dx3, z3, row(ln_mix_g, 1), comm=pair1)
    comm1 = chip_of(grads1, g4_1, from_sibling)
    (do,) = mm_dx("o_proj_dx", dz3_b, G["wo"], colsharded=False, epilogue=_ep_store, outs=[F32])
    dwo = mm_dw("o_proj_dw", o, dz3_b, colsharded=False)
    (dq, dk, dv, dbias), delivered = attn_bwd("attn_bwd", q, kv, bias, o, lse, do, comm=comm1)
    reduce_done(grads1, delivered)
    dkv = jnp.concatenate([dk, dv], axis=1)
    (dx2a,) = mm_dx("q_proj_dx", dq, G["wq"], colsharded=False, epilogue=_ep_resid(alpha, False), outs=[F32], tiles=[dz3])
    (dx2,) = mm_dx("kv_proj_dx", dkv, G["wkv"], colsharded=True, epilogue=_ep_add, outs=[F32], tiles=[dx2a])
    dwq = mm_dw("q_proj_dw", x2_b, dq, colsharded=False)
    dwkv = mm_dw("kv_proj_dw", x2_b, dkv, colsharded=True)
    grads2 = {"wo": dwo, "wq": dwq, "wkv": dwkv}
    g4_2, pair2 = pair_of(grads2)

    dbias2 = jnp.transpose(dbias, (1, 0, 2, 3)).reshape(H, -1)
    (drel_t,) = _mm("rel_bias_grad", dbias2, onehot, grid=(1, 1, 8),
                    a_spec=pl.BlockSpec((H, KB), lambda i, j, k: (0, k)),
                    b_spec=pl.BlockSpec((n_buckets, KB), lambda i, j, k: (0, k)), dims=NT, k_axis=2,
                    epilogue=_ep_store, out_shapes=[jax.ShapeDtypeStruct((H, n_buckets), F32)],
                    out_specs=[pl.BlockSpec((H, n_buckets), lambda i, j, k: (0, 0))], acc_shape=(H, n_buckets), exact=True)

    dx1, dw1_0, dw2_0, dg_mlp0, db_mlp0, delivered = mlp_bwd(0, dx2, z2, x1_b, act0_b, r0, pending=(grads2, g4_2, pair2))
    reduce_done(grads2, delivered)

    dz1, dz1_b, dg_mix0, db_mix0, dpw2_b = ln_bwd("ln_mix0_bwd", dx1, z1, row(ln_mix_g, 0))
    (ds,) = mm_dx("pw2_dx", dz1_b, G["pw2"], colsharded=False, epilogue=_ep_store, outs=[F32])
    dwpw2 = mm_dw("pw2_dw", s_b, dz1_b, colsharded=False)
    grads3 = {"w1_0": dw1_0, "w2_0": dw2_0, "pw2": dwpw2}
    g4_3, pair3 = pair_of(grads3)
    (dc, dcln_g, dcln_b, ddw_b), from_sibling = conv_bwd_ln("conv_bwd_ln", ds, c_pre, cln_g, cln_b, comm=pair3)
    (dh1_b, ddw_w, dpw1_b), delivered = conv_bwd_dw("conv_bwd_dw", dc, h1, dw_w,
                                                   comm=chip_of(grads3, g4_3, from_sibling))
    reduce_done(grads3, delivered)
    grads4 = {"pw1": mm_dw("pw1_dw", x_b, dh1_b, colsharded=True)}
    g4_4, _ = pair_of(grads4)
    comm4 = chip_of(grads4, g4_4, pair_exchange("grad_pair_exchange_pw1", g4_4))
    (dx,), delivered = mm_dx("pw1_dx", dh1_b, G["pw1"], colsharded=True, epilogue=_ep_resid(alpha, False), outs=[F32],
                             tiles=[dz1], comm=comm4)
    reduce_done(grads4, delivered)
    grad_x = dx.reshape(1, T, D)

    vec_rows = [dg_mix0, dg_mix1, db_mix0, db_mix1, dg_mlp0, dg_mlp1, db_mlp0, db_mlp1,
                ddw_b, dcln_g, dcln_b, dpw2_b, dpw1_b.reshape(2, D), ddw_w[:taps],
                jnp.pad(jnp.transpose(drel_t).reshape(1, -1), ((0, 0), (0, D - H * n_buckets)))]
    pack = jnp.concatenate(vec_rows, axis=0)
    pack = jnp.pad(pack, ((0, (-pack.shape[0]) % 8), (0, 0)))
    tot = all_reduce_small("grad_small_all_reduce", pack)

    def mine(rows):
        return lax.dynamic_slice_in_dim(rows, me * ds_, ds_, axis=1)

    g_ln_mix_g, g_ln_mix_b, g_ln_mlp_g, g_ln_mlp_b = tot[0:2], tot[2:4], tot[4:6], tot[6:8]
    g_dw_b, g_cln_g, g_cln_b, g_pw2_b = (mine(tot[8 + k:9 + k]) for k in range(4))
    g_pw1_b = lax.dynamic_slice_in_dim(tot[12:14].reshape(1, 2 * D), me * 2 * ds_, 2 * ds_, axis=1)
    g_dw_w = mine(tot[14:14 + taps])
    g_rel = tot[14 + taps, :H * n_buckets].reshape(n_buckets, H)

    res = {}

    def upd(nm, parts_, w, m, v):
        shp = w.shape
        if not isinstance(parts_, list):
            parts_ = [parts_]
        parts_ = [p[None] if p.ndim == 2 else p for p in parts_]
        w3, m3, v3 = (a.reshape((len(parts_),) + parts_[0].shape[1:]) for a in (w, m, v))
        outs = adamw("adamw_" + nm, parts_, w3, m3, v3)
        res[nm] = tuple(o_.reshape(shp) for o_ in outs)

    upd("conv_pw1_w", parts["pw1"], conv_pw1_w, m_conv_pw1_w, v_conv_pw1_w)
    upd("conv_pw1_b", g_pw1_b, conv_pw1_b, m_conv_pw1_b, v_conv_pw1_b)
    upd("conv_dw_w", g_dw_w, conv_dw_w, m_conv_dw_w, v_conv_dw_w)
    upd("conv_dw_b", g_dw_b, conv_dw_b, m_conv_dw_b, v_conv_dw_b)
    upd("conv_ln_g", g_cln_g, conv_ln_g, m_conv_ln_g, v_conv_ln_g)
    upd("conv_ln_b", g_cln_b, conv_ln_b, m_conv_ln_b, v_conv_ln_b)
    upd("conv_pw2_w", parts["pw2"], conv_pw2_w, m_conv_pw2_w, v_conv_pw2_w)
    upd("conv_pw2_b", g_pw2_b, conv_pw2_b, m_conv_pw2_b, v_conv_pw2_b)
    upd("w_kv", parts["wkv"], w_kv, m_w_kv, v_w_kv)
    upd("attn_wq", parts["wq"], attn_wq, m_attn_wq, v_attn_wq)
    upd("attn_wo", parts["wo"], attn_wo, m_attn_wo, v_attn_wo)
    upd("rel_bias", g_rel, rel_bias, m_rel_bias, v_rel_bias)
    upd("mlp_w1", [parts["w1_0"], parts["w1_1"]], mlp_w1, m_mlp_w1, v_mlp_w1)
    upd("mlp_w2", [parts["w2_0"], parts["w2_1"]], mlp_w2, m_mlp_w2, v_mlp_w2)
    upd("ln_mix_g", g_ln_mix_g, ln_mix_g, m_ln_mix_g, v_ln_mix_g)
    upd("ln_mix_b", g_ln_mix_b, ln_mix_b, m_ln_mix_b, v_ln_mix_b)
    upd("ln_mlp_g", g_ln_mlp_g, ln_mlp_g, m_ln_mlp_g, v_ln_mlp_g)
    upd("ln_mlp_b", g_ln_mlp_b, ln_mlp_b, m_ln_mlp_b, v_ln_mlp_b)

    order = ["conv_pw1_w", "conv_pw1_b", "conv_dw_w", "conv_dw_b", "conv_ln_g", "conv_ln_b", "conv_pw2_w",
             "conv_pw2_b", "w_kv", "attn_wq", "attn_wo", "rel_bias", "mlp_w1", "mlp_w2", "ln_mix_g", "ln_mix_b",
             "ln_mlp_g", "ln_mlp_b"]
    return (loss, grad_x, *[res[n_][0] for n_ in order], *[res[n_][1] for n_ in order],
            *[res[n_][2] for n_ in order], *[res[n_][3] for n_ in order])
```

```python
import math

import numpy as np
import jax
import jax.numpy as jnp
from jax import lax
from jax.experimental import pallas as pl
from jax.experimental.pallas import tpu as pltpu

F32 = jnp.float32
MXU_DTYPE = jnp.bfloat16
GRAD_DTYPE = jnp.bfloat16
VMEM_LIMIT_BYTES = 56 * 2**20
LANE = 128
N_DEV = 8
MESH_AXES = ("x", "y", "c")
MESH = pl.DeviceIdType.MESH

HEAD_DIM = 128
BAND = 128
BRANCHES = ((128, 1), (512, 4), (2048, 16))
ATT_TB = BAND * 16
ATT_MERGE_ROWS = 256
REL_MAX_DIST = 2048
LN_EPS = 1e-5
NEG = -1e30
HALO = 32

ADAM_LR, ADAM_B1, ADAM_B2, ADAM_EPS, ADAM_WD, ADAM_STEP = 0.001, 0.9, 0.999, 1e-08, 0.01, 10

NN = (((1,), (0,)), ((), ()))
NT = (((1,), (1,)), ((), ()))
TN = (((0,), (0,)), ((), ()))


def _cp(sem=None):
    return pltpu.CompilerParams(dimension_semantics=sem, vmem_limit_bytes=VMEM_LIMIT_BYTES)


def _sig(v):
    return 1.0 / (1.0 + jnp.exp(-v))


def _call(body, *, name, grid, in_specs, out_specs, out_shape, args, sem, scratch_shapes=(), comm=None):
    if comm is None:
        res = pl.pallas_call(body, name=name, grid=grid, in_specs=list(in_specs), out_specs=list(out_specs),
                             out_shape=list(out_shape), scratch_shapes=list(scratch_shapes),
                             compiler_params=_cp(sem))(*args)
        return list(res), []
    n_in, n_out, n_sc, nc_in, nc_out = len(in_specs), len(out_specs), len(scratch_shapes), len(comm.arrays), len(comm.out_shapes)

    def wrapped(*refs):
        pos = 0
        parts = []
        for cnt in (n_in, nc_in, n_out, nc_out, n_sc):
            parts.append(refs[pos:pos + cnt])
            pos += cnt
        ins, cin, outs, cout, sc = parts
        csem = refs[pos:]
        step = pl.program_id(0)
        for ax in range(1, len(grid)):
            step = step * grid[ax] + pl.program_id(ax)
        n_steps = math.prod(grid)

        @pl.when(step == 0)
        def _():
            comm.start(cin, cout, csem)

        if comm.relay_at is not None:
            @pl.when(step == min(n_steps - 1, int(n_steps * comm.relay_at)))
            def _():
                comm.relay(cin, cout, csem)

        body(*ins, *outs, *sc)

        @pl.when(step == n_steps - 1)
        def _():
            comm.wait(cin, cout, csem)

    res = pl.pallas_call(wrapped, name=name, grid=grid, in_specs=[*in_specs, *[ANY] * nc_in],
                         out_specs=[*out_specs, *[ANY] * nc_out], out_shape=[*out_shape, *comm.out_shapes],
                         scratch_shapes=[*scratch_shapes, *comm.sems],
                         compiler_params=_cp(("arbitrary",) * len(grid)))(*args, *comm.arrays)
    return list(res[:n_out]), list(res[n_out:])


def _mm(name, a, b, *, grid, a_spec, b_spec, dims, k_axis, epilogue, out_shapes, out_specs,
        acc_shape, extra=(), extra_specs=(), exact=False, nsplit=1, ncat=1, comm=None):
    nk = grid[k_axis]
    n_extra, n_out = len(extra), len(out_shapes)

    def dot(av, bv):
        if exact:
            return lax.dot_general(av, bv, dims, precision=lax.Precision.HIGHEST, preferred_element_type=F32)
        return lax.dot_general(av.astype(MXU_DTYPE), bv.astype(MXU_DTYPE), dims, preferred_element_type=F32)

    def product(a_ref, b_ref):
        if ncat > 1:
            return jnp.concatenate([dot(a_ref[...], b_ref[g]) for g in range(ncat)], axis=1)
        if nsplit == 1:
            return dot(a_ref[...], b_ref[...])
        w = a_ref.shape[1] // nsplit
        part = dot(a_ref[:, 0:w], b_ref[0])
        for s in range(1, nsplit):
            part = part + dot(a_ref[:, s * w:(s + 1) * w], b_ref[s])
        return part

    def body(a_ref, b_ref, *rest):
        ex, outs = rest[:n_extra], rest[n_extra:n_extra + n_out]
        if nk == 1:
            epilogue(product(a_ref, b_ref), ex, outs)
        else:
            acc = rest[n_extra + n_out]
            k = pl.program_id(k_axis)

            @pl.when(k == 0)
            def _():
                acc[...] = product(a_ref, b_ref)

            if nk > 2:
                @pl.when(jnp.logical_and(k > 0, k < nk - 1))
                def _():
                    acc[...] += product(a_ref, b_ref)

            @pl.when(k == nk - 1)
            def _():
                epilogue(acc[...] + product(a_ref, b_ref), ex, outs)

    sem = tuple("arbitrary" if ax == k_axis else "parallel" for ax in range(len(grid)))
    outs, couts = _call(body, name=name, grid=grid, in_specs=[a_spec, b_spec, *extra_specs], out_specs=out_specs,
                        out_shape=out_shapes, args=(a, b, *extra), sem=sem,
                        scratch_shapes=[pltpu.VMEM(acc_shape, F32)] if nk > 1 else [], comm=comm)
    return outs if comm is None else (outs, couts)


def _blk(n, want):
    return min(n, want)


MM_BLOCK = 1024
MM_K = 2048


def mm_fwd(name, x, w, *, colsharded, epilogue, outs, rowvec=(), tiles=(), comm=None):
    T, K = x.shape
    bm = _blk(T, MM_BLOCK)
    ncat = 1
    if colsharded:
        n_s = w.shape[2]
        N = N_DEV * n_s
        bk = K
        if n_s < MM_BLOCK and (N_DEV * n_s) % MM_BLOCK == 0:
            ncat = MM_BLOCK // n_s
            bn = MM_BLOCK
            b_spec = pl.BlockSpec((ncat, K, n_s), lambda i, j, k: (j, 0, 0))
        else:
            bn = _blk(n_s, MM_BLOCK)
            per = n_s // bn
            b_spec = pl.BlockSpec((None, K, bn), lambda i, j, k: (j // per, 0, j % per))
    else:
        N = w.shape[2]
        w = w.reshape(K, N)
        bn = _blk(N, MM_BLOCK)
        bk = _blk(K, MM_K)
        b_spec = pl.BlockSpec((bk, bn), lambda i, j, k: (k, j))
    grid = (T // bm, N // bn, K // bk)
    a_spec = pl.BlockSpec((bm, bk), lambda i, j, k: (i, k))
    tile_spec = pl.BlockSpec((bm, bn), lambda i, j, k: (i, j))
    vec_spec = pl.BlockSpec((1, bn), lambda i, j, k: (0, j))
    return _mm(name, x, w, grid=grid, a_spec=a_spec, b_spec=b_spec, dims=NN, k_axis=2, epilogue=epilogue,
               out_shapes=[jax.ShapeDtypeStruct((T, N), dt) for dt in outs], out_specs=[tile_spec] * len(outs),
               acc_shape=(bm, bn), extra=(*rowvec, *tiles),
               extra_specs=[vec_spec] * len(rowvec) + [tile_spec] * len(tiles), ncat=ncat, comm=comm)


def mm_dx(name, dy, w, *, colsharded, epilogue, outs, tiles=(), comm=None):
    T, N = dy.shape
    bm = _blk(T, MM_BLOCK)
    if colsharded:
        K, n_s = w.shape[1], w.shape[2]
        bko = _blk(K, MM_BLOCK)
        spk = max(1, min(N_DEV, MM_K // n_s))
        grid = (T // bm, K // bko, N_DEV // spk)
        a_spec = pl.BlockSpec((bm, spk * n_s), lambda i, j, s: (i, s))
        b_spec = pl.BlockSpec((spk, bko, n_s), lambda i, j, s: (s, j, 0))
    else:
        K = w.shape[1] * N_DEV
        w = w.reshape(K, N)
        bko = _blk(K, MM_BLOCK)
        spk = 1
        grid = (T // bm, K // bko, 1)
        a_spec = pl.BlockSpec((bm, N), lambda i, j, s: (i, 0))
        b_spec = pl.BlockSpec((bko, N), lambda i, j, s: (j, 0))
    tile_spec = pl.BlockSpec((bm, bko), lambda i, j, s: (i, j))
    if spk == 1 and colsharded:
        b_spec = pl.BlockSpec((None, bko, n_s), lambda i, j, s: (s, j, 0))
    return _mm(name, dy, w, grid=grid, a_spec=a_spec, b_spec=b_spec, dims=NT, k_axis=2, epilogue=epilogue,
               out_shapes=[jax.ShapeDtypeStruct((T, K), dt) for dt in outs], out_specs=[tile_spec] * len(outs),
               acc_shape=(bm, bko), extra=tuple(tiles), extra_specs=[tile_spec] * len(tiles), nsplit=spk, comm=comm)


def mm_dw(name, x, dy, *, colsharded, comm=None):
    T, K = x.shape
    N = dy.shape[1]
    narrow = x.dtype.itemsize == 2 and dy.dtype.itemsize == 2
    bt = _blk(T, 2 * MM_K if narrow else MM_K)
    bmo = _blk(K, MM_BLOCK)
    epilogue = _ep_store
    if colsharded:
        n_s = N // N_DEV
        out_shape = jax.ShapeDtypeStruct((N_DEV, K, n_s), GRAD_DTYPE)
        if n_s < MM_BLOCK and N % MM_BLOCK == 0:
            group = MM_BLOCK // n_s
            bno = MM_BLOCK
            out_spec = pl.BlockSpec((group, bmo, n_s), lambda i, j, t: (j, i, 0))

            def epilogue(acc, ex, outs):
                for g in range(group):
                    outs[0][g] = acc[:, g * n_s:(g + 1) * n_s].astype(outs[0].dtype)
        else:
            bno = _blk(n_s, MM_BLOCK)
            per = n_s // bno
            out_spec = pl.BlockSpec((None, bmo, bno), lambda i, j, t: (j // per, i, j % per))
    else:
        bno = _blk(N, MM_BLOCK)
        out_shape = jax.ShapeDtypeStruct((K, N), GRAD_DTYPE)
        out_spec = pl.BlockSpec((bmo, bno), lambda i, j, t: (i, j))
    grid = (K // bmo, N // bno, T // bt)
    a_spec = pl.BlockSpec((bt, bmo), lambda i, j, t: (t, i))
    b_spec = pl.BlockSpec((bt, bno), lambda i, j, t: (t, j))
    res = _mm(name, x, dy, grid=grid, a_spec=a_spec, b_spec=b_spec, dims=TN, k_axis=2, epilogue=epilogue,
              out_shapes=[out_shape], out_specs=[out_spec], acc_shape=(bmo, bno), comm=comm)
    (out,), couts = res if comm is not None else (res, [])
    out = out if colsharded else out.reshape(N_DEV, K // N_DEV, N)
    return out if comm is None else (out, couts)


def cast_bf16(name, a, layer=None):
    R, C = a.shape[-2:]
    br = _blk(R, 512)

    def body(a_ref, o_ref):
        o_ref[...] = a_ref[...].astype(MXU_DTYPE)

    spec = pl.BlockSpec((br, C), lambda i: (i, 0))
    in_spec = spec if layer is None else pl.BlockSpec((None, br, C), lambda i: (layer, i, 0))
    return pl.pallas_call(body, name=name, grid=(R // br,), in_specs=[in_spec], out_specs=spec,
                          out_shape=jax.ShapeDtypeStruct((R, C), MXU_DTYPE), compiler_params=_cp(("parallel",)))(a)


def _ln_stats(z):
    mu = jnp.mean(z, axis=-1, keepdims=True)
    zc = z - mu
    var = jnp.mean(zc * zc, axis=-1, keepdims=True)
    return zc * lax.rsqrt(var + LN_EPS)


def ln_fwd(name, z, g, b):
    T, D = z.shape
    br = _blk(T, 512)

    def body(z_ref, g_ref, b_ref, y_ref, yb_ref):
        y = _ln_stats(z_ref[...]) * g_ref[...] + b_ref[...]
        y_ref[...] = y
        yb_ref[...] = y.astype(MXU_DTYPE)

    row = pl.BlockSpec((br, D), lambda i: (i, 0))
    vec = pl.BlockSpec((1, D), lambda i: (0, 0))
    return pl.pallas_call(body, name=name, grid=(T // br,), in_specs=[row, vec, vec], out_specs=[row, row],
                          out_shape=[jax.ShapeDtypeStruct((T, D), F32), jax.ShapeDtypeStruct((T, D), MXU_DTYPE)],
                          compiler_params=_cp(("parallel",)))(z, g, b)


def ln_bwd(name, dy, z, g, comm=None):
    T, D = z.shape
    br = _blk(T, 512)

    def body(dy_ref, z_ref, g_ref, dz_ref, dzb_ref, dg_ref, db_ref, ds_ref):
        i = pl.program_id(0)
        z = z_ref[...]
        dy = dy_ref[...]
        mu = jnp.mean(z, axis=-1, keepdims=True)
        zc = z - mu
        var = jnp.mean(zc * zc, axis=-1, keepdims=True)
        rstd = lax.rsqrt(var + LN_EPS)
        xhat = zc * rstd
        dxh = dy * g_ref[...]
        m1 = jnp.mean(dxh, axis=-1, keepdims=True)
        m2 = jnp.mean(dxh * xhat, axis=-1, keepdims=True)
        dz = rstd * (dxh - m1 - xhat * m2)
        dz_ref[...] = dz
        dzb_ref[...] = dz.astype(MXU_DTYPE)

        @pl.when(i == 0)
        def _():
            dg_ref[...] = jnp.zeros_like(dg_ref)
            db_ref[...] = jnp.zeros_like(db_ref)
            ds_ref[...] = jnp.zeros_like(ds_ref)

        dg_ref[...] += jnp.sum(dy * xhat, axis=0, keepdims=True)
        db_ref[...] += jnp.sum(dy, axis=0, keepdims=True)
        ds_ref[...] += jnp.sum(dz, axis=0, keepdims=True)

    row = pl.BlockSpec((br, D), lambda i: (i, 0))
    vec = pl.BlockSpec((1, D), lambda i: (0, 0))
    vshape = jax.ShapeDtypeStruct((1, D), F32)
    outs, delivered = _call(body, name=name, grid=(T // br,), in_specs=[row, row, vec],
                            out_specs=[row, row, vec, vec, vec],
                            out_shape=[jax.ShapeDtypeStruct((T, D), F32), jax.ShapeDtypeStruct((T, D), MXU_DTYPE),
                                       vshape, vshape, vshape],
                            args=(dy, z, g), sem=("arbitrary",), comm=comm)
    return outs if comm is None else (outs, delivered)


def loss_ln_bwd(name, z, g, b, target):
    T, D = z.shape
    br = _blk(T, 512)
    n = T // br

    def body(z_ref, g_ref, b_ref, t_ref, loss_ref, dz_ref, dzb_ref, dg_ref, db_ref, acc_ref):
        i = pl.program_id(0)
        z = z_ref[...]
        mu = jnp.mean(z, axis=-1, keepdims=True)
        zc = z - mu
        var = jnp.mean(zc * zc, axis=-1, keepdims=True)
        rstd = lax.rsqrt(var + LN_EPS)
        xhat = zc * rstd
        err = (xhat * g_ref[...] + b_ref[...]) - t_ref[...]
        dy = err * (1.0 / D)
        dxh = dy * g_ref[...]
        m1 = jnp.mean(dxh, axis=-1, keepdims=True)
        m2 = jnp.mean(dxh * xhat, axis=-1, keepdims=True)
        dz = rstd * (dxh - m1 - xhat * m2)
        dz_ref[...] = dz
        dzb_ref[...] = dz.astype(MXU_DTYPE)

        @pl.when(i == 0)
        def _():
            acc_ref[...] = jnp.zeros_like(acc_ref)
            dg_ref[...] = jnp.zeros_like(dg_ref)
            db_ref[...] = jnp.zeros_like(db_ref)

        acc_ref[...] += jnp.sum(err * err, axis=0, keepdims=True)
        dg_ref[...] += jnp.sum(dy * xhat, axis=0, keepdims=True)
        db_ref[...] += jnp.sum(dy, axis=0, keepdims=True)

        @pl.when(i == n - 1)
        def _():
            loss_ref[...] = (0.5 / D) * jnp.sum(acc_ref[...], axis=1, keepdims=True)

    row = pl.BlockSpec((br, D), lambda i: (i, 0))
    vec = pl.BlockSpec((1, D), lambda i: (0, 0))
    vshape = jax.ShapeDtypeStruct((1, D), F32)
    return pl.pallas_call(body, name=name, grid=(n,), in_specs=[row, vec, vec, row],
                          out_specs=[pl.BlockSpec((1, 1), lambda i: (0, 0)), row, row, vec, vec],
                          out_shape=[jax.ShapeDtypeStruct((1, 1), F32), jax.ShapeDtypeStruct((T, D), F32),
                                     jax.ShapeDtypeStruct((T, D), MXU_DTYPE), vshape, vshape],
                          scratch_shapes=[pltpu.VMEM((1, D), F32)],
                          compiler_params=_cp(("arbitrary",)))(z, g, b, target)


CONV_BT = 128
SUBLANES = 8


def _shifted(ref, cs, bt, offsets, stage):
    for r in range(SUBLANES):
        offs = [off for off in offsets if off % SUBLANES == r]
        if offs:
            rows = bt + max(offs) - r
            stage[r, 0:rows, :] = ref[pl.ds(r, rows), cs]
    for k, off in enumerate(offsets):
        r = off % SUBLANES
        yield k, stage[r, off - r:off - r + bt, :]


def conv_fwd(name, h1, dw_w, dw_b, ln_g, ln_b, comm=None):
    T, D2 = h1.shape
    D = D2 // 2
    W = dw_w.shape[0]
    taps = W - 1
    bt = _blk(T, CONV_BT)
    hb = bt // HALO
    u_off = [HALO - (taps - 1) + k for k in range(taps)]

    def body(h_ref, hp_ref, w_ref, b_ref, g_ref, be_ref, c_ref, s_ref, ux, stage):
        i = pl.program_id(0)
        up = hp_ref[:, :D] * _sig(hp_ref[:, D:])
        ux[0:HALO, :] = jnp.where(i == 0, 0.0, up)
        ux[HALO:HALO + bt, :] = h_ref[:, :D] * _sig(h_ref[:, D:])
        for cb in range(D // LANE):
            cs = slice(cb * LANE, (cb + 1) * LANE)
            acc = jnp.broadcast_to(b_ref[:, cs], (bt, LANE))
            for k, u_k in _shifted(ux, cs, bt, u_off, stage.at[cb % 2]):
                acc = acc + w_ref[k:k + 1, cs] * u_k
            c_ref[:, cs] = acc
        n = _ln_stats(c_ref[...]) * g_ref[...] + be_ref[...]
        s_ref[...] = (n * _sig(n)).astype(MXU_DTYPE)

    main = pl.BlockSpec((bt, D2), lambda i: (i, 0))
    prev = pl.BlockSpec((HALO, D2), lambda i: (jnp.maximum(i * hb - 1, 0), 0))
    wspec = pl.BlockSpec((W, D), lambda i: (0, 0))
    vec = pl.BlockSpec((1, D), lambda i: (0, 0))
    row = pl.BlockSpec((bt, D), lambda i: (i, 0))
    return _call(body, name=name, grid=(T // bt,), in_specs=[main, prev, wspec, vec, vec, vec], out_specs=[row, row],
                 out_shape=[jax.ShapeDtypeStruct((T, D), F32), jax.ShapeDtypeStruct((T, D), MXU_DTYPE)],
                 scratch_shapes=[pltpu.VMEM((HALO + bt, D), F32), pltpu.VMEM((2, SUBLANES, HALO + bt, LANE), F32)],
                 args=(h1, h1, dw_w, dw_b, ln_g, ln_b), sem=("parallel",), comm=comm)


def conv_bwd_ln(name, ds, c, ln_g, ln_b, comm=None):
    T, D = c.shape
    br = _blk(T, 512)

    def body(ds_ref, c_ref, g_ref, be_ref, dc_ref, dg_ref, db_ref, dcs_ref):
        i = pl.program_id(0)
        c = c_ref[...]
        mu = jnp.mean(c, axis=-1, keepdims=True)
        cc = c - mu
        var = jnp.mean(cc * cc, axis=-1, keepdims=True)
        rstd = lax.rsqrt(var + LN_EPS)
        xhat = cc * rstd
        n = xhat * g_ref[...] + be_ref[...]
        sg = _sig(n)
        dn = ds_ref[...] * (sg * (1.0 + n * (1.0 - sg)))
        dxh = dn * g_ref[...]
        m1 = jnp.mean(dxh, axis=-1, keepdims=True)
        m2 = jnp.mean(dxh * xhat, axis=-1, keepdims=True)
        dc = rstd * (dxh - m1 - xhat * m2)
        dc_ref[...] = dc

        @pl.when(i == 0)
        def _():
            dg_ref[...] = jnp.zeros_like(dg_ref)
            db_ref[...] = jnp.zeros_like(db_ref)
            dcs_ref[...] = jnp.zeros_like(dcs_ref)

        dg_ref[...] += jnp.sum(dn * xhat, axis=0, keepdims=True)
        db_ref[...] += jnp.sum(dn, axis=0, keepdims=True)
        dcs_ref[...] += jnp.sum(dc, axis=0, keepdims=True)

    row = pl.BlockSpec((br, D), lambda i: (i, 0))
    vec = pl.BlockSpec((1, D), lambda i: (0, 0))
    vshape = jax.ShapeDtypeStruct((1, D), F32)
    outs, delivered = _call(body, name=name, grid=(T // br,), in_specs=[row, row, vec, vec],
                            out_specs=[row, vec, vec, vec],
                            out_shape=[jax.ShapeDtypeStruct((T, D), F32), vshape, vshape, vshape],
                            args=(ds, c, ln_g, ln_b), sem=("arbitrary",), comm=comm)
    return outs if comm is None else (outs, delivered)


def conv_bwd_dw(name, dc, h1, dw_w, comm=None):
    T, D2 = h1.shape
    D = D2 // 2
    W = dw_w.shape[0]
    taps = W - 1
    bt = _blk(T, CONV_BT)
    hb = bt // HALO
    n = T // bt

    def body(dc_ref, dcn_ref, h_ref, hp_ref, w_ref, dh_ref, dw_ref, dhs_ref, ux, dcx, du, stage_dc, stage_u):
        i = pl.program_id(0)
        a = h_ref[:, :D]
        sg = _sig(h_ref[:, D:])
        up = hp_ref[:, :D] * _sig(hp_ref[:, D:])
        ux[0:HALO, :] = jnp.where(i == 0, 0.0, up)
        ux[HALO:HALO + bt, :] = a * sg
        dcx[0:bt, :] = dc_ref[...]
        dcx[bt:bt + HALO, :] = jnp.where(i == n - 1, 0.0, dcn_ref[...])

        @pl.when(i == 0)
        def _():
            dw_ref[...] = jnp.zeros_like(dw_ref)
            dhs_ref[...] = jnp.zeros_like(dhs_ref)

        for cb in range(D // LANE):
            cs = slice(cb * LANE, (cb + 1) * LANE)
            dcb = dcx[0:bt, cs]
            acc = jnp.zeros((bt, LANE), F32)
            for k, dc_k in _shifted(dcx, cs, bt, [taps - 1 - k for k in range(taps)], stage_dc.at[cb % 2]):
                acc = acc + w_ref[k:k + 1, cs] * dc_k
            du[:, cs] = acc
            for k, u_k in _shifted(ux, cs, bt, [HALO - (taps - 1) + k for k in range(taps)], stage_u.at[cb % 2]):
                dw_ref[k:k + 1, cs] += jnp.sum(dcb * u_k, axis=0, keepdims=True)
        d_u = du[...]
        da = d_u * sg
        dg = d_u * a * sg * (1.0 - sg)
        dh_ref[:, :D] = da.astype(MXU_DTYPE)
        dh_ref[:, D:] = dg.astype(MXU_DTYPE)
        dhs_ref[:, :D] += jnp.sum(da, axis=0, keepdims=True)
        dhs_ref[:, D:] += jnp.sum(dg, axis=0, keepdims=True)

    row = pl.BlockSpec((bt, D), lambda i: (i, 0))
    nxt = pl.BlockSpec((HALO, D), lambda i: (jnp.minimum((i + 1) * hb, T // HALO - 1), 0))
    main = pl.BlockSpec((bt, D2), lambda i: (i, 0))
    prev = pl.BlockSpec((HALO, D2), lambda i: (jnp.maximum(i * hb - 1, 0), 0))
    wspec = pl.BlockSpec((W, D), lambda i: (0, 0))
    return _call(body, name=name, grid=(n,), in_specs=[row, nxt, main, prev, wspec],
                 out_specs=[main, wspec, pl.BlockSpec((1, D2), lambda i: (0, 0))],
                 out_shape=[jax.ShapeDtypeStruct((T, D2), MXU_DTYPE), jax.ShapeDtypeStruct((W, D), F32),
                            jax.ShapeDtypeStruct((1, D2), F32)],
                 scratch_shapes=[pltpu.VMEM((HALO + bt, D), F32), pltpu.VMEM((bt + HALO, D), F32),
                                 pltpu.VMEM((bt, D), F32), pltpu.VMEM((2, SUBLANES, HALO + bt, LANE), F32),
                                 pltpu.VMEM((2, SUBLANES, HALO + bt, LANE), F32)],
                 args=(dc, dc, h1, h1, dw_w), sem=("arbitrary",), comm=comm)


def _t5_bucket(dist, n_buckets):
    max_exact = n_buckets // 2
    large = max_exact + (np.log(np.maximum(dist, 1) / max_exact) / math.log(REL_MAX_DIST / max_exact)
                         * (n_buckets - max_exact)).astype(np.int32)
    large = np.minimum(large, n_buckets - 1)
    return np.where(dist < max_exact, dist, large).astype(np.int32)


def _band_tables(n_buckets):
    i = np.arange(BAND)[:, None]
    j = np.arange(2 * BAND)[None, :]
    delta = i - j + BAND
    out = []
    for window, dil in BRANCHES:
        ok = (delta >= 0) & (delta <= window // dil)
        out.append((_t5_bucket(np.clip(delta, 0, None) * dil, n_buckets), ok))
    return out


def _units():
    for bi in (2, 1, 0):
        d = BRANCHES[bi][1]
        for r in range(d):
            for nb in range(ATT_TB // (BAND * d)):
                yield bi, d, r, nb


def _staggered(units, stages):
    n = len(stages)
    state = {}
    for t in range(len(units) + n - 1):
        for k in range(n):
            u = t - k
            if 0 <= u < len(units):
                state[u] = stages[k](units[u], state.get(u))
    return


def _rows(start, size, d):
    return pl.ds(start, size) if d == 1 else pl.ds(start, size, stride=d)


def _bc(v):
    return jnp.broadcast_to(v, (BAND, LANE))


def attn_fwd(name, q, kv, bias, comm=None):
    T, D = q.shape
    H = D // HEAD_DIM
    TB = ATT_TB
    scale = HEAD_DIM ** -0.5

    def body(q_ref, kc_ref, kp_ref, vc_ref, vp_ref, b_ref, o_ref, l_ref, kx, vx, m_sc, s_sc, a_sc):
        first = (pl.program_id(1) == 0).astype(jnp.int32)
        kx[0:TB, :] = kp_ref[...]
        kx[TB:2 * TB, :] = kc_ref[...]
        vx[0:TB, :] = vp_ref[...]
        vx[TB:2 * TB, :] = vc_ref[...]
        def scores(unit, _):
            bi, d, r, nb = unit
            qs = _rows(nb * BAND * d + r, BAND, d)
            ks = _rows(TB + (nb - 1) * BAND * d + r, 2 * BAND, d)
            qb = q_ref[qs, :].astype(MXU_DTYPE)
            kb = kx[ks, :].astype(MXU_DTYPE)
            vb = vx[ks, :].astype(MXU_DTYPE)
            b = b_ref[bi + 3 * first] if nb == 0 else b_ref[bi]
            return qs, vb, lax.dot_general(qb, kb, NT, preferred_element_type=F32) * scale + b

        def softmax(unit, st):
            qs, vb, s = st
            mrow = jnp.max(s, axis=1, keepdims=True)
            p = jnp.exp(s - mrow)
            m_sc.at[unit[0]][qs, :] = _bc(mrow)
            s_sc.at[unit[0]][qs, :] = _bc(jnp.sum(p, axis=1, keepdims=True))
            return qs, vb, p.astype(MXU_DTYPE)

        def values(unit, st):
            qs, vb, pb = st
            a_sc.at[unit[0]][qs, :] = lax.dot_general(pb, vb, NN, preferred_element_type=F32)

        _staggered(list(_units()), [scores, softmax, values])
        for c0 in range(0, TB, ATT_MERGE_ROWS):
            rows = slice(c0, c0 + ATT_MERGE_ROWS)
            ms = [m_sc[bi, rows, :] for bi in range(3)]
            m = jnp.maximum(jnp.maximum(ms[0], ms[1]), ms[2])
            ws = [jnp.exp(mi - m) for mi in ms]
            den = ws[0] * s_sc[0, rows, :] + ws[1] * s_sc[1, rows, :] + ws[2] * s_sc[2, rows, :]
            num = ws[0] * a_sc[0, rows, :] + ws[1] * a_sc[1, rows, :] + ws[2] * a_sc[2, rows, :]
            o_ref[rows, :] = num / den
            l_ref[rows, :] = m + jnp.log(den)

    cur = lambda off: pl.BlockSpec((TB, HEAD_DIM), lambda h, i: (i, off + h))
    prv = lambda off: pl.BlockSpec((TB, HEAD_DIM), lambda h, i: (jnp.maximum(i - 1, 0), off + h))
    bspec = pl.BlockSpec((6, None, BAND, 2 * BAND), lambda h, i: (0, h, 0, 0))
    sc = lambda *shape: pltpu.VMEM(shape + (HEAD_DIM,), F32)
    return _call(body, name=name, grid=(H, T // TB), in_specs=[cur(0), cur(0), prv(0), cur(H), prv(H), bspec],
                 out_specs=[cur(0), cur(0)],
                 out_shape=[jax.ShapeDtypeStruct((T, D), F32), jax.ShapeDtypeStruct((T, D), F32)],
                 scratch_shapes=[sc(2 * TB), sc(2 * TB), sc(3, TB), sc(3, TB), sc(3, TB)],
                 args=(q, kv, kv, kv, kv, bias), sem=("parallel", "parallel"), comm=comm)


def attn_bwd(name, q, kv, bias, o, lse, do, comm=None):
    T, D = q.shape
    H = D // HEAD_DIM
    TB = ATT_TB
    nI = T // TB
    scale = HEAD_DIM ** -0.5

    def body(q_ref, kc_ref, kp_ref, vc_ref, vp_ref, b_ref, o_ref, l_ref, do_ref,
             dq_ref, dk_ref, dv_ref, db_ref, kx, vx, dkx, dvx, ck, cv, dl_sc, dq_sc):
        step = pl.program_id(1)
        first = (step == nI - 1).astype(jnp.int32)
        kx[0:TB, :] = kp_ref[...]
        kx[TB:2 * TB, :] = kc_ref[...]
        vx[0:TB, :] = vp_ref[...]
        vx[TB:2 * TB, :] = vc_ref[...]
        dl_sc[...] = jnp.broadcast_to(jnp.sum(do_ref[...] * o_ref[...], axis=1, keepdims=True), (TB, LANE))

        @pl.when(step == 0)
        def _():
            db_ref[...] = jnp.zeros_like(db_ref)
            ck[...] = jnp.zeros_like(ck)
            cv[...] = jnp.zeros_like(cv)

        held = [None, None]

        def scores(unit, _):
            bi, d, r, nb = unit
            qs = _rows(nb * BAND * d + r, BAND, d)
            ks = _rows(TB + (nb - 1) * BAND * d + r, 2 * BAND, d)
            qb = q_ref[qs, :].astype(MXU_DTYPE)
            kb = kx[ks, :].astype(MXU_DTYPE)
            vb = vx[ks, :].astype(MXU_DTYPE)
            dob = do_ref[qs, :].astype(MXU_DTYPE)
            b = b_ref[bi + 3 * first] if nb == 0 else b_ref[bi]
            s = lax.dot_general(qb, kb, NT, preferred_element_type=F32) * scale + b
            dp = lax.dot_general(dob, vb, NT, preferred_element_type=F32)
            return qs, qb, kb, dob, s, dp

        def softmax_bwd(unit, st):
            qs, qb, kb, dob, s, dp = st
            p = jnp.exp(s - l_ref[qs, :][:, :1])
            dsv = p * (dp - dl_sc[qs, :][:, :1])
            db_ref[unit[0]] += dsv
            return qs, qb, kb, dob, p.astype(MXU_DTYPE), dsv.astype(MXU_DTYPE)

        def products(unit, st):
            bi, d, r, nb = unit
            qs, qb, kb, dob, pb, dsb = st
            band = lambda b: _rows(TB + b * BAND * d + r, BAND, d)
            dv_blk = lax.dot_general(pb, dob, TN, preferred_element_type=F32)
            dk_blk = lax.dot_general(dsb, qb, TN, preferred_element_type=F32) * scale
            dq_sc.at[bi][qs, :] = lax.dot_general(dsb, kb, NN, preferred_element_type=F32) * scale
            dkb, dvb = dkx.at[bi], dvx.at[bi]
            if nb == 0:
                dkb[band(-1), :] = dk_blk[:BAND]
                dvb[band(-1), :] = dv_blk[:BAND]
            else:
                dkb[band(nb - 1), :] = held[0] + dk_blk[:BAND]
                dvb[band(nb - 1), :] = held[1] + dv_blk[:BAND]
            held[0], held[1] = dk_blk[BAND:], dv_blk[BAND:]
            if nb == TB // (BAND * d) - 1:
                dkb[band(nb), :] = held[0]
                dvb[band(nb), :] = held[1]

        _staggered(list(_units()), [scores, softmax_bwd, products])

        dq_ref[...] = (dq_sc[0] + dq_sc[1] + dq_sc[2]).astype(dq_ref.dtype)
        for dx_ref, dxx, cx in ((dk_ref, dkx, ck), (dv_ref, dvx, cv)):
            dx_ref[...] = (dxx[0, TB:2 * TB, :] + dxx[1, TB:2 * TB, :] + dxx[2, TB:2 * TB, :] + cx[...]).astype(dx_ref.dtype)
            cx[...] = dxx[2, 0:TB, :]
            for bi in (1, 0):
                lo = TB - BAND * BRANCHES[bi][1]
                cx[lo:TB, :] += dxx[bi, lo:TB, :]

    blk = lambda h, i: nI - 1 - i
    cur = lambda off: pl.BlockSpec((TB, HEAD_DIM), lambda h, i: (blk(h, i), off + h))
    prv = lambda off: pl.BlockSpec((TB, HEAD_DIM), lambda h, i: (jnp.maximum(blk(h, i) - 1, 0), off + h))
    bspec = pl.BlockSpec((3, None, BAND, 2 * BAND), lambda h, i: (0, h, 0, 0))
    bspec_in = pl.BlockSpec((6, None, BAND, 2 * BAND), lambda h, i: (0, h, 0, 0))
    sc = lambda *shape: pltpu.VMEM(shape + (HEAD_DIM,), F32)
    return _call(body, name=name, grid=(H, nI),
                 in_specs=[cur(0), cur(0), prv(0), cur(H), prv(H), bspec_in, cur(0), cur(0), cur(0)],
                 out_specs=[cur(0), cur(0), cur(0), bspec],
                 out_shape=[jax.ShapeDtypeStruct((T, D), MXU_DTYPE), jax.ShapeDtypeStruct((T, D), MXU_DTYPE),
                            jax.ShapeDtypeStruct((T, D), MXU_DTYPE), jax.ShapeDtypeStruct((3, H, BAND, 2 * BAND), F32)],
                 scratch_shapes=[sc(2 * TB), sc(2 * TB), sc(3, 2 * TB), sc(3, 2 * TB), sc(TB), sc(TB), sc(TB), sc(3, TB)],
                 args=(q, kv, kv, kv, kv, bias, o, lse, do), sem=("arbitrary", "arbitrary"), comm=comm)


def adamw(name, parts, w, m, v):
    L, R, C = w.shape
    P = parts[0].shape[0]
    br = R if R % 8 else _blk(R, max(8, (1 << 18) // C))
    c1 = 1.0 / (1.0 - ADAM_B1 ** ADAM_STEP)
    c2 = 1.0 / (1.0 - ADAM_B2 ** ADAM_STEP)

    def body(*refs):
        p_refs = refs[:L]
        w_ref, m_ref, v_ref, g_ref, d_ref, nm_ref, nv_ref = refs[L:]
        lay = pl.program_id(0)

        def total(p_ref):
            g = p_ref[0].astype(F32)
            for k in range(1, P):
                g = g + p_ref[k].astype(F32)
            return g

        g = total(p_refs[0])
        for j in range(1, L):
            g = jnp.where(lay == j, total(p_refs[j]), g)
        nm = ADAM_B1 * m_ref[...] + (1.0 - ADAM_B1) * g
        nv = ADAM_B2 * v_ref[...] + (1.0 - ADAM_B2) * (g * g)
        g_ref[...] = g
        nm_ref[...] = nm
        nv_ref[...] = nv
        d_ref[...] = -ADAM_LR * ((nm * c1) / (jnp.sqrt(nv * c2) + ADAM_EPS) + ADAM_WD * w_ref[...])

    row = pl.BlockSpec((None, br, C), lambda l, i: (l, i, 0))
    pspecs = [pl.BlockSpec((P, br, C), lambda l, i, j=j: (0, jnp.where(l == j, i, 0), 0)) for j in range(L)]
    shp = jax.ShapeDtypeStruct((L, R, C), F32)
    return pl.pallas_call(body, name=name, grid=(L, R // br), in_specs=[*pspecs, row, row, row],
                          out_specs=[row] * 4, out_shape=[shp] * 4,
                          compiler_params=_cp(("parallel", "parallel")))(*parts, w, m, v)


def pair_sum(name, g, s1, c_idx):
    _, _, R, C = g.shape
    br = _blk(R, max(16, (1 << 19) // C))

    def body(c_ref, g_ref, s_ref, t_ref):
        t_ref[...] = (g_ref[...].astype(F32) + s_ref[...].astype(F32)).astype(t_ref.dtype)

    return pl.pallas_call(
        body, name=name,
        grid_spec=pltpu.PrefetchScalarGridSpec(
            num_scalar_prefetch=1, grid=(4, R // br),
            in_specs=[pl.BlockSpec((None, None, br, C), lambda j, i, c: (j, c[0], i, 0)),
                      pl.BlockSpec((None, br, C), lambda j, i, c: (j, i, 0))],
            out_specs=pl.BlockSpec((None, br, C), lambda j, i, c: (j, i, 0))),
        out_shape=jax.ShapeDtypeStruct((4, R, C), g.dtype), compiler_params=_cp(("parallel", "parallel")))(c_idx, g, s1)


def _me():
    return lax.axis_index("x"), lax.axis_index("y"), lax.axis_index("c")


ANY = pl.BlockSpec(memory_space=pl.ANY)


class Gather:
    relay_at = 0.7

    def __init__(self, shards):
        n = len(shards)
        self.arrays = list(shards)
        self.out_shapes = [jax.ShapeDtypeStruct((N_DEV,) + s.shape, s.dtype) for s in shards]
        self.sems = [pltpu.SemaphoreType.DMA((n, 7)), pltpu.SemaphoreType.DMA((n, 7)), pltpu.SemaphoreType.DMA((n,))]

    def _ctx(self, ins, outs, sems):
        send_sems, recv_sems, local_sems = sems
        x, y, c = _me()
        chips = [(1 - x, y), (x, 1 - y), (1 - x, 1 - y)]

        def copy(a, k, block, to, src=None):
            px, py, pc = block
            dst = outs[a].at[4 * px + 2 * py + pc]
            return pltpu.make_async_remote_copy(src_ref=dst if src is None else src, dst_ref=dst,
                                                send_sem=send_sems.at[a, k], recv_sem=recv_sems.at[a, k],
                                                device_id=to, device_id_type=MESH)

        def own(a):
            return pltpu.make_async_copy(ins[a], outs[a].at[4 * x + 2 * y + c], local_sems.at[a])

        def first(a):
            return [copy(a, 0, (x, y, c), (x, y, 1 - c), src=ins[a])] + \
                   [copy(a, 1 + j, (x, y, c), (*chip, c), src=ins[a]) for j, chip in enumerate(chips)]

        def passed(a):
            return [copy(a, 4 + j, (*chip, c), (x, y, 1 - c)) for j, chip in enumerate(chips)]

        return (x, y, c), chips, copy, own, first, passed

    def start(self, ins, outs, sems):
        _, _, _, own, first, _ = self._ctx(ins, outs, sems)
        for a in range(len(ins)):
            own(a).start()
            for cp in first(a):
                cp.start()

    def relay(self, ins, outs, sems):
        me, chips, copy, _, _, passed = self._ctx(ins, outs, sems)
        for a in range(len(ins)):
            fwd = passed(a)
            for j, chip in enumerate(chips):
                copy(a, 1 + j, (*chip, me[2]), me).wait_recv()
                fwd[j].start()

    def wait(self, ins, outs, sems):
        (x, y, c), chips, copy, own, first, passed = self._ctx(ins, outs, sems)
        for a in range(len(ins)):
            copy(a, 0, (x, y, 1 - c), (x, y, c)).wait_recv()
            for j, chip in enumerate(chips):
                copy(a, 4 + j, (*chip, 1 - c), (x, y, c)).wait_recv()
            for cp in first(a) + passed(a):
                cp.wait_send()
            own(a).wait()


def all_gather(name, shards):
    comm = Gather(shards)
    n = len(shards)

    def body(*refs):
        comm.start(refs[:n], refs[n:2 * n], refs[2 * n:])
        comm.relay(refs[:n], refs[n:2 * n], refs[2 * n:])
        comm.wait(refs[:n], refs[n:2 * n], refs[2 * n:])

    return pl.pallas_call(body, name=name, in_specs=[ANY] * n, out_specs=[ANY] * n, out_shape=comm.out_shapes,
                          scratch_shapes=comm.sems)(*shards)


class PairExchange:
    relay_at = None

    def __init__(self, grads):
        n = len(grads)
        self.arrays = list(grads)
        self.out_shapes = [jax.ShapeDtypeStruct((4,) + g.shape[2:], g.dtype) for g in grads]
        self.sems = [pltpu.SemaphoreType.DMA((n, 4)), pltpu.SemaphoreType.DMA((n, 4))]

    def _copies(self, ins, outs, sems):
        send_sems, recv_sems = sems
        x, y, c = _me()
        return [pltpu.make_async_remote_copy(
            src_ref=ins[a].at[j, 1 - c], dst_ref=outs[a].at[j], send_sem=send_sems.at[a, j],
            recv_sem=recv_sems.at[a, j], device_id=(x, y, 1 - c), device_id_type=MESH)
            for a in range(len(ins)) for j in range(4)]

    def start(self, ins, outs, sems):
        for cp in self._copies(ins, outs, sems):
            cp.start()

    def wait(self, ins, outs, sems):
        for cp in self._copies(ins, outs, sems):
            cp.wait()


def pair_exchange(name, grads):
    comm = PairExchange(grads)
    n = len(grads)

    def body(*refs):
        comm.start(refs[:n], refs[n:2 * n], refs[2 * n:])
        comm.wait(refs[:n], refs[n:2 * n], refs[2 * n:])

    return pl.pallas_call(body, name=name, in_specs=[ANY] * n, out_specs=[ANY] * n, out_shape=comm.out_shapes,
                          scratch_shapes=comm.sems)(*grads)


class ChipExchange:
    relay_at = None

    def __init__(self, sums):
        n = len(sums)
        self.arrays = list(sums)
        self.out_shapes = [jax.ShapeDtypeStruct(s.shape, s.dtype) for s in sums]
        self.sems = [pltpu.SemaphoreType.DMA((n, 3)), pltpu.SemaphoreType.DMA((n, 3)), pltpu.SemaphoreType.DMA((n,))]

    def _copies(self, ins, outs, sems, arrivals):
        send_sems, recv_sems, local_sems = sems
        x, y, c = _me()
        my = 2 * x + y
        chips = [(1 - x, y), (x, 1 - y), (1 - x, 1 - y)]
        mine, sends, recvs = [], [], []
        for a in range(len(ins)):
            mine.append(pltpu.make_async_copy(ins[a].at[my], outs[a].at[my], local_sems.at[a]))
            for j, (px, py) in enumerate(chips):
                sends.append(pltpu.make_async_remote_copy(
                    src_ref=ins[a].at[2 * px + py], dst_ref=outs[a].at[my],
                    send_sem=send_sems.at[a, j], recv_sem=recv_sems.at[a, j],
                    device_id=(px, py, c), device_id_type=MESH))
                if arrivals:
                    recvs.append(pltpu.make_async_remote_copy(
                        src_ref=ins[a].at[my], dst_ref=outs[a].at[2 * px + py],
                        send_sem=send_sems.at[a, j], recv_sem=recv_sems.at[a, j],
                        device_id=(px, py, c), device_id_type=MESH))
        return mine, sends, recvs

    def start(self, ins, outs, sems):
        mine, sends, _ = self._copies(ins, outs, sems, arrivals=False)
        for cp in mine + sends:
            cp.start()

    def wait(self, ins, outs, sems):
        mine, sends, recvs = self._copies(ins, outs, sems, arrivals=True)
        for cp in recvs:
            cp.wait_recv()
        for cp in sends:
            cp.wait_send()
        for cp in mine:
            cp.wait()


def all_reduce_small(name, pack):
    R, C = pack.shape

    def body(p_ref, o_ref, buf, send_sems, recv_sems):
        x, y, c = _me()
        me = 4 * x + 2 * y + c
        buf[me] = p_ref[...]
        copies = []
        for k in range(1, N_DEV):
            px, py, pc = x ^ (k >> 2), y ^ ((k >> 1) & 1), c ^ (k & 1)
            copies.append(pltpu.make_async_remote_copy(
                src_ref=p_ref, dst_ref=buf.at[me], send_sem=send_sems.at[k - 1], recv_sem=recv_sems.at[k - 1],
                device_id=(px, py, pc), device_id_type=MESH))
        for cp in copies:
            cp.start()
        for cp in copies:
            cp.wait()
        acc = buf[0]
        for d in range(1, N_DEV):
            acc = acc + buf[d]
        o_ref[...] = acc

    vm = pl.BlockSpec(memory_space=pltpu.VMEM)
    return pl.pallas_call(
        body, name=name, in_specs=[vm], out_specs=vm, out_shape=jax.ShapeDtypeStruct((R, C), F32),
        scratch_shapes=[pltpu.VMEM((N_DEV, R, C), F32), pltpu.SemaphoreType.DMA((N_DEV - 1,)),
                        pltpu.SemaphoreType.DMA((N_DEV - 1,))],
    )(pack)


def _ep_bias(acc, ex, outs):
    outs[0][...] = acc + ex[0][...]


def _ep_store(acc, ex, outs):
    outs[0][...] = acc.astype(outs[0].dtype)


def _ep_resid(alpha, bias):
    def ep(acc, ex, outs):
        if bias:
            outs[0][...] = alpha * ex[1][...] + (acc + ex[0][...])
        else:
            outs[0][...] = alpha * ex[0][...] + acc
    return ep


def _ep_relu2(acc, ex, outs):
    r = jnp.maximum(acc, 0.0)
    outs[0][...] = (r * r).astype(outs[0].dtype)
    outs[1][...] = r


def _ep_relu2_bwd(acc, ex, outs):
    outs[0][...] = (acc * (2.0 * ex[0][...])).astype(outs[0].dtype)


def _ep_add(acc, ex, outs):
    outs[0][...] = ex[0][...] + acc


def kernel(x, conv_pw1_w, conv_pw1_b, conv_dw_w, conv_dw_b, conv_ln_g, conv_ln_b, conv_pw2_w, conv_pw2_b, w_kv, attn_wq, attn_wo, rel_bias, mlp_w1, mlp_w2, ln_mix_g, ln_mix_b, ln_mlp_g, ln_mlp_b, loss_target, m_conv_pw1_w, m_conv_pw1_b, m_conv_dw_w, m_conv_dw_b, m_conv_ln_g, m_conv_ln_b, m_conv_pw2_w, m_conv_pw2_b, m_w_kv, m_attn_wq, m_attn_wo, m_rel_bias, m_mlp_w1, m_mlp_w2, m_ln_mix_g, m_ln_mix_b, m_ln_mlp_g, m_ln_mlp_b, v_conv_pw1_w, v_conv_pw1_b, v_conv_dw_w, v_conv_dw_b, v_conv_ln_g, v_conv_ln_b, v_conv_pw2_w, v_conv_pw2_b, v_w_kv, v_attn_wq, v_attn_wo, v_rel_bias, v_mlp_w1, v_mlp_w2, v_ln_mix_g, v_ln_mix_b, v_ln_mlp_g, v_ln_mlp_b):
    T, D = x.shape[1], x.shape[2]
    H = D // HEAD_DIM
    depth = mlp_w1.shape[0]
    assert depth == 2 and T % ATT_TB == 0
    alpha = (2 * depth) ** 0.25
    ds_ = D // N_DEV
    xi, yi, ci = _me()
    me = 4 * xi + 2 * yi + ci
    c_idx = ci.astype(jnp.int32).reshape(1)
    x2 = x.reshape(T, D)
    target = loss_target.reshape(T, D)
    n_buckets = rel_bias.shape[0]
    taps = conv_dw_w.shape[1]

    local = {
        "pw1": (conv_pw1_w, 0), "pw2": (conv_pw2_w, 0), "wkv": (w_kv, None), "wq": (attn_wq, 0), "wo": (attn_wo, 0),
        "w1_0": (mlp_w1, 0), "w1_1": (mlp_w1, 1), "w2_0": (mlp_w2, 0), "w2_1": (mlp_w2, 1),
    }
    names = list(local)
    small = jnp.concatenate([conv_dw_w[0], conv_dw_b, conv_ln_g, conv_ln_b, conv_pw2_b,
                             conv_pw1_b.reshape(2, ds_)], axis=0)
    small = jnp.pad(small, ((0, (-small.shape[0]) % 8), (0, 0)))
    shard = {k: cast_bf16("cast_" + k, *local[k]) for k in names}
    G = {}

    def gather_start(*keys):
        return Gather([shard[k] for k in keys])

    def gather_done(keys, delivered):
        G.update(zip(keys, delivered))

    g_pw1, g_small = all_gather("gather_pw1", [shard["pw1"], small])
    G["pw1"] = g_pw1
    sm = jnp.transpose(g_small, (1, 0, 2)).reshape(small.shape[0], D)
    dw_w = jnp.pad(sm[:taps], ((0, HALO - taps), (0, 0)))
    dw_b, cln_g, cln_b, pw2_b = (sm[taps + k:taps + k + 1] for k in range(4))
    pw1_b = g_small[:, taps + 4:taps + 6, :].reshape(1, 2 * D)

    row = lambda a, l: a[l:l + 1]

    x_b = cast_bf16("cast_x", x2)
    (h1,) = mm_fwd("pw1", x_b, G["pw1"], colsharded=True, epilogue=_ep_bias, outs=[F32], rowvec=[pw1_b])
    (c_pre, s_b), delivered = conv_fwd("conv_fwd", h1, dw_w, dw_b, cln_g, cln_b, comm=gather_start("pw2", "w1_0"))
    gather_done(("pw2", "w1_0"), delivered)
    (z1,) = mm_fwd("pw2", s_b, G["pw2"], colsharded=False, epilogue=_ep_resid(alpha, True), outs=[F32],
                   rowvec=[pw2_b], tiles=[x2])
    x1, x1_b = ln_fwd("ln_mix0", z1, row(ln_mix_g, 0), row(ln_mix_b, 0))

    def mlp_fwd(l, xin, xin_b, up_keys=(), down_keys=()):
        res_ = mm_fwd(f"mlp_up{l}", xin_b, G[f"w1_{l}"], colsharded=True, epilogue=_ep_relu2, outs=[MXU_DTYPE, F32],
                      comm=gather_start(*up_keys) if up_keys else None)
        (act_b, r), delivered = res_ if up_keys else (res_, [])
        gather_done(up_keys, delivered)
        res_ = mm_fwd(f"mlp_down{l}", act_b, G[f"w2_{l}"], colsharded=False, epilogue=_ep_resid(alpha, False),
                      outs=[F32], tiles=[xin], comm=gather_start(*down_keys) if down_keys else None)
        ((z,), delivered) = res_ if down_keys else (res_, [])
        gather_done(down_keys, delivered)
        if l == depth - 1:
            return act_b, r, z, None, None
        y, y_b = ln_fwd(f"ln_mlp{l}", z, row(ln_mlp_g, l), row(ln_mlp_b, l))
        return act_b, r, z, y, y_b

    act0_b, r0, z2, x2_, x2_b = mlp_fwd(0, x1, x1_b, up_keys=("w2_0",), down_keys=("wkv", "wq", "wo"))

    (kv,) = mm_fwd("kv_proj", x2_b, G["wkv"], colsharded=True, epilogue=_ep_store, outs=[F32])
    (q,) = mm_fwd("q_proj", x2_b, G["wq"], colsharded=False, epilogue=_ep_store, outs=[F32])
    tables = _band_tables(n_buckets)
    onehot = jnp.concatenate([(jnp.arange(n_buckets)[:, None] == jnp.asarray(bucket).reshape(1, -1)).astype(F32)
                              for bucket, _ in tables], axis=1)
    KB = onehot.shape[1] // 8
    (btab,) = _mm("rel_bias_table", rel_bias, onehot, grid=(1, 8, 1),
                  a_spec=pl.BlockSpec((n_buckets, H), lambda i, j, k: (0, 0)),
                  b_spec=pl.BlockSpec((n_buckets, KB), lambda i, j, k: (0, j)), dims=TN, k_axis=2,
                  epilogue=_ep_store, out_shapes=[jax.ShapeDtypeStruct((H, 8 * KB), F32)],
                  out_specs=[pl.BlockSpec((H, KB), lambda i, j, k: (0, j))], acc_shape=(H, KB), exact=True)
    band_ok = jnp.asarray(np.stack([ok for _, ok in tables]))[:, None]
    bias = jnp.where(band_ok, jnp.transpose(btab.reshape(H, 3, BAND, 2 * BAND), (1, 0, 2, 3)), NEG)
    has_prev = jnp.asarray(np.arange(2 * BAND) >= BAND)
    bias = jnp.concatenate([bias, jnp.where(has_prev, bias, NEG)], axis=0)
    (o, lse), delivered = attn_fwd("attn_fwd", q, kv, bias, comm=gather_start("w1_1", "w2_1"))
    gather_done(("w1_1", "w2_1"), delivered)
    (z3,) = mm_fwd("o_proj", o, G["wo"], colsharded=False, epilogue=_ep_resid(alpha, False), outs=[F32], tiles=[x2_])
    x3, x3_b = ln_fwd("ln_mix1", z3, row(ln_mix_g, 1), row(ln_mix_b, 1))
    act1_b, r1, z4, _, _ = mlp_fwd(1, x3, x3_b)

    loss_local, *last_norm_bwd = loss_ln_bwd("loss_ln_mlp1_bwd", z4, row(ln_mlp_g, 1), row(ln_mlp_b, 1), target)
    loss = lax.psum(loss_local[0, 0], MESH_AXES)

    parts = {}

    def pair_of(grads):
        g4 = [g.reshape((4, 2) + g.shape[1:]) for g in grads.values()]
        return g4, PairExchange(g4)

    def chip_of(grads, g4, from_sibling):
        return ChipExchange([pair_sum("grad_pair_sum_" + k, g, s, c_idx) for k, g, s in zip(grads, g4, from_sibling)])

    def reduce_done(grads, delivered):
        parts.update(zip(grads, delivered))

    def mlp_bwd(l, dy_out, z, xin_b, act_b, r, pending=None, norm_bwd=None):
        if norm_bwd is not None:
            dz, dz_b, dg, db = norm_bwd
            comm = None
        elif pending is None:
            dz, dz_b, dg, db, _ = ln_bwd(f"ln_mlp{l}_bwd", dy_out, z, row(ln_mlp_g, l))
            comm = None
        else:
            (dz, dz_b, dg, db, _), from_sibling = ln_bwd(f"ln_mlp{l}_bwd", dy_out, z, row(ln_mlp_g, l), comm=pending[2])
            comm = chip_of(pending[0], pending[1], from_sibling)
        res_ = mm_dx(f"mlp_down{l}_dx", dz_b, G[f"w2_{l}"], colsharded=False, epilogue=_ep_relu2_bwd,
                     outs=[MXU_DTYPE], tiles=[r], comm=comm)
        ((dhm_b,), delivered) = res_ if comm is not None else (res_, [])
        dw2 = mm_dw(f"mlp_down{l}_dw", act_b, dz_b, colsharded=False)
        (dxin,) = mm_dx(f"mlp_up{l}_dx", dhm_b, G[f"w1_{l}"], colsharded=True, epilogue=_ep_resid(alpha, False),
                        outs=[F32], tiles=[dz])
        dw1 = mm_dw(f"mlp_up{l}_dw", xin_b, dhm_b, colsharded=True)
        return dxin, dw1, dw2, dg, db, delivered

    dx3, dw1_1, dw2_1, dg_mlp1, db_mlp1, _ = mlp_bwd(1, None, z4, x3_b, act1_b, r1, norm_bwd=last_norm_bwd)
    grads1 = {"w1_1": dw1_1, "w2_1": dw2_1}
    g4_1, pair1 = pair_of(grads1)

    (dz3, dz3_b, dg_mix1, db_mix1, _), from_sibling = ln_bwd("ln_mix1_bwd", dx3, z3, row(ln_mix_g, 1), comm=pair1)
    comm1 = chip_of(grads1, g4_1, from_sibling)
    (do,) = mm_dx("o_proj_dx", dz3_b, G["wo"], colsharded=False, epilogue=_ep_store, outs=[F32])
    dwo = mm_dw("o_proj_dw", o, dz3_b, colsharded=False)
    (dq, dk, dv, dbias), delivered = attn_bwd("attn_bwd", q, kv, bias, o, lse, do, comm=comm1)
    reduce_done(grads1, delivered)
    dkv = jnp.concatenate([dk, dv], axis=1)
    (dx2a,) = mm_dx("q_proj_dx", dq, G["wq"], colsharded=False, epilogue=_ep_resid(alpha, False), outs=[F32], tiles=[dz3])
    (dx2,) = mm_dx("kv_proj_dx", dkv, G["wkv"], colsharded=True, epilogue=_ep_add, outs=[F32], tiles=[dx2a])
    dwq = mm_dw("q_proj_dw", x2_b, dq, colsharded=False)
    dwkv = mm_dw("kv_proj_dw", x2_b, dkv, colsharded=True)
    grads2 = {"wo": dwo, "wq": dwq, "wkv": dwkv}
    g4_2, pair2 = pair_of(grads2)

    dbias2 = jnp.transpose(dbias, (1, 0, 2, 3)).reshape(H, -1)
    (drel_t,) = _mm("rel_bias_grad", dbias2, onehot, grid=(1, 1, 8),
                    a_spec=pl.BlockSpec((H, KB), lambda i, j, k: (0, k)),
                    b_spec=pl.BlockSpec((n_buckets, KB), lambda i, j, k: (0, k)), dims=NT, k_axis=2,
                    epilogue=_ep_store, out_shapes=[jax.ShapeDtypeStruct((H, n_buckets), F32)],
                    out_specs=[pl.BlockSpec((H, n_buckets), lambda i, j, k: (0, 0))], acc_shape=(H, n_buckets), exact=True)

    dx1, dw1_0, dw2_0, dg_mlp0, db_mlp0, delivered = mlp_bwd(0, dx2, z2, x1_b, act0_b, r0, pending=(grads2, g4_2, pair2))
    reduce_done(grads2, delivered)

    dz1, dz1_b, dg_mix0, db_mix0, dpw2_b = ln_bwd("ln_mix0_bwd", dx1, z1, row(ln_mix_g, 0))
    (ds,) = mm_dx("pw2_dx", dz1_b, G["pw2"], colsharded=False, epilogue=_ep_store, outs=[F32])
    dwpw2 = mm_dw("pw2_dw", s_b, dz1_b, colsharded=False)
    grads3 = {"w1_0": dw1_0, "w2_0": dw2_0, "pw2": dwpw2}
    g4_3, pair3 = pair_of(grads3)
    (dc, dcln_g, dcln_b, ddw_b), from_sibling = conv_bwd_ln("conv_bwd_ln", ds, c_pre, cln_g, cln_b, comm=pair3)
    (dh1_b, ddw_w, dpw1_b), delivered = conv_bwd_dw("conv_bwd_dw", dc, h1, dw_w,
                                                   comm=chip_of(grads3, g4_3, from_sibling))
    reduce_done(grads3, delivered)
    grads4 = {"pw1": mm_dw("pw1_dw", x_b, dh1_b, colsharded=True)}
    g4_4, _ = pair_of(grads4)
    comm4 = chip_of(grads4, g4_4, pair_exchange("grad_pair_exchange_pw1", g4_4))
    (dx,), delivered = mm_dx("pw1_dx", dh1_b, G["pw1"], colsharded=True, epilogue=_ep_resid(alpha, False), outs=[F32],
                             tiles=[dz1], comm=comm4)
    reduce_done(grads4, delivered)
    grad_x = dx.reshape(1, T, D)

    vec_rows = [dg_mix0, dg_mix1, db_mix0, db_mix1, dg_mlp0, dg_mlp1, db_mlp0, db_mlp1,
                ddw_b, dcln_g, dcln_b, dpw2_b, dpw1_b.reshape(2, D), ddw_w[:taps],
                jnp.pad(jnp.transpose(drel_t).reshape(1, -1), ((0, 0), (0, D - H * n_buckets)))]
    pack = jnp.concatenate(vec_rows, axis=0)
    pack = jnp.pad(pack, ((0, (-pack.shape[0]) % 8), (0, 0)))
    tot = all_reduce_small("grad_small_all_reduce", pack)

    def mine(rows):
        return lax.dynamic_slice_in_dim(rows, me * ds_, ds_, axis=1)

    g_ln_mix_g, g_ln_mix_b, g_ln_mlp_g, g_ln_mlp_b = tot[0:2], tot[2:4], tot[4:6], tot[6:8]
    g_dw_b, g_cln_g, g_cln_b, g_pw2_b = (mine(tot[8 + k:9 + k]) for k in range(4))
    g_pw1_b = lax.dynamic_slice_in_dim(tot[12:14].reshape(1, 2 * D), me * 2 * ds_, 2 * ds_, axis=1)
    g_dw_w = mine(tot[14:14 + taps])
    g_rel = tot[14 + taps, :H * n_buckets].reshape(n_buckets, H)

    res = {}

    def upd(nm, parts_, w, m, v):
        shp = w.shape
        if not isinstance(parts_, list):
            parts_ = [parts_]
        parts_ = [p[None] if p.ndim == 2 else p for p in parts_]
        w3, m3, v3 = (a.reshape((len(parts_),) + parts_[0].shape[1:]) for a in (w, m, v))
        outs = adamw("adamw_" + nm, parts_, w3, m3, v3)
        res[nm] = tuple(o_.reshape(shp) for o_ in outs)

    upd("conv_pw1_w", parts["pw1"], conv_pw1_w, m_conv_pw1_w, v_conv_pw1_w)
    upd("conv_pw1_b", g_pw1_b, conv_pw1_b, m_conv_pw1_b, v_conv_pw1_b)
    upd("conv_dw_w", g_dw_w, conv_dw_w, m_conv_dw_w, v_conv_dw_w)
    upd("conv_dw_b", g_dw_b, conv_dw_b, m_conv_dw_b, v_conv_dw_b)
    upd("conv_ln_g", g_cln_g, conv_ln_g, m_conv_ln_g, v_conv_ln_g)
    upd("conv_ln_b", g_cln_b, conv_ln_b, m_conv_ln_b, v_conv_ln_b)
    upd("conv_pw2_w", parts["pw2"], conv_pw2_w, m_conv_pw2_w, v_conv_pw2_w)
    upd("conv_pw2_b", g_pw2_b, conv_pw2_b, m_conv_pw2_b, v_conv_pw2_b)
    upd("w_kv", parts["wkv"], w_kv, m_w_kv, v_w_kv)
    upd("attn_wq", parts["wq"], attn_wq, m_attn_wq, v_attn_wq)
    upd("attn_wo", parts["wo"], attn_wo, m_attn_wo, v_attn_wo)
    upd("rel_bias", g_rel, rel_bias, m_rel_bias, v_rel_bias)
    upd("mlp_w1", [parts["w1_0"], parts["w1_1"]], mlp_w1, m_mlp_w1, v_mlp_w1)
    upd("mlp_w2", [parts["w2_0"], parts["w2_1"]], mlp_w2, m_mlp_w2, v_mlp_w2)
    upd("ln_mix_g", g_ln_mix_g, ln_mix_g, m_ln_mix_g, v_ln_mix_g)
    upd("ln_mix_b", g_ln_mix_b, ln_mix_b, m_ln_mix_b, v_ln_mix_b)
    upd("ln_mlp_g", g_ln_mlp_g, ln_mlp_g, m_ln_mlp_g, v_ln_mlp_g)
    upd("ln_mlp_b", g_ln_mlp_b, ln_mlp_b, m_ln_mlp_b, v_ln_mlp_b)

    order = ["conv_pw1_w", "conv_pw1_b", "conv_dw_w", "conv_dw_b", "conv_ln_g", "conv_ln_b", "conv_pw2_w",
             "conv_pw2_b", "w_kv", "attn_wq", "attn_wo", "rel_bias", "mlp_w1", "mlp_w2", "ln_mix_g", "ln_mix_b",
             "ln_mlp_g", "ln_mlp_b"]
    return (loss, grad_x, *[res[n_][0] for n_ in order], *[res[n_][1] for n_ in order],
            *[res[n_][2] for n_ in order], *[res[n_][3] for n_ in order])
```

```python
import math

import numpy as np
import jax
import jax.numpy as jnp
from jax import lax
from jax.experimental import pallas as pl
from jax.experimental.pallas import tpu as pltpu

F32 = jnp.float32
MXU_DTYPE = jnp.bfloat16
GRAD_DTYPE = jnp.bfloat16
VMEM_LIMIT_BYTES = 56 * 2**20
LANE = 128
N_DEV = 8
MESH_AXES = ("x", "y", "c")
MESH = pl.DeviceIdType.MESH

HEAD_DIM = 128
BAND = 128
BRANCHES = ((128, 1), (512, 4), (2048, 16))
ATT_TB = BAND * 16
ATT_MERGE_ROWS = 256
REL_MAX_DIST = 2048
LN_EPS = 1e-5
NEG = -1e30
HALO = 32

ADAM_LR, ADAM_B1, ADAM_B2, ADAM_EPS, ADAM_WD, ADAM_STEP = 0.001, 0.9, 0.999, 1e-08, 0.01, 10

NN = (((1,), (0,)), ((), ()))
NT = (((1,), (1,)), ((), ()))
TN = (((0,), (0,)), ((), ()))


def _cp(sem=None):
    return pltpu.CompilerParams(dimension_semantics=sem, vmem_limit_bytes=VMEM_LIMIT_BYTES)


def _sig(v):
    return 1.0 / (1.0 + jnp.exp(-v))


def _call(body, *, name, grid, in_specs, out_specs, out_shape, args, sem, scratch_shapes=(), comm=None):
    if comm is None:
        res = pl.pallas_call(body, name=name, grid=grid, in_specs=list(in_specs), out_specs=list(out_specs),
                             out_shape=list(out_shape), scratch_shapes=list(scratch_shapes),
                             compiler_params=_cp(sem))(*args)
        return list(res), []
    n_in, n_out, n_sc, nc_in, nc_out = len(in_specs), len(out_specs), len(scratch_shapes), len(comm.arrays), len(comm.out_shapes)

    def wrapped(*refs):
        pos = 0
        parts = []
        for cnt in (n_in, nc_in, n_out, nc_out, n_sc):
            parts.append(refs[pos:pos + cnt])
            pos += cnt
        ins, cin, outs, cout, sc = parts
        csem = refs[pos:]
        step = pl.program_id(0)
        for ax in range(1, len(grid)):
            step = step * grid[ax] + pl.program_id(ax)
        n_steps = math.prod(grid)

        @pl.when(step == 0)
        def _():
            comm.start(cin, cout, csem)

        if comm.relay_at is not None:
            @pl.when(step == min(n_steps - 1, int(n_steps * comm.relay_at)))
            def _():
                comm.relay(cin, cout, csem)

        body(*ins, *outs, *sc)

        @pl.when(step == n_steps - 1)
        def _():
            comm.wait(cin, cout, csem)

    res = pl.pallas_call(wrapped, name=name, grid=grid, in_specs=[*in_specs, *[ANY] * nc_in],
                         out_specs=[*out_specs, *[ANY] * nc_out], out_shape=[*out_shape, *comm.out_shapes],
                         scratch_shapes=[*scratch_shapes, *comm.sems],
                         compiler_params=_cp(("arbitrary",) * len(grid)))(*args, *comm.arrays)
    return list(res[:n_out]), list(res[n_out:])


def _mm(name, a, b, *, grid, a_spec, b_spec, dims, k_axis, epilogue, out_shapes, out_specs,
        acc_shape, extra=(), extra_specs=(), exact=False, nsplit=1, ncat=1, comm=None):
    nk = grid[k_axis]
    n_extra, n_out = len(extra), len(out_shapes)

    def dot(av, bv):
        if exact:
            return lax.dot_general(av, bv, dims, precision=lax.Precision.HIGHEST, preferred_element_type=F32)
        return lax.dot_general(av.astype(MXU_DTYPE), bv.astype(MXU_DTYPE), dims, preferred_element_type=F32)

    def product(a_ref, b_ref):
        if ncat > 1:
            return jnp.concatenate([dot(a_ref[...], b_ref[g]) for g in range(ncat)], axis=1)
        if nsplit == 1:
            return dot(a_ref[...], b_ref[...])
        w = a_ref.shape[1] // nsplit
        part = dot(a_ref[:, 0:w], b_ref[0])
        for s in range(1, nsplit):
            part = part + dot(a_ref[:, s * w:(s + 1) * w], b_ref[s])
        return part

    def body(a_ref, b_ref, *rest):
        ex, outs = rest[:n_extra], rest[n_extra:n_extra + n_out]
        if nk == 1:
            epilogue(product(a_ref, b_ref), ex, outs)
        else:
            acc = rest[n_extra + n_out]
            k = pl.program_id(k_axis)

            @pl.when(k == 0)
            def _():
                acc[...] = product(a_ref, b_ref)

            if nk > 2:
                @pl.when(jnp.logical_and(k > 0, k < nk - 1))
                def _():
                    acc[...] += product(a_ref, b_ref)

            @pl.when(k == nk - 1)
            def _():
                epilogue(acc[...] + product(a_ref, b_ref), ex, outs)

    sem = tuple("arbitrary" if ax == k_axis else "parallel" for ax in range(len(grid)))
    outs, couts = _call(body, name=name, grid=grid, in_specs=[a_spec, b_spec, *extra_specs], out_specs=out_specs,
                        out_shape=out_shapes, args=(a, b, *extra), sem=sem,
                        scratch_shapes=[pltpu.VMEM(acc_shape, F32)] if nk > 1 else [], comm=comm)
    return outs if comm is None else (outs, couts)


def _blk(n, want):
    return min(n, want)


MM_BLOCK = 1024
MM_K = 2048


def mm_fwd(name, x, w, *, colsharded, epilogue, outs, rowvec=(), tiles=(), comm=None):
    T, K = x.shape
    bm = _blk(T, MM_BLOCK)
    ncat = 1
    if colsharded:
        n_s = w.shape[2]
        N = N_DEV * n_s
        bk = K
        if n_s < MM_BLOCK and (N_DEV * n_s) % MM_BLOCK == 0:
            ncat = MM_BLOCK // n_s
            bn = MM_BLOCK
            b_spec = pl.BlockSpec((ncat, K, n_s), lambda i, j, k: (j, 0, 0))
        else:
            bn = _blk(n_s, MM_BLOCK)
            per = n_s // bn
            b_spec = pl.BlockSpec((None, K, bn), lambda i, j, k: (j // per, 0, j % per))
    else:
        N = w.shape[2]
        w = w.reshape(K, N)
        bn = _blk(N, MM_BLOCK)
        bk = _blk(K, MM_K)
        b_spec = pl.BlockSpec((bk, bn), lambda i, j, k: (k, j))
    grid = (T // bm, N // bn, K // bk)
    a_spec = pl.BlockSpec((bm, bk), lambda i, j, k: (i, k))
    tile_spec = pl.BlockSpec((bm, bn), lambda i, j, k: (i, j))
    vec_spec = pl.BlockSpec((1, bn), lambda i, j, k: (0, j))
    return _mm(name, x, w, grid=grid, a_spec=a_spec, b_spec=b_spec, dims=NN, k_axis=2, epilogue=epilogue,
               out_shapes=[jax.ShapeDtypeStruct((T, N), dt) for dt in outs], out_specs=[tile_spec] * len(outs),
               acc_shape=(bm, bn), extra=(*rowvec, *tiles),
               extra_specs=[vec_spec] * len(rowvec) + [tile_spec] * len(tiles), ncat=ncat, comm=comm)


def mm_dx(name, dy, w, *, colsharded, epilogue, outs, tiles=(), comm=None):
    T, N = dy.shape
    bm = _blk(T, MM_BLOCK)
    if colsharded:
        K, n_s = w.shape[1], w.shape[2]
        bko = _blk(K, MM_BLOCK)
        spk = max(1, min(N_DEV, MM_K // n_s))
        grid = (T // bm, K // bko, N_DEV // spk)
        a_spec = pl.BlockSpec((bm, spk * n_s), lambda i, j, s: (i, s))
        b_spec = pl.BlockSpec((spk, bko, n_s), lambda i, j, s: (s, j, 0))
    else:
        K = w.shape[1] * N_DEV
        w = w.reshape(K, N)
        bko = _blk(K, MM_BLOCK)
        spk = 1
        grid = (T // bm, K // bko, 1)
        a_spec = pl.BlockSpec((bm, N), lambda i, j, s: (i, 0))
        b_spec = pl.BlockSpec((bko, N), lambda i, j, s: (j, 0))
    tile_spec = pl.BlockSpec((bm, bko), lambda i, j, s: (i, j))
    if spk == 1 and colsharded:
        b_spec = pl.BlockSpec((None, bko, n_s), lambda i, j, s: (s, j, 0))
    return _mm(name, dy, w, grid=grid, a_spec=a_spec, b_spec=b_spec, dims=NT, k_axis=2, epilogue=epilogue,
               out_shapes=[jax.ShapeDtypeStruct((T, K), dt) for dt in outs], out_specs=[tile_spec] * len(outs),
               acc_shape=(bm, bko), extra=tuple(tiles), extra_specs=[tile_spec] * len(tiles), nsplit=spk, comm=comm)


def mm_dw(name, x, dy, *, colsharded, comm=None):
    T, K = x.shape
    N = dy.shape[1]
    narrow = x.dtype.itemsize == 2 and dy.dtype.itemsize == 2
    bt = _blk(T, 2 * MM_K if narrow else MM_K)
    bmo = _blk(K, MM_BLOCK)
    epilogue = _ep_store
    if colsharded:
        n_s = N // N_DEV
        out_shape = jax.ShapeDtypeStruct((N_DEV, K, n_s), GRAD_DTYPE)
        if n_s < MM_BLOCK and N % MM_BLOCK == 0:
            group = MM_BLOCK // n_s
            bno = MM_BLOCK
            out_spec = pl.BlockSpec((group, bmo, n_s), lambda i, j, t: (j, i, 0))

            def epilogue(acc, ex, outs):
                for g in range(group):
                    outs[0][g] = acc[:, g * n_s:(g + 1) * n_s].astype(outs[0].dtype)
        else:
            bno = _blk(n_s, MM_BLOCK)
            per = n_s // bno
            out_spec = pl.BlockSpec((None, bmo, bno), lambda i, j, t: (j // per, i, j % per))
    else:
        bno = _blk(N, MM_BLOCK)
        out_shape = jax.ShapeDtypeStruct((K, N), GRAD_DTYPE)
        out_spec = pl.BlockSpec((bmo, bno), lambda i, j, t: (i, j))
    grid = (K // bmo, N // bno, T // bt)
    a_spec = pl.BlockSpec((bt, bmo), lambda i, j, t: (t, i))
    b_spec = pl.BlockSpec((bt, bno), lambda i, j, t: (t, j))
    res = _mm(name, x, dy, grid=grid, a_spec=a_spec, b_spec=b_spec, dims=TN, k_axis=2, epilogue=epilogue,
              out_shapes=[out_shape], out_specs=[out_spec], acc_shape=(bmo, bno), comm=comm)
    (out,), couts = res if comm is not None else (res, [])
    out = out if colsharded else out.reshape(N_DEV, K // N_DEV, N)
    return out if comm is None else (out, couts)


def cast_bf16(name, a, layer=None, comm=None):
    R, C = a.shape[-2:]
    br = _blk(R, 512)

    def body(a_ref, o_ref):
        o_ref[...] = a_ref[...].astype(MXU_DTYPE)

    spec = pl.BlockSpec((br, C), lambda i: (i, 0))
    in_spec = spec if layer is None else pl.BlockSpec((None, br, C), lambda i: (layer, i, 0))
    (out,), delivered = _call(body, name=name, grid=(R // br,), in_specs=[in_spec], out_specs=[spec],
                              out_shape=[jax.ShapeDtypeStruct((R, C), MXU_DTYPE)], args=(a,), sem=("parallel",),
                              comm=comm)
    return out if comm is None else (out, delivered)


def _ln_stats(z):
    mu = jnp.mean(z, axis=-1, keepdims=True)
    zc = z - mu
    var = jnp.mean(zc * zc, axis=-1, keepdims=True)
    return zc * lax.rsqrt(var + LN_EPS)


def ln_fwd(name, z, g, b):
    T, D = z.shape
    br = _blk(T, 512)

    def body(z_ref, g_ref, b_ref, y_ref, yb_ref):
        y = _ln_stats(z_ref[...]) * g_ref[...] + b_ref[...]
        y_ref[...] = y
        yb_ref[...] = y.astype(MXU_DTYPE)

    row = pl.BlockSpec((br, D), lambda i: (i, 0))
    vec = pl.BlockSpec((1, D), lambda i: (0, 0))
    return pl.pallas_call(body, name=name, grid=(T // br,), in_specs=[row, vec, vec], out_specs=[row, row],
                          out_shape=[jax.ShapeDtypeStruct((T, D), F32), jax.ShapeDtypeStruct((T, D), MXU_DTYPE)],
                          compiler_params=_cp(("parallel",)))(z, g, b)


def ln_bwd(name, dy, z, g, comm=None):
    T, D = z.shape
    br = _blk(T, 512)

    def body(dy_ref, z_ref, g_ref, dz_ref, dzb_ref, dg_ref, db_ref, ds_ref):
        i = pl.program_id(0)
        z = z_ref[...]
        dy = dy_ref[...]
        mu = jnp.mean(z, axis=-1, keepdims=True)
        zc = z - mu
        var = jnp.mean(zc * zc, axis=-1, keepdims=True)
        rstd = lax.rsqrt(var + LN_EPS)
        xhat = zc * rstd
        dxh = dy * g_ref[...]
        m1 = jnp.mean(dxh, axis=-1, keepdims=True)
        m2 = jnp.mean(dxh * xhat, axis=-1, keepdims=True)
        dz = rstd * (dxh - m1 - xhat * m2)
        dz_ref[...] = dz
        dzb_ref[...] = dz.astype(MXU_DTYPE)

        @pl.when(i == 0)
        def _():
            dg_ref[...] = jnp.zeros_like(dg_ref)
            db_ref[...] = jnp.zeros_like(db_ref)
            ds_ref[...] = jnp.zeros_like(ds_ref)

        dg_ref[...] += jnp.sum(dy * xhat, axis=0, keepdims=True)
        db_ref[...] += jnp.sum(dy, axis=0, keepdims=True)
        ds_ref[...] += jnp.sum(dz, axis=0, keepdims=True)

    row = pl.BlockSpec((br, D), lambda i: (i, 0))
    vec = pl.BlockSpec((1, D), lambda i: (0, 0))
    vshape = jax.ShapeDtypeStruct((1, D), F32)
    outs, delivered = _call(body, name=name, grid=(T // br,), in_specs=[row, row, vec],
                            out_specs=[row, row, vec, vec, vec],
                            out_shape=[jax.ShapeDtypeStruct((T, D), F32), jax.ShapeDtypeStruct((T, D), MXU_DTYPE),
                                       vshape, vshape, vshape],
                            args=(dy, z, g), sem=("arbitrary",), comm=comm)
    return outs if comm is None else (outs, delivered)


def loss_ln_bwd(name, z, g, b, target):
    T, D = z.shape
    br = _blk(T, 512)
    n = T // br

    def body(z_ref, g_ref, b_ref, t_ref, loss_ref, dz_ref, dzb_ref, dg_ref, db_ref, acc_ref):
        i = pl.program_id(0)
        z = z_ref[...]
        mu = jnp.mean(z, axis=-1, keepdims=True)
        zc = z - mu
        var = jnp.mean(zc * zc, axis=-1, keepdims=True)
        rstd = lax.rsqrt(var + LN_EPS)
        xhat = zc * rstd
        err = (xhat * g_ref[...] + b_ref[...]) - t_ref[...]
        dy = err * (1.0 / D)
        dxh = dy * g_ref[...]
        m1 = jnp.mean(dxh, axis=-1, keepdims=True)
        m2 = jnp.mean(dxh * xhat, axis=-1, keepdims=True)
        dz = rstd * (dxh - m1 - xhat * m2)
        dz_ref[...] = dz
        dzb_ref[...] = dz.astype(MXU_DTYPE)

        @pl.when(i == 0)
        def _():
            acc_ref[...] = jnp.zeros_like(acc_ref)
            dg_ref[...] = jnp.zeros_like(dg_ref)
            db_ref[...] = jnp.zeros_like(db_ref)

        acc_ref[...] += jnp.sum(err * err, axis=0, keepdims=True)
        dg_ref[...] += jnp.sum(dy * xhat, axis=0, keepdims=True)
        db_ref[...] += jnp.sum(dy, axis=0, keepdims=True)

        @pl.when(i == n - 1)
        def _():
            loss_ref[...] = (0.5 / D) * jnp.sum(acc_ref[...], axis=1, keepdims=True)

    row = pl.BlockSpec((br, D), lambda i: (i, 0))
    vec = pl.BlockSpec((1, D), lambda i: (0, 0))
    vshape = jax.ShapeDtypeStruct((1, D), F32)
    return pl.pallas_call(body, name=name, grid=(n,), in_specs=[row, vec, vec, row],
                          out_specs=[pl.BlockSpec((1, 1), lambda i: (0, 0)), row, row, vec, vec],
                          out_shape=[jax.ShapeDtypeStruct((1, 1), F32), jax.ShapeDtypeStruct((T, D), F32),
                                     jax.ShapeDtypeStruct((T, D), MXU_DTYPE), vshape, vshape],
                          scratch_shapes=[pltpu.VMEM((1, D), F32)],
                          compiler_params=_cp(("arbitrary",)))(z, g, b, target)


CONV_BT = 128
SUBLANES = 8


def _shifted(ref, cs, bt, offsets, stage):
    for r in range(SUBLANES):
        offs = [off for off in offsets if off % SUBLANES == r]
        if offs:
            rows = bt + max(offs) - r
            stage[r, 0:rows, :] = ref[pl.ds(r, rows), cs]
    for k, off in enumerate(offsets):
        r = off % SUBLANES
        yield k, stage[r, off - r:off - r + bt, :]


def conv_fwd(name, h1, dw_w, dw_b, ln_g, ln_b, comm=None):
    T, D2 = h1.shape
    D = D2 // 2
    W = dw_w.shape[0]
    taps = W - 1
    bt = _blk(T, CONV_BT)
    hb = bt // HALO
    u_off = [HALO - (taps - 1) + k for k in range(taps)]

    def body(h_ref, hp_ref, w_ref, b_ref, g_ref, be_ref, c_ref, s_ref, ux, stage):
        i = pl.program_id(0)
        up = hp_ref[:, :D] * _sig(hp_ref[:, D:])
        ux[0:HALO, :] = jnp.where(i == 0, 0.0, up)
        ux[HALO:HALO + bt, :] = h_ref[:, :D] * _sig(h_ref[:, D:])
        for cb in range(D // LANE):
            cs = slice(cb * LANE, (cb + 1) * LANE)
            acc = jnp.broadcast_to(b_ref[:, cs], (bt, LANE))
            for k, u_k in _shifted(ux, cs, bt, u_off, stage.at[cb % 2]):
                acc = acc + w_ref[k:k + 1, cs] * u_k
            c_ref[:, cs] = acc
        n = _ln_stats(c_ref[...]) * g_ref[...] + be_ref[...]
        s_ref[...] = (n * _sig(n)).astype(MXU_DTYPE)

    main = pl.BlockSpec((bt, D2), lambda i: (i, 0))
    prev = pl.BlockSpec((HALO, D2), lambda i: (jnp.maximum(i * hb - 1, 0), 0))
    wspec = pl.BlockSpec((W, D), lambda i: (0, 0))
    vec = pl.BlockSpec((1, D), lambda i: (0, 0))
    row = pl.BlockSpec((bt, D), lambda i: (i, 0))
    return _call(body, name=name, grid=(T // bt,), in_specs=[main, prev, wspec, vec, vec, vec], out_specs=[row, row],
                 out_shape=[jax.ShapeDtypeStruct((T, D), F32), jax.ShapeDtypeStruct((T, D), MXU_DTYPE)],
                 scratch_shapes=[pltpu.VMEM((HALO + bt, D), F32), pltpu.VMEM((2, SUBLANES, HALO + bt, LANE), F32)],
                 args=(h1, h1, dw_w, dw_b, ln_g, ln_b), sem=("parallel",), comm=comm)


def conv_bwd_ln(name, ds, c, ln_g, ln_b, comm=None):
    T, D = c.shape
    br = _blk(T, 512)

    def body(ds_ref, c_ref, g_ref, be_ref, dc_ref, dg_ref, db_ref, dcs_ref):
        i = pl.program_id(0)
        c = c_ref[...]
        mu = jnp.mean(c, axis=-1, keepdims=True)
        cc = c - mu
        var = jnp.mean(cc * cc, axis=-1, keepdims=True)
        rstd = lax.rsqrt(var + LN_EPS)
        xhat = cc * rstd
        n = xhat * g_ref[...] + be_ref[...]
        sg = _sig(n)
        dn = ds_ref[...] * (sg * (1.0 + n * (1.0 - sg)))
        dxh = dn * g_ref[...]
        m1 = jnp.mean(dxh, axis=-1, keepdims=True)
        m2 = jnp.mean(dxh * xhat, axis=-1, keepdims=True)
        dc = rstd * (dxh - m1 - xhat * m2)
        dc_ref[...] = dc

        @pl.when(i == 0)
        def _():
            dg_ref[...] = jnp.zeros_like(dg_ref)
            db_ref[...] = jnp.zeros_like(db_ref)
            dcs_ref[...] = jnp.zeros_like(dcs_ref)

        dg_ref[...] += jnp.sum(dn * xhat, axis=0, keepdims=True)
        db_ref[...] += jnp.sum(dn, axis=0, keepdims=True)
        dcs_ref[...] += jnp.sum(dc, axis=0, keepdims=True)

    row = pl.BlockSpec((br, D), lambda i: (i, 0))
    vec = pl.BlockSpec((1, D), lambda i: (0, 0))
    vshape = jax.ShapeDtypeStruct((1, D), F32)
    outs, delivered = _call(body, name=name, grid=(T // br,), in_specs=[row, row, vec, vec],
                            out_specs=[row, vec, vec, vec],
                            out_shape=[jax.ShapeDtypeStruct((T, D), F32), vshape, vshape, vshape],
                            args=(ds, c, ln_g, ln_b), sem=("arbitrary",), comm=comm)
    return outs if comm is None else (outs, delivered)


def conv_bwd_dw(name, dc, h1, dw_w, comm=None):
    T, D2 = h1.shape
    D = D2 // 2
    W = dw_w.shape[0]
    taps = W - 1
    bt = _blk(T, CONV_BT)
    hb = bt // HALO
    n = T // bt

    def body(dc_ref, dcn_ref, h_ref, hp_ref, w_ref, dh_ref, dw_ref, dhs_ref, ux, dcx, du, stage_dc, stage_u):
        i = pl.program_id(0)
        a = h_ref[:, :D]
        sg = _sig(h_ref[:, D:])
        up = hp_ref[:, :D] * _sig(hp_ref[:, D:])
        ux[0:HALO, :] = jnp.where(i == 0, 0.0, up)
        ux[HALO:HALO + bt, :] = a * sg
        dcx[0:bt, :] = dc_ref[...]
        dcx[bt:bt + HALO, :] = jnp.where(i == n - 1, 0.0, dcn_ref[...])

        @pl.when(i == 0)
        def _():
            dw_ref[...] = jnp.zeros_like(dw_ref)
            dhs_ref[...] = jnp.zeros_like(dhs_ref)

        for cb in range(D // LANE):
            cs = slice(cb * LANE, (cb + 1) * LANE)
            dcb = dcx[0:bt, cs]
            acc = jnp.zeros((bt, LANE), F32)
            for k, dc_k in _shifted(dcx, cs, bt, [taps - 1 - k for k in range(taps)], stage_dc.at[cb % 2]):
                acc = acc + w_ref[k:k + 1, cs] * dc_k
            du[:, cs] = acc
            for k, u_k in _shifted(ux, cs, bt, [HALO - (taps - 1) + k for k in range(taps)], stage_u.at[cb % 2]):
                dw_ref[k:k + 1, cs] += jnp.sum(dcb * u_k, axis=0, keepdims=True)
        d_u = du[...]
        da = d_u * sg
        dg = d_u * a * sg * (1.0 - sg)
        dh_ref[:, :D] = da.astype(MXU_DTYPE)
        dh_ref[:, D:] = dg.astype(MXU_DTYPE)
        dhs_ref[:, :D] += jnp.sum(da, axis=0, keepdims=True)
        dhs_ref[:, D:] += jnp.sum(dg, axis=0, keepdims=True)

    row = pl.BlockSpec((bt, D), lambda i: (i, 0))
    nxt = pl.BlockSpec((HALO, D), lambda i: (jnp.minimum((i + 1) * hb, T // HALO - 1), 0))
    main = pl.BlockSpec((bt, D2), lambda i: (i, 0))
    prev = pl.BlockSpec((HALO, D2), lambda i: (jnp.maximum(i * hb - 1, 0), 0))
    wspec = pl.BlockSpec((W, D), lambda i: (0, 0))
    return _call(body, name=name, grid=(n,), in_specs=[row, nxt, main, prev, wspec],
                 out_specs=[main, wspec, pl.BlockSpec((1, D2), lambda i: (0, 0))],
                 out_shape=[jax.ShapeDtypeStruct((T, D2), MXU_DTYPE), jax.ShapeDtypeStruct((W, D), F32),
                            jax.ShapeDtypeStruct((1, D2), F32)],
                 scratch_shapes=[pltpu.VMEM((HALO + bt, D), F32), pltpu.VMEM((bt + HALO, D), F32),
                                 pltpu.VMEM((bt, D), F32), pltpu.VMEM((2, SUBLANES, HALO + bt, LANE), F32),
                                 pltpu.VMEM((2, SUBLANES, HALO + bt, LANE), F32)],
                 args=(dc, dc, h1, h1, dw_w), sem=("arbitrary",), comm=comm)


def _t5_bucket(dist, n_buckets):
    max_exact = n_buckets // 2
    large = max_exact + (np.log(np.maximum(dist, 1) / max_exact) / math.log(REL_MAX_DIST / max_exact)
                         * (n_buckets - max_exact)).astype(np.int32)
    large = np.minimum(large, n_buckets - 1)
    return np.where(dist < max_exact, dist, large).astype(np.int32)


def _band_tables(n_buckets):
    i = np.arange(BAND)[:, None]
    j = np.arange(2 * BAND)[None, :]
    delta = i - j + BAND
    out = []
    for window, dil in BRANCHES:
        ok = (delta >= 0) & (delta <= window // dil)
        out.append((_t5_bucket(np.clip(delta, 0, None) * dil, n_buckets), ok))
    return out


def _units():
    for bi in (2, 1, 0):
        d = BRANCHES[bi][1]
        for r in range(d):
            for nb in range(ATT_TB // (BAND * d)):
                yield bi, d, r, nb


def _staggered(units, stages):
    n = len(stages)
    state = {}
    for t in range(len(units) + n - 1):
        for k in range(n):
            u = t - k
            if 0 <= u < len(units):
                state[u] = stages[k](units[u], state.get(u))
    return


def _rows(start, size, d):
    return pl.ds(start, size) if d == 1 else pl.ds(start, size, stride=d)


def _bc(v):
    return jnp.broadcast_to(v, (BAND, LANE))


def attn_fwd(name, q, kv, bias, comm=None):
    T, D = q.shape
    H = D // HEAD_DIM
    TB = ATT_TB
    scale = HEAD_DIM ** -0.5

    def body(q_ref, kc_ref, kp_ref, vc_ref, vp_ref, b_ref, o_ref, l_ref, kx, vx, m_sc, s_sc, a_sc):
        first = (pl.program_id(1) == 0).astype(jnp.int32)
        kx[0:TB, :] = kp_ref[...]
        kx[TB:2 * TB, :] = kc_ref[...]
        vx[0:TB, :] = vp_ref[...]
        vx[TB:2 * TB, :] = vc_ref[...]
        def scores(unit, _):
            bi, d, r, nb = unit
            qs = _rows(nb * BAND * d + r, BAND, d)
            ks = _rows(TB + (nb - 1) * BAND * d + r, 2 * BAND, d)
            qb = q_ref[qs, :].astype(MXU_DTYPE)
            kb = kx[ks, :].astype(MXU_DTYPE)
            vb = vx[ks, :].astype(MXU_DTYPE)
            b = b_ref[bi + 3 * first] if nb == 0 else b_ref[bi]
            return qs, vb, lax.dot_general(qb, kb, NT, preferred_element_type=F32) * scale + b

        def softmax(unit, st):
            qs, vb, s = st
            mrow = jnp.max(s, axis=1, keepdims=True)
            p = jnp.exp(s - mrow)
            m_sc.at[unit[0]][qs, :] = _bc(mrow)
            s_sc.at[unit[0]][qs, :] = _bc(jnp.sum(p, axis=1, keepdims=True))
            return qs, vb, p.astype(MXU_DTYPE)

        def values(unit, st):
            qs, vb, pb = st
            a_sc.at[unit[0]][qs, :] = lax.dot_general(pb, vb, NN, preferred_element_type=F32)

        _staggered(list(_units()), [scores, softmax, values])
        for c0 in range(0, TB, ATT_MERGE_ROWS):
            rows = slice(c0, c0 + ATT_MERGE_ROWS)
            ms = [m_sc[bi, rows, :] for bi in range(3)]
            m = jnp.maximum(jnp.maximum(ms[0], ms[1]), ms[2])
            ws = [jnp.exp(mi - m) for mi in ms]
            den = ws[0] * s_sc[0, rows, :] + ws[1] * s_sc[1, rows, :] + ws[2] * s_sc[2, rows, :]
            num = ws[0] * a_sc[0, rows, :] + ws[1] * a_sc[1, rows, :] + ws[2] * a_sc[2, rows, :]
            o_ref[rows, :] = num / den
            l_ref[rows, :] = m + jnp.log(den)

    cur = lambda off: pl.BlockSpec((TB, HEAD_DIM), lambda h, i: (i, off + h))
    prv = lambda off: pl.BlockSpec((TB, HEAD_DIM), lambda h, i: (jnp.maximum(i - 1, 0), off + h))
    bspec = pl.BlockSpec((6, None, BAND, 2 * BAND), lambda h, i: (0, h, 0, 0))
    sc = lambda *shape: pltpu.VMEM(shape + (HEAD_DIM,), F32)
    return _call(body, name=name, grid=(H, T // TB), in_specs=[cur(0), cur(0), prv(0), cur(H), prv(H), bspec],
                 out_specs=[cur(0), cur(0)],
                 out_shape=[jax.ShapeDtypeStruct((T, D), F32), jax.ShapeDtypeStruct((T, D), F32)],
                 scratch_shapes=[sc(2 * TB), sc(2 * TB), sc(3, TB), sc(3, TB), sc(3, TB)],
                 args=(q, kv, kv, kv, kv, bias), sem=("parallel", "parallel"), comm=comm)


def attn_bwd(name, q, kv, bias, o, lse, do, comm=None):
    T, D = q.shape
    H = D // HEAD_DIM
    TB = ATT_TB
    nI = T // TB
    scale = HEAD_DIM ** -0.5

    def body(q_ref, kc_ref, kp_ref, vc_ref, vp_ref, b_ref, o_ref, l_ref, do_ref,
             dq_ref, dk_ref, dv_ref, db_ref, kx, vx, dkx, dvx, ck, cv, dl_sc, dq_sc):
        step = pl.program_id(1)
        first = (step == nI - 1).astype(jnp.int32)
        kx[0:TB, :] = kp_ref[...]
        kx[TB:2 * TB, :] = kc_ref[...]
        vx[0:TB, :] = vp_ref[...]
        vx[TB:2 * TB, :] = vc_ref[...]
        dl_sc[...] = jnp.broadcast_to(jnp.sum(do_ref[...] * o_ref[...], axis=1, keepdims=True), (TB, LANE))

        @pl.when(step == 0)
        def _():
            db_ref[...] = jnp.zeros_like(db_ref)
            ck[...] = jnp.zeros_like(ck)
            cv[...] = jnp.zeros_like(cv)

        held = [None, None]

        def scores(unit, _):
            bi, d, r, nb = unit
            qs = _rows(nb * BAND * d + r, BAND, d)
            ks = _rows(TB + (nb - 1) * BAND * d + r, 2 * BAND, d)
            qb = q_ref[qs, :].astype(MXU_DTYPE)
            kb = kx[ks, :].astype(MXU_DTYPE)
            vb = vx[ks, :].astype(MXU_DTYPE)
            dob = do_ref[qs, :].astype(MXU_DTYPE)
            b = b_ref[bi + 3 * first] if nb == 0 else b_ref[bi]
            s = lax.dot_general(qb, kb, NT, preferred_element_type=F32) * scale + b
            dp = lax.dot_general(dob, vb, NT, preferred_element_type=F32)
            return qs, qb, kb, dob, s, dp

        def softmax_bwd(unit, st):
            qs, qb, kb, dob, s, dp = st
            p = jnp.exp(s - l_ref[qs, :][:, :1])
            dsv = p * (dp - dl_sc[qs, :][:, :1])
            db_ref[unit[0]] += dsv
            return qs, qb, kb, dob, p.astype(MXU_DTYPE), dsv.astype(MXU_DTYPE)

        def products(unit, st):
            bi, d, r, nb = unit
            qs, qb, kb, dob, pb, dsb = st
            band = lambda b: _rows(TB + b * BAND * d + r, BAND, d)
            dv_blk = lax.dot_general(pb, dob, TN, preferred_element_type=F32)
            dk_blk = lax.dot_general(dsb, qb, TN, preferred_element_type=F32) * scale
            dq_sc.at[bi][qs, :] = lax.dot_general(dsb, kb, NN, preferred_element_type=F32) * scale
            dkb, dvb = dkx.at[bi], dvx.at[bi]
            if nb == 0:
                dkb[band(-1), :] = dk_blk[:BAND]
                dvb[band(-1), :] = dv_blk[:BAND]
            else:
                dkb[band(nb - 1), :] = held[0] + dk_blk[:BAND]
                dvb[band(nb - 1), :] = held[1] + dv_blk[:BAND]
            held[0], held[1] = dk_blk[BAND:], dv_blk[BAND:]
            if nb == TB // (BAND * d) - 1:
                dkb[band(nb), :] = held[0]
                dvb[band(nb), :] = held[1]

        _staggered(list(_units()), [scores, softmax_bwd, products])

        dq_ref[...] = (dq_sc[0] + dq_sc[1] + dq_sc[2]).astype(dq_ref.dtype)
        for dx_ref, dxx, cx in ((dk_ref, dkx, ck), (dv_ref, dvx, cv)):
            dx_ref[...] = (dxx[0, TB:2 * TB, :] + dxx[1, TB:2 * TB, :] + dxx[2, TB:2 * TB, :] + cx[...]).astype(dx_ref.dtype)
            cx[...] = dxx[2, 0:TB, :]
            for bi in (1, 0):
                lo = TB - BAND * BRANCHES[bi][1]
                cx[lo:TB, :] += dxx[bi, lo:TB, :]

    blk = lambda h, i: nI - 1 - i
    cur = lambda off: pl.BlockSpec((TB, HEAD_DIM), lambda h, i: (blk(h, i), off + h))
    prv = lambda off: pl.BlockSpec((TB, HEAD_DIM), lambda h, i: (jnp.maximum(blk(h, i) - 1, 0), off + h))
    bspec = pl.BlockSpec((3, None, BAND, 2 * BAND), lambda h, i: (0, h, 0, 0))
    bspec_in = pl.BlockSpec((6, None, BAND, 2 * BAND), lambda h, i: (0, h, 0, 0))
    sc = lambda *shape: pltpu.VMEM(shape + (HEAD_DIM,), F32)
    return _call(body, name=name, grid=(H, nI),
                 in_specs=[cur(0), cur(0), prv(0), cur(H), prv(H), bspec_in, cur(0), cur(0), cur(0)],
                 out_specs=[cur(0), cur(0), cur(0), bspec],
                 out_shape=[jax.ShapeDtypeStruct((T, D), MXU_DTYPE), jax.ShapeDtypeStruct((T, D), MXU_DTYPE),
                            jax.ShapeDtypeStruct((T, D), MXU_DTYPE), jax.ShapeDtypeStruct((3, H, BAND, 2 * BAND), F32)],
                 scratch_shapes=[sc(2 * TB), sc(2 * TB), sc(3, 2 * TB), sc(3, 2 * TB), sc(TB), sc(TB), sc(TB), sc(3, TB)],
                 args=(q, kv, kv, kv, kv, bias, o, lse, do), sem=("arbitrary", "arbitrary"), comm=comm)


def adamw(name, parts, w, m, v):
    L, R, C = w.shape
    P = parts[0].shape[0]
    br = R if R % 8 else _blk(R, max(8, (1 << 18) // C))
    c1 = 1.0 / (1.0 - ADAM_B1 ** ADAM_STEP)
    c2 = 1.0 / (1.0 - ADAM_B2 ** ADAM_STEP)

    def body(*refs):
        p_refs = refs[:L]
        w_ref, m_ref, v_ref, g_ref, d_ref, nm_ref, nv_ref = refs[L:]
        lay = pl.program_id(0)

        def total(p_ref):
            g = p_ref[0].astype(F32)
            for k in range(1, P):
                g = g + p_ref[k].astype(F32)
            return g

        g = total(p_refs[0])
        for j in range(1, L):
            g = jnp.where(lay == j, total(p_refs[j]), g)
        nm = ADAM_B1 * m_ref[...] + (1.0 - ADAM_B1) * g
        nv = ADAM_B2 * v_ref[...] + (1.0 - ADAM_B2) * (g * g)
        g_ref[...] = g
        nm_ref[...] = nm
        nv_ref[...] = nv
        d_ref[...] = -ADAM_LR * ((nm * c1) / (jnp.sqrt(nv * c2) + ADAM_EPS) + ADAM_WD * w_ref[...])

    row = pl.BlockSpec((None, br, C), lambda l, i: (l, i, 0))
    pspecs = [pl.BlockSpec((P, br, C), lambda l, i, j=j: (0, jnp.where(l == j, i, 0), 0)) for j in range(L)]
    shp = jax.ShapeDtypeStruct((L, R, C), F32)
    return pl.pallas_call(body, name=name, grid=(L, R // br), in_specs=[*pspecs, row, row, row],
                          out_specs=[row] * 4, out_shape=[shp] * 4,
                          compiler_params=_cp(("parallel", "parallel")))(*parts, w, m, v)


def pair_sum(name, g, s1, c_idx):
    _, _, R, C = g.shape
    br = _blk(R, max(16, (1 << 19) // C))

    def body(c_ref, g_ref, s_ref, t_ref):
        t_ref[...] = (g_ref[...].astype(F32) + s_ref[...].astype(F32)).astype(t_ref.dtype)

    return pl.pallas_call(
        body, name=name,
        grid_spec=pltpu.PrefetchScalarGridSpec(
            num_scalar_prefetch=1, grid=(4, R // br),
            in_specs=[pl.BlockSpec((None, None, br, C), lambda j, i, c: (j, c[0], i, 0)),
                      pl.BlockSpec((None, br, C), lambda j, i, c: (j, i, 0))],
            out_specs=pl.BlockSpec((None, br, C), lambda j, i, c: (j, i, 0))),
        out_shape=jax.ShapeDtypeStruct((4, R, C), g.dtype), compiler_params=_cp(("parallel", "parallel")))(c_idx, g, s1)


def _me():
    return lax.axis_index("x"), lax.axis_index("y"), lax.axis_index("c")


ANY = pl.BlockSpec(memory_space=pl.ANY)


class Gather:
    relay_at = 0.7

    def __init__(self, shards):
        n = len(shards)
        self.arrays = list(shards)
        self.out_shapes = [jax.ShapeDtypeStruct((N_DEV,) + s.shape, s.dtype) for s in shards]
        self.sems = [pltpu.SemaphoreType.DMA((n, 7)), pltpu.SemaphoreType.DMA((n, 7)), pltpu.SemaphoreType.DMA((n,))]

    def _ctx(self, ins, outs, sems):
        send_sems, recv_sems, local_sems = sems
        x, y, c = _me()
        chips = [(1 - x, y), (x, 1 - y), (1 - x, 1 - y)]

        def copy(a, k, block, to, src=None):
            px, py, pc = block
            dst = outs[a].at[4 * px + 2 * py + pc]
            return pltpu.make_async_remote_copy(src_ref=dst if src is None else src, dst_ref=dst,
                                                send_sem=send_sems.at[a, k], recv_sem=recv_sems.at[a, k],
                                                device_id=to, device_id_type=MESH)

        def own(a):
            return pltpu.make_async_copy(ins[a], outs[a].at[4 * x + 2 * y + c], local_sems.at[a])

        def first(a):
            return [copy(a, 0, (x, y, c), (x, y, 1 - c), src=ins[a])] + \
                   [copy(a, 1 + j, (x, y, c), (*chip, c), src=ins[a]) for j, chip in enumerate(chips)]

        def passed(a):
            return [copy(a, 4 + j, (*chip, c), (x, y, 1 - c)) for j, chip in enumerate(chips)]

        return (x, y, c), chips, copy, own, first, passed

    def start(self, ins, outs, sems):
        _, _, _, own, first, _ = self._ctx(ins, outs, sems)
        for a in range(len(ins)):
            own(a).start()
            for cp in first(a):
                cp.start()

    def relay(self, ins, outs, sems):
        me, chips, copy, _, _, passed = self._ctx(ins, outs, sems)
        for a in range(len(ins)):
            fwd = passed(a)
            for j, chip in enumerate(chips):
                copy(a, 1 + j, (*chip, me[2]), me).wait_recv()
                fwd[j].start()

    def wait(self, ins, outs, sems):
        (x, y, c), chips, copy, own, first, passed = self._ctx(ins, outs, sems)
        for a in range(len(ins)):
            copy(a, 0, (x, y, 1 - c), (x, y, c)).wait_recv()
            for j, chip in enumerate(chips):
                copy(a, 4 + j, (*chip, 1 - c), (x, y, c)).wait_recv()
            for cp in first(a) + passed(a):
                cp.wait_send()
            own(a).wait()


class PairExchange:
    relay_at = None

    def __init__(self, grads):
        n = len(grads)
        self.arrays = list(grads)
        self.out_shapes = [jax.ShapeDtypeStruct((4,) + g.shape[2:], g.dtype) for g in grads]
        self.sems = [pltpu.SemaphoreType.DMA((n, 4)), pltpu.SemaphoreType.DMA((n, 4))]

    def _copies(self, ins, outs, sems):
        send_sems, recv_sems = sems
        x, y, c = _me()
        return [pltpu.make_async_remote_copy(
            src_ref=ins[a].at[j, 1 - c], dst_ref=outs[a].at[j], send_sem=send_sems.at[a, j],
            recv_sem=recv_sems.at[a, j], device_id=(x, y, 1 - c), device_id_type=MESH)
            for a in range(len(ins)) for j in range(4)]

    def start(self, ins, outs, sems):
        for cp in self._copies(ins, outs, sems):
            cp.start()

    def wait(self, ins, outs, sems):
        for cp in self._copies(ins, outs, sems):
            cp.wait()


def pair_exchange(name, grads):
    comm = PairExchange(grads)
    n = len(grads)

    def body(*refs):
        comm.start(refs[:n], refs[n:2 * n], refs[2 * n:])
        comm.wait(refs[:n], refs[n:2 * n], refs[2 * n:])

    return pl.pallas_call(body, name=name, in_specs=[ANY] * n, out_specs=[ANY] * n, out_shape=comm.out_shapes,
                          scratch_shapes=comm.sems)(*grads)


class ChipExchange:
    relay_at = None

    def __init__(self, sums):
        n = len(sums)
        self.arrays = list(sums)
        self.out_shapes = [jax.ShapeDtypeStruct(s.shape, s.dtype) for s in sums]
        self.sems = [pltpu.SemaphoreType.DMA((n, 3)), pltpu.SemaphoreType.DMA((n, 3)), pltpu.SemaphoreType.DMA((n,))]

    def _copies(self, ins, outs, sems, arrivals):
        send_sems, recv_sems, local_sems = sems
        x, y, c = _me()
        my = 2 * x + y
        chips = [(1 - x, y), (x, 1 - y), (1 - x, 1 - y)]
        mine, sends, recvs = [], [], []
        for a in range(len(ins)):
            mine.append(pltpu.make_async_copy(ins[a].at[my], outs[a].at[my], local_sems.at[a]))
            for j, (px, py) in enumerate(chips):
                sends.append(pltpu.make_async_remote_copy(
                    src_ref=ins[a].at[2 * px + py], dst_ref=outs[a].at[my],
                    send_sem=send_sems.at[a, j], recv_sem=recv_sems.at[a, j],
                    device_id=(px, py, c), device_id_type=MESH))
                if arrivals:
                    recvs.append(pltpu.make_async_remote_copy(
                        src_ref=ins[a].at[my], dst_ref=outs[a].at[2 * px + py],
                        send_sem=send_sems.at[a, j], recv_sem=recv_sems.at[a, j],
                        device_id=(px, py, c), device_id_type=MESH))
        return mine, sends, recvs

    def start(self, ins, outs, sems):
        mine, sends, _ = self._copies(ins, outs, sems, arrivals=False)
        for cp in mine + sends:
            cp.start()

    def wait(self, ins, outs, sems):
        mine, sends, recvs = self._copies(ins, outs, sems, arrivals=True)
        for cp in recvs:
            cp.wait_recv()
        for cp in sends:
            cp.wait_send()
        for cp in mine:
            cp.wait()


def all_reduce_small(name, pack):
    R, C = pack.shape

    def body(p_ref, o_ref, buf, send_sems, recv_sems):
        x, y, c = _me()
        me = 4 * x + 2 * y + c
        buf[me] = p_ref[...]
        copies = []
        for k in range(1, N_DEV):
            px, py, pc = x ^ (k >> 2), y ^ ((k >> 1) & 1), c ^ (k & 1)
            copies.append(pltpu.make_async_remote_copy(
                src_ref=p_ref, dst_ref=buf.at[me], send_sem=send_sems.at[k - 1], recv_sem=recv_sems.at[k - 1],
                device_id=(px, py, pc), device_id_type=MESH))
        for cp in copies:
            cp.start()
        for cp in copies:
            cp.wait()
        acc = buf[0]
        for d in range(1, N_DEV):
            acc = acc + buf[d]
        o_ref[...] = acc

    vm = pl.BlockSpec(memory_space=pltpu.VMEM)
    return pl.pallas_call(
        body, name=name, in_specs=[vm], out_specs=vm, out_shape=jax.ShapeDtypeStruct((R, C), F32),
        scratch_shapes=[pltpu.VMEM((N_DEV, R, C), F32), pltpu.SemaphoreType.DMA((N_DEV - 1,)),
                        pltpu.SemaphoreType.DMA((N_DEV - 1,))],
    )(pack)


def _ep_bias(acc, ex, outs):
    outs[0][...] = acc + ex[0][...]


def _ep_store(acc, ex, outs):
    outs[0][...] = acc.astype(outs[0].dtype)


def _ep_resid(alpha, bias):
    def ep(acc, ex, outs):
        if bias:
            outs[0][...] = alpha * ex[1][...] + (acc + ex[0][...])
        else:
            outs[0][...] = alpha * ex[0][...] + acc
    return ep


def _ep_relu2(acc, ex, outs):
    r = jnp.maximum(acc, 0.0)
    outs[0][...] = (r * r).astype(outs[0].dtype)
    outs[1][...] = r


def _ep_relu2_bwd(acc, ex, outs):
    outs[0][...] = (acc * (2.0 * ex[0][...])).astype(outs[0].dtype)


def _ep_add(acc, ex, outs):
    outs[0][...] = ex[0][...] + acc


def kernel(x, conv_pw1_w, conv_pw1_b, conv_dw_w, conv_dw_b, conv_ln_g, conv_ln_b, conv_pw2_w, conv_pw2_b, w_kv, attn_wq, attn_wo, rel_bias, mlp_w1, mlp_w2, ln_mix_g, ln_mix_b, ln_mlp_g, ln_mlp_b, loss_target, m_conv_pw1_w, m_conv_pw1_b, m_conv_dw_w, m_conv_dw_b, m_conv_ln_g, m_conv_ln_b, m_conv_pw2_w, m_conv_pw2_b, m_w_kv, m_attn_wq, m_attn_wo, m_rel_bias, m_mlp_w1, m_mlp_w2, m_ln_mix_g, m_ln_mix_b, m_ln_mlp_g, m_ln_mlp_b, v_conv_pw1_w, v_conv_pw1_b, v_conv_dw_w, v_conv_dw_b, v_conv_ln_g, v_conv_ln_b, v_conv_pw2_w, v_conv_pw2_b, v_w_kv, v_attn_wq, v_attn_wo, v_rel_bias, v_mlp_w1, v_mlp_w2, v_ln_mix_g, v_ln_mix_b, v_ln_mlp_g, v_ln_mlp_b):
    T, D = x.shape[1], x.shape[2]
    H = D // HEAD_DIM
    depth = mlp_w1.shape[0]
    assert depth == 2 and T % ATT_TB == 0
    alpha = (2 * depth) ** 0.25
    ds_ = D // N_DEV
    xi, yi, ci = _me()
    me = 4 * xi + 2 * yi + ci
    c_idx = ci.astype(jnp.int32).reshape(1)
    x2 = x.reshape(T, D)
    target = loss_target.reshape(T, D)
    n_buckets = rel_bias.shape[0]
    taps = conv_dw_w.shape[1]

    local = {
        "pw1": (conv_pw1_w, 0), "pw2": (conv_pw2_w, 0), "wkv": (w_kv, None), "wq": (attn_wq, 0), "wo": (attn_wo, 0),
        "w1_0": (mlp_w1, 0), "w1_1": (mlp_w1, 1), "w2_0": (mlp_w2, 0), "w2_1": (mlp_w2, 1),
    }
    names = list(local)
    small = jnp.concatenate([conv_dw_w[0], conv_dw_b, conv_ln_g, conv_ln_b, conv_pw2_b,
                             conv_pw1_b.reshape(2, ds_)], axis=0)
    small = jnp.pad(small, ((0, (-small.shape[0]) % 8), (0, 0)))
    shard = {k: cast_bf16("cast_" + k, *local[k]) for k in names}
    G = {}

    def gather_start(*keys):
        return Gather([shard[k] for k in keys])

    def gather_done(keys, delivered):
        G.update(zip(keys, delivered))

    x_b, (g_pw1, g_small) = cast_bf16("cast_x", x2, comm=Gather([shard["pw1"], small]))
    G["pw1"] = g_pw1
    sm = jnp.transpose(g_small, (1, 0, 2)).reshape(small.shape[0], D)
    dw_w = jnp.pad(sm[:taps], ((0, HALO - taps), (0, 0)))
    dw_b, cln_g, cln_b, pw2_b = (sm[taps + k:taps + k + 1] for k in range(4))
    pw1_b = g_small[:, taps + 4:taps + 6, :].reshape(1, 2 * D)

    row = lambda a, l: a[l:l + 1]

    (h1,) = mm_fwd("pw1", x_b, G["pw1"], colsharded=True, epilogue=_ep_bias, outs=[F32], rowvec=[pw1_b])
    (c_pre, s_b), delivered = conv_fwd("conv_fwd", h1, dw_w, dw_b, cln_g, cln_b, comm=gather_start("pw2", "w1_0"))
    gather_done(("pw2", "w1_0"), delivered)
    (z1,) = mm_fwd("pw2", s_b, G["pw2"], colsharded=False, epilogue=_ep_resid(alpha, True), outs=[F32],
                   rowvec=[pw2_b], tiles=[x2])
    x1, x1_b = ln_fwd("ln_mix0", z1, row(ln_mix_g, 0), row(ln_mix_b, 0))

    def mlp_fwd(l, xin, xin_b, up_keys=(), down_keys=()):
        res_ = mm_fwd(f"mlp_up{l}", xin_b, G[f"w1_{l}"], colsharded=True, epilogue=_ep_relu2, outs=[MXU_DTYPE, F32],
                      comm=gather_start(*up_keys) if up_keys else None)
        (act_b, r), delivered = res_ if up_keys else (res_, [])
        gather_done(up_keys, delivered)
        res_ = mm_fwd(f"mlp_down{l}", act_b, G[f"w2_{l}"], colsharded=False, epilogue=_ep_resid(alpha, False),
                      outs=[F32], tiles=[xin], comm=gather_start(*down_keys) if down_keys else None)
        ((z,), delivered) = res_ if down_keys else (res_, [])
        gather_done(down_keys, delivered)
        if l == depth - 1:
            return act_b, r, z, None, None
        y, y_b = ln_fwd(f"ln_mlp{l}", z, row(ln_mlp_g, l), row(ln_mlp_b, l))
        return act_b, r, z, y, y_b

    act0_b, r0, z2, x2_, x2_b = mlp_fwd(0, x1, x1_b, up_keys=("w2_0",), down_keys=("wkv", "wq", "wo"))

    (kv,) = mm_fwd("kv_proj", x2_b, G["wkv"], colsharded=True, epilogue=_ep_store, outs=[F32])
    (q,) = mm_fwd("q_proj", x2_b, G["wq"], colsharded=False, epilogue=_ep_store, outs=[F32])
    tables = _band_tables(n_buckets)
    onehot = jnp.concatenate([(jnp.arange(n_buckets)[:, None] == jnp.asarray(bucket).reshape(1, -1)).astype(F32)
                              for bucket, _ in tables], axis=1)
    KB = onehot.shape[1] // 8
    (btab,) = _mm("rel_bias_table", rel_bias, onehot, grid=(1, 8, 1),
                  a_spec=pl.BlockSpec((n_buckets, H), lambda i, j, k: (0, 0)),
                  b_spec=pl.BlockSpec((n_buckets, KB), lambda i, j, k: (0, j)), dims=TN, k_axis=2,
                  epilogue=_ep_store, out_shapes=[jax.ShapeDtypeStruct((H, 8 * KB), F32)],
                  out_specs=[pl.BlockSpec((H, KB), lambda i, j, k: (0, j))], acc_shape=(H, KB), exact=True)
    band_ok = jnp.asarray(np.stack([ok for _, ok in tables]))[:, None]
    bias = jnp.where(band_ok, jnp.transpose(btab.reshape(H, 3, BAND, 2 * BAND), (1, 0, 2, 3)), NEG)
    has_prev = jnp.asarray(np.arange(2 * BAND) >= BAND)
    bias = jnp.concatenate([bias, jnp.where(has_prev, bias, NEG)], axis=0)
    (o, lse), delivered = attn_fwd("attn_fwd", q, kv, bias, comm=gather_start("w1_1", "w2_1"))
    gather_done(("w1_1", "w2_1"), delivered)
    (z3,) = mm_fwd("o_proj", o, G["wo"], colsharded=False, epilogue=_ep_resid(alpha, False), outs=[F32], tiles=[x2_])
    x3, x3_b = ln_fwd("ln_mix1", z3, row(ln_mix_g, 1), row(ln_mix_b, 1))
    act1_b, r1, z4, _, _ = mlp_fwd(1, x3, x3_b)

    loss_local, *last_norm_bwd = loss_ln_bwd("loss_ln_mlp1_bwd", z4, row(ln_mlp_g, 1), row(ln_mlp_b, 1), target)
    loss = lax.psum(loss_local[0, 0], MESH_AXES)

    parts = {}

    def pair_of(grads):
        g4 = [g.reshape((4, 2) + g.shape[1:]) for g in grads.values()]
        return g4, PairExchange(g4)

    def chip_of(grads, g4, from_sibling):
        return ChipExchange([pair_sum("grad_pair_sum_" + k, g, s, c_idx) for k, g, s in zip(grads, g4, from_sibling)])

    def reduce_done(grads, delivered):
        parts.update(zip(grads, delivered))

    def mlp_bwd(l, dy_out, z, xin_b, act_b, r, pending=None, norm_bwd=None):
        if norm_bwd is not None:
            dz, dz_b, dg, db = norm_bwd
            comm = None
        elif pending is None:
            dz, dz_b, dg, db, _ = ln_bwd(f"ln_mlp{l}_bwd", dy_out, z, row(ln_mlp_g, l))
            comm = None
        else:
            (dz, dz_b, dg, db, _), from_sibling = ln_bwd(f"ln_mlp{l}_bwd", dy_out, z, row(ln_mlp_g, l), comm=pending[2])
            comm = chip_of(pending[0], pending[1], from_sibling)
        res_ = mm_dx(f"mlp_down{l}_dx", dz_b, G[f"w2_{l}"], colsharded=False, epilogue=_ep_relu2_bwd,
                     outs=[MXU_DTYPE], tiles=[r], comm=comm)
        ((dhm_b,), delivered) = res_ if comm is not None else (res_, [])
        dw2 = mm_dw(f"mlp_down{l}_dw", act_b, dz_b, colsharded=False)
        (dxin,) = mm_dx(f"mlp_up{l}_dx", dhm_b, G[f"w1_{l}"], colsharded=True, epilogue=_ep_resid(alpha, False),
                        outs=[F32], tiles=[dz])
        dw1 = mm_dw(f"mlp_up{l}_dw", xin_b, dhm_b, colsharded=True)
        return dxin, dw1, dw2, dg, db, delivered

    dx3, dw1_1, dw2_1, dg_mlp1, db_mlp1, _ = mlp_bwd(1, None, z4, x3_b, act1_b, r1, norm_bwd=last_norm_bwd)
    grads1 = {"w1_1": dw1_1, "w2_1": dw2_1}
    g4_1, pair1 = pair_of(grads1)

    (dz3, dz3_b, dg_mix1, db_mix1, _), from_sibling = ln_bwd("ln_mix1_bwd", dx3, z3, row(ln_mix_g, 1), comm=pair1)
    comm1 = chip_of(grads1, g4_1, from_sibling)
    (do,) = mm_dx("o_proj_dx", dz3_b, G["wo"], colsharded=False, epilogue=_ep_store, outs=[F32])
    dwo = mm_dw("o_proj_dw", o, dz3_b, colsharded=False)
    (dq, dk, dv, dbias), delivered = attn_bwd("attn_bwd", q, kv, bias, o, lse, do, comm=comm1)
    reduce_done(grads1, delivered)
    dkv = jnp.concatenate([dk, dv], axis=1)
    (dx2a,) = mm_dx("q_proj_dx", dq, G["wq"], colsharded=False, epilogue=_ep_resid(alpha, False), outs=[F32], tiles=[dz3])
    (dx2,) = mm_dx("kv_proj_dx", dkv, G["wkv"], colsharded=True, epilogue=_ep_add, outs=[F32], tiles=[dx2a])
    dwq = mm_dw("q_proj_dw", x2_b, dq, colsharded=False)
    dwkv = mm_dw("kv_proj_dw", x2_b, dkv, colsharded=True)
    grads2 = {"wo": dwo, "wq": dwq, "wkv": dwkv}
    g4_2, pair2 = pair_of(grads2)

    dbias2 = jnp.transpose(dbias, (1, 0, 2, 3)).reshape(H, -1)
    (drel_t,) = _mm("rel_bias_grad", dbias2, onehot, grid=(1, 1, 8),
                    a_spec=pl.BlockSpec((H, KB), lambda i, j, k: (0, k)),
                    b_spec=pl.BlockSpec((n_buckets, KB), lambda i, j, k: (0, k)), dims=NT, k_axis=2,
                    epilogue=_ep_store, out_shapes=[jax.ShapeDtypeStruct((H, n_buckets), F32)],
                    out_specs=[pl.BlockSpec((H, n_buckets), lambda i, j, k: (0, 0))], acc_shape=(H, n_buckets), exact=True)

    dx1, dw1_0, dw2_0, dg_mlp0, db_mlp0, delivered = mlp_bwd(0, dx2, z2, x1_b, act0_b, r0, pending=(grads2, g4_2, pair2))
    reduce_done(grads2, delivered)

    dz1, dz1_b, dg_mix0, db_mix0, dpw2_b = ln_bwd("ln_mix0_bwd", dx1, z1, row(ln_mix_g, 0))
    (ds,) = mm_dx("pw2_dx", dz1_b, G["pw2"], colsharded=False, epilogue=_ep_store, outs=[F32])
    dwpw2 = mm_dw("pw2_dw", s_b, dz1_b, colsharded=False)
    grads3 = {"w1_0": dw1_0, "w2_0": dw2_0, "pw2": dwpw2}
    g4_3, pair3 = pair_of(grads3)
    (dc, dcln_g, dcln_b, ddw_b), from_sibling = conv_bwd_ln("conv_bwd_ln", ds, c_pre, cln_g, cln_b, comm=pair3)
    (dh1_b, ddw_w, dpw1_b), delivered = conv_bwd_dw("conv_bwd_dw", dc, h1, dw_w,
                                                   comm=chip_of(grads3, g4_3, from_sibling))
    reduce_done(grads3, delivered)
    grads4 = {"pw1": mm_dw("pw1_dw", x_b, dh1_b, colsharded=True)}
    g4_4, _ = pair_of(grads4)
    comm4 = chip_of(grads4, g4_4, pair_exchange("grad_pair_exchange_pw1", g4_4))
    (dx,), delivered = mm_dx("pw1_dx", dh1_b, G["pw1"], colsharded=True, epilogue=_ep_resid(alpha, False), outs=[F32],
                             tiles=[dz1], comm=comm4)
    reduce_done(grads4, delivered)
    grad_x = dx.reshape(1, T, D)

    vec_rows = [dg_mix0, dg_mix1, db_mix0, db_mix1, dg_mlp0, dg_mlp1, db_mlp0, db_mlp1,
                ddw_b, dcln_g, dcln_b, dpw2_b, dpw1_b.reshape(2, D), ddw_w[:taps],
                jnp.pad(jnp.transpose(drel_t).reshape(1, -1), ((0, 0), (0, D - H * n_buckets)))]
    pack = jnp.concatenate(vec_rows, axis=0)
    pack = jnp.pad(pack, ((0, (-pack.shape[0]) % 8), (0, 0)))
    tot = all_reduce_small("grad_small_all_reduce", pack)

    def mine(rows):
        return lax.dynamic_slice_in_dim(rows, me * ds_, ds_, axis=1)

    g_ln_mix_g, g_ln_mix_b, g_ln_mlp_g, g_ln_mlp_b = tot[0:2], tot[2:4], tot[4:6], tot[6:8]
    g_dw_b, g_cln_g, g_cln_b, g_pw2_b = (mine(tot[8 + k:9 + k]) for k in range(4))
    g_pw1_b = lax.dynamic_slice_in_dim(tot[12:14].reshape(1, 2 * D), me * 2 * ds_, 2 * ds_, axis=1)
    g_dw_w = mine(tot[14:14 + taps])
    g_rel = tot[14 + taps, :H * n_buckets].reshape(n_buckets, H)

    res = {}

    def upd(nm, parts_, w, m, v):
        shp = w.shape
        if not isinstance(parts_, list):
            parts_ = [parts_]
        parts_ = [p[None] if p.ndim == 2 else p for p in parts_]
        w3, m3, v3 = (a.reshape((len(parts_),) + parts_[0].shape[1:]) for a in (w, m, v))
        outs = adamw("adamw_" + nm, parts_, w3, m3, v3)
        res[nm] = tuple(o_.reshape(shp) for o_ in outs)

    upd("conv_pw1_w", parts["pw1"], conv_pw1_w, m_conv_pw1_w, v_conv_pw1_w)
    upd("conv_pw1_b", g_pw1_b, conv_pw1_b, m_conv_pw1_b, v_conv_pw1_b)
    upd("conv_dw_w", g_dw_w, conv_dw_w, m_conv_dw_w, v_conv_dw_w)
    upd("conv_dw_b", g_dw_b, conv_dw_b, m_conv_dw_b, v_conv_dw_b)
    upd("conv_ln_g", g_cln_g, conv_ln_g, m_conv_ln_g, v_conv_ln_g)
    upd("conv_ln_b", g_cln_b, conv_ln_b, m_conv_ln_b, v_conv_ln_b)
    upd("conv_pw2_w", parts["pw2"], conv_pw2_w, m_conv_pw2_w, v_conv_pw2_w)
    upd("conv_pw2_b", g_pw2_b, conv_pw2_b, m_conv_pw2_b, v_conv_pw2_b)
    upd("w_kv", parts["wkv"], w_kv, m_w_kv, v_w_kv)
    upd("attn_wq", parts["wq"], attn_wq, m_attn_wq, v_attn_wq)
    upd("attn_wo", parts["wo"], attn_wo, m_attn_wo, v_attn_wo)
    upd("rel_bias", g_rel, rel_bias, m_rel_bias, v_rel_bias)
    upd("mlp_w1", [parts["w1_0"], parts["w1_1"]], mlp_w1, m_mlp_w1, v_mlp_w1)
    upd("mlp_w2", [parts["w2_0"], parts["w2_1"]], mlp_w2, m_mlp_w2, v_mlp_w2)
    upd("ln_mix_g", g_ln_mix_g, ln_mix_g, m_ln_mix_g, v_ln_mix_g)
    upd("ln_mix_b", g_ln_mix_b, ln_mix_b, m_ln_mix_b, v_ln_mix_b)
    upd("ln_mlp_g", g_ln_mlp_g, ln_mlp_g, m_ln_mlp_g, v_ln_mlp_g)
    upd("ln_mlp_b", g_ln_mlp_b, ln_mlp_b, m_ln_mlp_b, v_ln_mlp_b)

    order = ["conv_pw1_w", "conv_pw1_b", "conv_dw_w", "conv_dw_b", "conv_ln_g", "conv_ln_b", "conv_pw2_w",
             "conv_pw2_b", "w_kv", "attn_wq", "attn_wo", "rel_bias", "mlp_w1", "mlp_w2", "ln_mix_g", "ln_mix_b",
             "ln_mlp_g", "ln_mlp_b"]
    return (loss, grad_x, *[res[n_][0] for n_ in order], *[res[n_][1] for n_ in order],
            *[res[n_][2] for n_ in order], *[res[n_][3] for n_ in order])
```

```python
import math

import numpy as np
import jax
import jax.numpy as jnp
from jax import lax
from jax.experimental import pallas as pl
from jax.experimental.pallas import tpu as pltpu

F32 = jnp.float32
MXU_DTYPE = jnp.bfloat16
GRAD_DTYPE = jnp.bfloat16
VMEM_LIMIT_BYTES = 56 * 2**20
LANE = 128
N_DEV = 8
MESH_AXES = ("x", "y", "c")
MESH = pl.DeviceIdType.MESH

HEAD_DIM = 128
BAND = 128
BRANCHES = ((128, 1), (512, 4), (2048, 16))
ATT_TB = BAND * 16
ATT_MERGE_ROWS = 256
REL_MAX_DIST = 2048
LN_EPS = 1e-5
NEG = -1e30
HALO = 32

ADAM_LR, ADAM_B1, ADAM_B2, ADAM_EPS, ADAM_WD, ADAM_STEP = 0.001, 0.9, 0.999, 1e-08, 0.01, 10

NN = (((1,), (0,)), ((), ()))
NT = (((1,), (1,)), ((), ()))
TN = (((0,), (0,)), ((), ()))


def _cp(sem=None):
    return pltpu.CompilerParams(dimension_semantics=sem, vmem_limit_bytes=VMEM_LIMIT_BYTES)


def _sig(v):
    return 1.0 / (1.0 + jnp.exp(-v))


def _call(body, *, name, grid, in_specs, out_specs, out_shape, args, sem, scratch_shapes=(), comm=None):
    if comm is None:
        res = pl.pallas_call(body, name=name, grid=grid, in_specs=list(in_specs), out_specs=list(out_specs),
                             out_shape=list(out_shape), scratch_shapes=list(scratch_shapes),
                             compiler_params=_cp(sem))(*args)
        return list(res), []
    n_in, n_out, n_sc, nc_in, nc_out = len(in_specs), len(out_specs), len(scratch_shapes), len(comm.arrays), len(comm.out_shapes)

    def wrapped(*refs):
        pos = 0
        parts = []
        for cnt in (n_in, nc_in, n_out, nc_out, n_sc):
            parts.append(refs[pos:pos + cnt])
            pos += cnt
        ins, cin, outs, cout, sc = parts
        csem = refs[pos:]
        step = pl.program_id(0)
        for ax in range(1, len(grid)):
            step = step * grid[ax] + pl.program_id(ax)
        n_steps = math.prod(grid)

        @pl.when(step == 0)
        def _():
            comm.start(cin, cout, csem)

        if comm.relay_at is not None:
            @pl.when(step == min(n_steps - 1, int(n_steps * comm.relay_at)))
            def _():
                comm.relay(cin, cout, csem)

        body(*ins, *outs, *sc)

        @pl.when(step == n_steps - 1)
        def _():
            comm.wait(cin, cout, csem)

    res = pl.pallas_call(wrapped, name=name, grid=grid, in_specs=[*in_specs, *[ANY] * nc_in],
                         out_specs=[*out_specs, *[ANY] * nc_out], out_shape=[*out_shape, *comm.out_shapes],
                         scratch_shapes=[*scratch_shapes, *comm.sems],
                         compiler_params=_cp(("arbitrary",) * len(grid)))(*args, *comm.arrays)
    return list(res[:n_out]), list(res[n_out:])


def _mm(name, a, b, *, grid, a_spec, b_spec, dims, k_axis, epilogue, out_shapes, out_specs,
        acc_shape, extra=(), extra_specs=(), exact=False, nsplit=1, ncat=1, comm=None):
    nk = grid[k_axis]
    n_extra, n_out = len(extra), len(out_shapes)

    def dot(av, bv):
        if exact:
            return lax.dot_general(av, bv, dims, precision=lax.Precision.HIGHEST, preferred_element_type=F32)
        return lax.dot_general(av.astype(MXU_DTYPE), bv.astype(MXU_DTYPE), dims, preferred_element_type=F32)

    def product(a_ref, b_ref):
        if ncat > 1:
            return jnp.concatenate([dot(a_ref[...], b_ref[g]) for g in range(ncat)], axis=1)
        if nsplit == 1:
            return dot(a_ref[...], b_ref[...])
        w = a_ref.shape[1] // nsplit
        part = dot(a_ref[:, 0:w], b_ref[0])
        for s in range(1, nsplit):
            part = part + dot(a_ref[:, s * w:(s + 1) * w], b_ref[s])
        return part

    def body(a_ref, b_ref, *rest):
        ex, outs = rest[:n_extra], rest[n_extra:n_extra + n_out]
        if nk == 1:
            epilogue(product(a_ref, b_ref), ex, outs)
        else:
            acc = rest[n_extra + n_out]
            k = pl.program_id(k_axis)

            @pl.when(k == 0)
            def _():
                acc[...] = product(a_ref, b_ref)

            if nk > 2:
                @pl.when(jnp.logical_and(k > 0, k < nk - 1))
                def _():
                    acc[...] += product(a_ref, b_ref)

            @pl.when(k == nk - 1)
            def _():
                epilogue(acc[...] + product(a_ref, b_ref), ex, outs)

    sem = tuple("arbitrary" if ax == k_axis else "parallel" for ax in range(len(grid)))
    outs, couts = _call(body, name=name, grid=grid, in_specs=[a_spec, b_spec, *extra_specs], out_specs=out_specs,
                        out_shape=out_shapes, args=(a, b, *extra), sem=sem,
                        scratch_shapes=[pltpu.VMEM(acc_shape, F32)] if nk > 1 else [], comm=comm)
    return outs if comm is None else (outs, couts)


def _blk(n, want):
    return min(n, want)


MM_BLOCK = 1024
MM_K = 2048


def mm_fwd(name, x, w, *, colsharded, epilogue, outs, rowvec=(), tiles=(), comm=None):
    T, K = x.shape
    bm = _blk(T, MM_BLOCK)
    ncat = 1
    if colsharded:
        n_s = w.shape[2]
        N = N_DEV * n_s
        bk = K
        if n_s < MM_BLOCK and (N_DEV * n_s) % MM_BLOCK == 0:
            ncat = MM_BLOCK // n_s
            bn = MM_BLOCK
            b_spec = pl.BlockSpec((ncat, K, n_s), lambda i, j, k: (j, 0, 0))
        else:
            bn = _blk(n_s, MM_BLOCK)
            per = n_s // bn
            b_spec = pl.BlockSpec((None, K, bn), lambda i, j, k: (j // per, 0, j % per))
    else:
        N = w.shape[2]
        w = w.reshape(K, N)
        bn = _blk(N, MM_BLOCK)
        bk = _blk(K, MM_K)
        b_spec = pl.BlockSpec((bk, bn), lambda i, j, k: (k, j))
    grid = (T // bm, N // bn, K // bk)
    a_spec = pl.BlockSpec((bm, bk), lambda i, j, k: (i, k))
    tile_spec = pl.BlockSpec((bm, bn), lambda i, j, k: (i, j))
    vec_spec = pl.BlockSpec((1, bn), lambda i, j, k: (0, j))
    return _mm(name, x, w, grid=grid, a_spec=a_spec, b_spec=b_spec, dims=NN, k_axis=2, epilogue=epilogue,
               out_shapes=[jax.ShapeDtypeStruct((T, N), dt) for dt in outs], out_specs=[tile_spec] * len(outs),
               acc_shape=(bm, bn), extra=(*rowvec, *tiles),
               extra_specs=[vec_spec] * len(rowvec) + [tile_spec] * len(tiles), ncat=ncat, comm=comm)


def mm_dx(name, dy, w, *, colsharded, epilogue, outs, tiles=(), comm=None):
    T, N = dy.shape
    bm = _blk(T, MM_BLOCK)
    if colsharded:
        K, n_s = w.shape[1], w.shape[2]
        bko = _blk(K, MM_BLOCK)
        spk = max(1, min(N_DEV, MM_K // n_s))
        grid = (T // bm, K // bko, N_DEV // spk)
        a_spec = pl.BlockSpec((bm, spk * n_s), lambda i, j, s: (i, s))
        b_spec = pl.BlockSpec((spk, bko, n_s), lambda i, j, s: (s, j, 0))
    else:
        K = w.shape[1] * N_DEV
        w = w.reshape(K, N)
        bko = _blk(K, MM_BLOCK)
        spk = 1
        grid = (T // bm, K // bko, 1)
        a_spec = pl.BlockSpec((bm, N), lambda i, j, s: (i, 0))
        b_spec = pl.BlockSpec((bko, N), lambda i, j, s: (j, 0))
    tile_spec = pl.BlockSpec((bm, bko), lambda i, j, s: (i, j))
    if spk == 1 and colsharded:
        b_spec = pl.BlockSpec((None, bko, n_s), lambda i, j, s: (s, j, 0))
    return _mm(name, dy, w, grid=grid, a_spec=a_spec, b_spec=b_spec, dims=NT, k_axis=2, epilogue=epilogue,
               out_shapes=[jax.ShapeDtypeStruct((T, K), dt) for dt in outs], out_specs=[tile_spec] * len(outs),
               acc_shape=(bm, bko), extra=tuple(tiles), extra_specs=[tile_spec] * len(tiles), nsplit=spk, comm=comm)


def mm_dw(name, x, dy, *, colsharded, comm=None):
    T, K = x.shape
    N = dy.shape[1]
    narrow = x.dtype.itemsize == 2 and dy.dtype.itemsize == 2
    bt = _blk(T, 2 * MM_K if narrow else MM_K)
    bmo = _blk(K, MM_BLOCK)
    epilogue = _ep_store
    if colsharded:
        n_s = N // N_DEV
        out_shape = jax.ShapeDtypeStruct((N_DEV, K, n_s), GRAD_DTYPE)
        if n_s < MM_BLOCK and N % MM_BLOCK == 0:
            group = MM_BLOCK // n_s
            bno = MM_BLOCK
            out_spec = pl.BlockSpec((group, bmo, n_s), lambda i, j, t: (j, i, 0))

            def epilogue(acc, ex, outs):
                for g in range(group):
                    outs[0][g] = acc[:, g * n_s:(g + 1) * n_s].astype(outs[0].dtype)
        else:
            bno = _blk(n_s, MM_BLOCK)
            per = n_s // bno
            out_spec = pl.BlockSpec((None, bmo, bno), lambda i, j, t: (j // per, i, j % per))
    else:
        bno = _blk(N, MM_BLOCK)
        out_shape = jax.ShapeDtypeStruct((K, N), GRAD_DTYPE)
        out_spec = pl.BlockSpec((bmo, bno), lambda i, j, t: (i, j))
    grid = (K // bmo, N // bno, T // bt)
    a_spec = pl.BlockSpec((bt, bmo), lambda i, j, t: (t, i))
    b_spec = pl.BlockSpec((bt, bno), lambda i, j, t: (t, j))
    res = _mm(name, x, dy, grid=grid, a_spec=a_spec, b_spec=b_spec, dims=TN, k_axis=2, epilogue=epilogue,
              out_shapes=[out_shape], out_specs=[out_spec], acc_shape=(bmo, bno), comm=comm)
    (out,), couts = res if comm is not None else (res, [])
    out = out if colsharded else out.reshape(N_DEV, K // N_DEV, N)
    return out if comm is None else (out, couts)


def cast_bf16(name, a, layer=None, comm=None):
    R, C = a.shape[-2:]
    br = _blk(R, 512)

    def body(a_ref, o_ref):
        o_ref[...] = a_ref[...].astype(MXU_DTYPE)

    spec = pl.BlockSpec((br, C), lambda i: (i, 0))
    in_spec = spec if layer is None else pl.BlockSpec((None, br, C), lambda i: (layer, i, 0))
    (out,), delivered = _call(body, name=name, grid=(R // br,), in_specs=[in_spec], out_specs=[spec],
                              out_shape=[jax.ShapeDtypeStruct((R, C), MXU_DTYPE)], args=(a,), sem=("parallel",),
                              comm=comm)
    return out if comm is None else (out, delivered)


def _ln_stats(z):
    mu = jnp.mean(z, axis=-1, keepdims=True)
    zc = z - mu
    var = jnp.mean(zc * zc, axis=-1, keepdims=True)
    return zc * lax.rsqrt(var + LN_EPS)


def ln_fwd(name, z, g, b):
    T, D = z.shape
    br = _blk(T, 512)

    def body(z_ref, g_ref, b_ref, y_ref, yb_ref):
        y = _ln_stats(z_ref[...]) * g_ref[...] + b_ref[...]
        y_ref[...] = y
        yb_ref[...] = y.astype(MXU_DTYPE)

    row = pl.BlockSpec((br, D), lambda i: (i, 0))
    vec = pl.BlockSpec((1, D), lambda i: (0, 0))
    return pl.pallas_call(body, name=name, grid=(T // br,), in_specs=[row, vec, vec], out_specs=[row, row],
                          out_shape=[jax.ShapeDtypeStruct((T, D), F32), jax.ShapeDtypeStruct((T, D), MXU_DTYPE)],
                          compiler_params=_cp(("parallel",)))(z, g, b)


def ln_bwd(name, dy, z, g, comm=None):
    T, D = z.shape
    br = _blk(T, 512)

    def body(dy_ref, z_ref, g_ref, dz_ref, dzb_ref, dg_ref, db_ref, ds_ref):
        i = pl.program_id(0)
        z = z_ref[...]
        dy = dy_ref[...]
        mu = jnp.mean(z, axis=-1, keepdims=True)
        zc = z - mu
        var = jnp.mean(zc * zc, axis=-1, keepdims=True)
        rstd = lax.rsqrt(var + LN_EPS)
        xhat = zc * rstd
        dxh = dy * g_ref[...]
        m1 = jnp.mean(dxh, axis=-1, keepdims=True)
        m2 = jnp.mean(dxh * xhat, axis=-1, keepdims=True)
        dz = rstd * (dxh - m1 - xhat * m2)
        dz_ref[...] = dz
        dzb_ref[...] = dz.astype(MXU_DTYPE)

        @pl.when(i == 0)
        def _():
            dg_ref[...] = jnp.zeros_like(dg_ref)
            db_ref[...] = jnp.zeros_like(db_ref)
            ds_ref[...] = jnp.zeros_like(ds_ref)

        dg_ref[...] += jnp.sum(dy * xhat, axis=0, keepdims=True)
        db_ref[...] += jnp.sum(dy, axis=0, keepdims=True)
        ds_ref[...] += jnp.sum(dz, axis=0, keepdims=True)

    row = pl.BlockSpec((br, D), lambda i: (i, 0))
    vec = pl.BlockSpec((1, D), lambda i: (0, 0))
    vshape = jax.ShapeDtypeStruct((1, D), F32)
    outs, delivered = _call(body, name=name, grid=(T // br,), in_specs=[row, row, vec],
                            out_specs=[row, row, vec, vec, vec],
                            out_shape=[jax.ShapeDtypeStruct((T, D), F32), jax.ShapeDtypeStruct((T, D), MXU_DTYPE),
                                       vshape, vshape, vshape],
                            args=(dy, z, g), sem=("arbitrary",), comm=comm)
    return outs if comm is None else (outs, delivered)


def loss_ln_bwd(name, z, g, b, target):
    T, D = z.shape
    br = _blk(T, 512)
    n = T // br

    def body(z_ref, g_ref, b_ref, t_ref, loss_ref, dz_ref, dzb_ref, dg_ref, db_ref, acc_ref):
        i = pl.program_id(0)
        z = z_ref[...]
        mu = jnp.mean(z, axis=-1, keepdims=True)
        zc = z - mu
        var = jnp.mean(zc * zc, axis=-1, keepdims=True)
        rstd = lax.rsqrt(var + LN_EPS)
        xhat = zc * rstd
        err = (xhat * g_ref[...] + b_ref[...]) - t_ref[...]
        dy = err * (1.0 / D)
        dxh = dy * g_ref[...]
        m1 = jnp.mean(dxh, axis=-1, keepdims=True)
        m2 = jnp.mean(dxh * xhat, axis=-1, keepdims=True)
        dz = rstd * (dxh - m1 - xhat * m2)
        dz_ref[...] = dz
        dzb_ref[...] = dz.astype(MXU_DTYPE)

        @pl.when(i == 0)
        def _():
            acc_ref[...] = jnp.zeros_like(acc_ref)
            dg_ref[...] = jnp.zeros_like(dg_ref)
            db_ref[...] = jnp.zeros_like(db_ref)

        acc_ref[...] += jnp.sum(err * err, axis=0, keepdims=True)
        dg_ref[...] += jnp.sum(dy * xhat, axis=0, keepdims=True)
        db_ref[...] += jnp.sum(dy, axis=0, keepdims=True)

        @pl.when(i == n - 1)
        def _():
            loss_ref[...] = (0.5 / D) * jnp.sum(acc_ref[...], axis=1, keepdims=True)

    row = pl.BlockSpec((br, D), lambda i: (i, 0))
    vec = pl.BlockSpec((1, D), lambda i: (0, 0))
    vshape = jax.ShapeDtypeStruct((1, D), F32)
    return pl.pallas_call(body, name=name, grid=(n,), in_specs=[row, vec, vec, row],
                          out_specs=[pl.BlockSpec((1, 1), lambda i: (0, 0)), row, row, vec, vec],
                          out_shape=[jax.ShapeDtypeStruct((1, 1), F32), jax.ShapeDtypeStruct((T, D), F32),
                                     jax.ShapeDtypeStruct((T, D), MXU_DTYPE), vshape, vshape],
                          scratch_shapes=[pltpu.VMEM((1, D), F32)],
                          compiler_params=_cp(("arbitrary",)))(z, g, b, target)


CONV_BT = 128
SUBLANES = 8


def _shifted(ref, cs, bt, offsets, stage):
    for r in range(SUBLANES):
        offs = [off for off in offsets if off % SUBLANES == r]
        if offs:
            rows = bt + max(offs) - r
            stage[r, 0:rows, :] = ref[pl.ds(r, rows), cs]
    for k, off in enumerate(offsets):
        r = off % SUBLANES
        yield k, stage[r, off - r:off - r + bt, :]


def conv_fwd(name, h1, dw_w, dw_b, ln_g, ln_b, comm=None):
    T, D2 = h1.shape
    D = D2 // 2
    W = dw_w.shape[0]
    taps = W - 1
    bt = _blk(T, CONV_BT)
    hb = bt // HALO
    u_off = [HALO - (taps - 1) + k for k in range(taps)]

    def body(h_ref, hp_ref, w_ref, b_ref, g_ref, be_ref, c_ref, s_ref, ux, stage):
        i = pl.program_id(0)
        up = hp_ref[:, :D] * _sig(hp_ref[:, D:])
        ux[0:HALO, :] = jnp.where(i == 0, 0.0, up)
        ux[HALO:HALO + bt, :] = h_ref[:, :D] * _sig(h_ref[:, D:])
        for cb in range(D // LANE):
            cs = slice(cb * LANE, (cb + 1) * LANE)
            acc = jnp.broadcast_to(b_ref[:, cs], (bt, LANE))
            for k, u_k in _shifted(ux, cs, bt, u_off, stage.at[cb % 2]):
                acc = acc + w_ref[k:k + 1, cs] * u_k
            c_ref[:, cs] = acc
        n = _ln_stats(c_ref[...]) * g_ref[...] + be_ref[...]
        s_ref[...] = (n * _sig(n)).astype(MXU_DTYPE)

    main = pl.BlockSpec((bt, D2), lambda i: (i, 0))
    prev = pl.BlockSpec((HALO, D2), lambda i: (jnp.maximum(i * hb - 1, 0), 0))
    wspec = pl.BlockSpec((W, D), lambda i: (0, 0))
    vec = pl.BlockSpec((1, D), lambda i: (0, 0))
    row = pl.BlockSpec((bt, D), lambda i: (i, 0))
    return _call(body, name=name, grid=(T // bt,), in_specs=[main, prev, wspec, vec, vec, vec], out_specs=[row, row],
                 out_shape=[jax.ShapeDtypeStruct((T, D), F32), jax.ShapeDtypeStruct((T, D), MXU_DTYPE)],
                 scratch_shapes=[pltpu.VMEM((HALO + bt, D), F32), pltpu.VMEM((2, SUBLANES, HALO + bt, LANE), F32)],
                 args=(h1, h1, dw_w, dw_b, ln_g, ln_b), sem=("parallel",), comm=comm)


def conv_bwd_ln(name, ds, c, ln_g, ln_b, comm=None):
    T, D = c.shape
    br = _blk(T, 512)

    def body(ds_ref, c_ref, g_ref, be_ref, dc_ref, dg_ref, db_ref, dcs_ref):
        i = pl.program_id(0)
        c = c_ref[...]
        mu = jnp.mean(c, axis=-1, keepdims=True)
        cc = c - mu
        var = jnp.mean(cc * cc, axis=-1, keepdims=True)
        rstd = lax.rsqrt(var + LN_EPS)
        xhat = cc * rstd
        n = xhat * g_ref[...] + be_ref[...]
        sg = _sig(n)
        dn = ds_ref[...] * (sg * (1.0 + n * (1.0 - sg)))
        dxh = dn * g_ref[...]
        m1 = jnp.mean(dxh, axis=-1, keepdims=True)
        m2 = jnp.mean(dxh * xhat, axis=-1, keepdims=True)
        dc = rstd * (dxh - m1 - xhat * m2)
        dc_ref[...] = dc

        @pl.when(i == 0)
        def _():
            dg_ref[...] = jnp.zeros_like(dg_ref)
            db_ref[...] = jnp.zeros_like(db_ref)
            dcs_ref[...] = jnp.zeros_like(dcs_ref)

        dg_ref[...] += jnp.sum(dn * xhat, axis=0, keepdims=True)
        db_ref[...] += jnp.sum(dn, axis=0, keepdims=True)
        dcs_ref[...] += jnp.sum(dc, axis=0, keepdims=True)

    row = pl.BlockSpec((br, D), lambda i: (i, 0))
    vec = pl.BlockSpec((1, D), lambda i: (0, 0))
    vshape = jax.ShapeDtypeStruct((1, D), F32)
    outs, delivered = _call(body, name=name, grid=(T // br,), in_specs=[row, row, vec, vec],
                            out_specs=[row, vec, vec, vec],
                            out_shape=[jax.ShapeDtypeStruct((T, D), F32), vshape, vshape, vshape],
                            args=(ds, c, ln_g, ln_b), sem=("arbitrary",), comm=comm)
    return outs if comm is None else (outs, delivered)


def conv_bwd_dw(name, dc, h1, dw_w, comm=None):
    T, D2 = h1.shape
    D = D2 // 2
    W = dw_w.shape[0]
    taps = W - 1
    bt = _blk(T, CONV_BT)
    hb = bt // HALO
    n = T // bt

    def body(dc_ref, dcn_ref, h_ref, hp_ref, w_ref, dh_ref, dw_ref, dhs_ref, ux, dcx, du, stage_dc, stage_u):
        i = pl.program_id(0)
        a = h_ref[:, :D]
        sg = _sig(h_ref[:, D:])
        up = hp_ref[:, :D] * _sig(hp_ref[:, D:])
        ux[0:HALO, :] = jnp.where(i == 0, 0.0, up)
        ux[HALO:HALO + bt, :] = a * sg
        dcx[0:bt, :] = dc_ref[...]
        dcx[bt:bt + HALO, :] = jnp.where(i == n - 1, 0.0, dcn_ref[...])

        @pl.when(i == 0)
        def _():
            dw_ref[...] = jnp.zeros_like(dw_ref)
            dhs_ref[...] = jnp.zeros_like(dhs_ref)

        for cb in range(D // LANE):
            cs = slice(cb * LANE, (cb + 1) * LANE)
            dcb = dcx[0:bt, cs]
            acc = jnp.zeros((bt, LANE), F32)
            for k, dc_k in _shifted(dcx, cs, bt, [taps - 1 - k for k in range(taps)], stage_dc.at[cb % 2]):
                acc = acc + w_ref[k:k + 1, cs] * dc_k
            du[:, cs] = acc
            for k, u_k in _shifted(ux, cs, bt, [HALO - (taps - 1) + k for k in range(taps)], stage_u.at[cb % 2]):
                dw_ref[k:k + 1, cs] += jnp.sum(dcb * u_k, axis=0, keepdims=True)
        d_u = du[...]
        da = d_u * sg
        dg = d_u * a * sg * (1.0 - sg)
        dh_ref[:, :D] = da.astype(MXU_DTYPE)
        dh_ref[:, D:] = dg.astype(MXU_DTYPE)
        dhs_ref[:, :D] += jnp.sum(da, axis=0, keepdims=True)
        dhs_ref[:, D:] += jnp.sum(dg, axis=0, keepdims=True)

    row = pl.BlockSpec((bt, D), lambda i: (i, 0))
    nxt = pl.BlockSpec((HALO, D), lambda i: (jnp.minimum((i + 1) * hb, T // HALO - 1), 0))
    main = pl.BlockSpec((bt, D2), lambda i: (i, 0))
    prev = pl.BlockSpec((HALO, D2), lambda i: (jnp.maximum(i * hb - 1, 0), 0))
    wspec = pl.BlockSpec((W, D), lambda i: (0, 0))
    return _call(body, name=name, grid=(n,), in_specs=[row, nxt, main, prev, wspec],
                 out_specs=[main, wspec, pl.BlockSpec((1, D2), lambda i: (0, 0))],
                 out_shape=[jax.ShapeDtypeStruct((T, D2), MXU_DTYPE), jax.ShapeDtypeStruct((W, D), F32),
                            jax.ShapeDtypeStruct((1, D2), F32)],
                 scratch_shapes=[pltpu.VMEM((HALO + bt, D), F32), pltpu.VMEM((bt + HALO, D), F32),
                                 pltpu.VMEM((bt, D), F32), pltpu.VMEM((2, SUBLANES, HALO + bt, LANE), F32),
                                 pltpu.VMEM((2, SUBLANES, HALO + bt, LANE), F32)],
                 args=(dc, dc, h1, h1, dw_w), sem=("arbitrary",), comm=comm)


def _t5_bucket(dist, n_buckets):
    max_exact = n_buckets // 2
    large = max_exact + (np.log(np.maximum(dist, 1) / max_exact) / math.log(REL_MAX_DIST / max_exact)
                         * (n_buckets - max_exact)).astype(np.int32)
    large = np.minimum(large, n_buckets - 1)
    return np.where(dist < max_exact, dist, large).astype(np.int32)


def _band_tables(n_buckets):
    i = np.arange(BAND)[:, None]
    j = np.arange(2 * BAND)[None, :]
    delta = i - j + BAND
    out = []
    for window, dil in BRANCHES:
        ok = (delta >= 0) & (delta <= window // dil)
        out.append((_t5_bucket(np.clip(delta, 0, None) * dil, n_buckets), ok))
    return out


def _units():
    for bi in (2, 1, 0):
        d = BRANCHES[bi][1]
        for r in range(d):
            for nb in range(ATT_TB // (BAND * d)):
                yield bi, d, r, nb


def _staggered(units, stages):
    n = len(stages)
    state = {}
    for t in range(len(units) + n - 1):
        for k in range(n):
            u = t - k
            if 0 <= u < len(units):
                state[u] = stages[k](units[u], state.get(u))
    return


def _rows(start, size, d):
    return pl.ds(start, size) if d == 1 else pl.ds(start, size, stride=d)


def _bc(v):
    return jnp.broadcast_to(v, (BAND, LANE))


def attn_fwd(name, q, kv, bias, comm=None):
    T, D = q.shape
    H = D // HEAD_DIM
    TB = ATT_TB
    scale = HEAD_DIM ** -0.5

    def body(q_ref, kc_ref, kp_ref, vc_ref, vp_ref, b_ref, o_ref, l_ref, kx, vx, m_sc, s_sc, a_sc):
        first = (pl.program_id(1) == 0).astype(jnp.int32)
        kx[0:TB, :] = kp_ref[...]
        kx[TB:2 * TB, :] = kc_ref[...]
        vx[0:TB, :] = vp_ref[...]
        vx[TB:2 * TB, :] = vc_ref[...]
        def scores(unit, _):
            bi, d, r, nb = unit
            qs = _rows(nb * BAND * d + r, BAND, d)
            ks = _rows(TB + (nb - 1) * BAND * d + r, 2 * BAND, d)
            qb = q_ref[qs, :].astype(MXU_DTYPE)
            kb = kx[ks, :].astype(MXU_DTYPE)
            vb = vx[ks, :].astype(MXU_DTYPE)
            b = b_ref[bi + 3 * first] if nb == 0 else b_ref[bi]
            return qs, vb, lax.dot_general(qb, kb, NT, preferred_element_type=F32) * scale + b

        def softmax(unit, st):
            qs, vb, s = st
            mrow = jnp.max(s, axis=1, keepdims=True)
            p = jnp.exp(s - mrow)
            m_sc.at[unit[0]][qs, :] = _bc(mrow)
            s_sc.at[unit[0]][qs, :] = _bc(jnp.sum(p, axis=1, keepdims=True))
            return qs, vb, p.astype(MXU_DTYPE)

        def values(unit, st):
            qs, vb, pb = st
            a_sc.at[unit[0]][qs, :] = lax.dot_general(pb, vb, NN, preferred_element_type=F32)

        _staggered(list(_units()), [scores, softmax, values])
        for c0 in range(0, TB, ATT_MERGE_ROWS):
            rows = slice(c0, c0 + ATT_MERGE_ROWS)
            ms = [m_sc[bi, rows, :] for bi in range(3)]
            m = jnp.maximum(jnp.maximum(ms[0], ms[1]), ms[2])
            ws = [jnp.exp(mi - m) for mi in ms]
            den = ws[0] * s_sc[0, rows, :] + ws[1] * s_sc[1, rows, :] + ws[2] * s_sc[2, rows, :]
            num = ws[0] * a_sc[0, rows, :] + ws[1] * a_sc[1, rows, :] + ws[2] * a_sc[2, rows, :]
            o_ref[rows, :] = num / den
            l_ref[rows, :] = m + jnp.log(den)

    cur = lambda off: pl.BlockSpec((TB, HEAD_DIM), lambda h, i: (i, off + h))
    prv = lambda off: pl.BlockSpec((TB, HEAD_DIM), lambda h, i: (jnp.maximum(i - 1, 0), off + h))
    bspec = pl.BlockSpec((6, None, BAND, 2 * BAND), lambda h, i: (0, h, 0, 0))
    sc = lambda *shape: pltpu.VMEM(shape + (HEAD_DIM,), F32)
    return _call(body, name=name, grid=(H, T // TB), in_specs=[cur(0), cur(0), prv(0), cur(H), prv(H), bspec],
                 out_specs=[cur(0), cur(0)],
                 out_shape=[jax.ShapeDtypeStruct((T, D), F32), jax.ShapeDtypeStruct((T, D), F32)],
                 scratch_shapes=[sc(2 * TB), sc(2 * TB), sc(3, TB), sc(3, TB), sc(3, TB)],
                 args=(q, kv, kv, kv, kv, bias), sem=("parallel", "parallel"), comm=comm)


def attn_bwd(name, q, kv, bias, o, lse, do, comm=None):
    T, D = q.shape
    H = D // HEAD_DIM
    TB = ATT_TB
    nI = T // TB
    scale = HEAD_DIM ** -0.5

    def body(q_ref, kc_ref, kp_ref, vc_ref, vp_ref, b_ref, o_ref, l_ref, do_ref,
             dq_ref, dkv_hbm, db_ref, kx, vx, dkx, dvx, ck, cv, dl_sc, dq_sc, stage, out_sems):
        step = pl.program_id(1)
        first = (step == nI - 1).astype(jnp.int32)
        kx[0:TB, :] = kp_ref[...]
        kx[TB:2 * TB, :] = kc_ref[...]
        vx[0:TB, :] = vp_ref[...]
        vx[TB:2 * TB, :] = vc_ref[...]
        dl_sc[...] = jnp.broadcast_to(jnp.sum(do_ref[...] * o_ref[...], axis=1, keepdims=True), (TB, LANE))

        @pl.when(step == 0)
        def _():
            db_ref[...] = jnp.zeros_like(db_ref)
            ck[...] = jnp.zeros_like(ck)
            cv[...] = jnp.zeros_like(cv)

        held = [None, None]

        def scores(unit, _):
            bi, d, r, nb = unit
            qs = _rows(nb * BAND * d + r, BAND, d)
            ks = _rows(TB + (nb - 1) * BAND * d + r, 2 * BAND, d)
            qb = q_ref[qs, :].astype(MXU_DTYPE)
            kb = kx[ks, :].astype(MXU_DTYPE)
            vb = vx[ks, :].astype(MXU_DTYPE)
            dob = do_ref[qs, :].astype(MXU_DTYPE)
            b = b_ref[bi + 3 * first] if nb == 0 else b_ref[bi]
            s = lax.dot_general(qb, kb, NT, preferred_element_type=F32) * scale + b
            dp = lax.dot_general(dob, vb, NT, preferred_element_type=F32)
            return qs, qb, kb, dob, s, dp

        def softmax_bwd(unit, st):
            qs, qb, kb, dob, s, dp = st
            p = jnp.exp(s - l_ref[qs, :][:, :1])
            dsv = p * (dp - dl_sc[qs, :][:, :1])
            db_ref[unit[0]] += dsv
            return qs, qb, kb, dob, p.astype(MXU_DTYPE), dsv.astype(MXU_DTYPE)

        def products(unit, st):
            bi, d, r, nb = unit
            qs, qb, kb, dob, pb, dsb = st
            band = lambda b: _rows(TB + b * BAND * d + r, BAND, d)
            dv_blk = lax.dot_general(pb, dob, TN, preferred_element_type=F32)
            dk_blk = lax.dot_general(dsb, qb, TN, preferred_element_type=F32) * scale
            dq_sc.at[bi][qs, :] = lax.dot_general(dsb, kb, NN, preferred_element_type=F32) * scale
            dkb, dvb = dkx.at[bi], dvx.at[bi]
            if nb == 0:
                dkb[band(-1), :] = dk_blk[:BAND]
                dvb[band(-1), :] = dv_blk[:BAND]
            else:
                dkb[band(nb - 1), :] = held[0] + dk_blk[:BAND]
                dvb[band(nb - 1), :] = held[1] + dv_blk[:BAND]
            held[0], held[1] = dk_blk[BAND:], dv_blk[BAND:]
            if nb == TB // (BAND * d) - 1:
                dkb[band(nb), :] = held[0]
                dvb[band(nb), :] = held[1]

        _staggered(list(_units()), [scores, softmax_bwd, products])

        dq_ref[...] = (dq_sc[0] + dq_sc[1] + dq_sc[2]).astype(dq_ref.dtype)
        head = pl.program_id(0)
        rows = pl.ds(pl.multiple_of((nI - 1 - step) * TB, TB), TB)

        def writes():
            return [pltpu.make_async_copy(
                stage.at[j], dkv_hbm.at[rows, pl.ds(pl.multiple_of(j * D + head * HEAD_DIM, HEAD_DIM), HEAD_DIM)],
                out_sems.at[j]) for j in range(2)]

        @pl.when(jnp.logical_or(head > 0, step > 0))
        def _():
            for cp in writes():
                cp.wait()

        for j, (dxx, cx) in enumerate(((dkx, ck), (dvx, cv))):
            stage[j] = (dxx[0, TB:2 * TB, :] + dxx[1, TB:2 * TB, :] + dxx[2, TB:2 * TB, :] + cx[...]).astype(stage.dtype)
            cx[...] = dxx[2, 0:TB, :]
            for bi in (1, 0):
                lo = TB - BAND * BRANCHES[bi][1]
                cx[lo:TB, :] += dxx[bi, lo:TB, :]
        for cp in writes():
            cp.start()

        @pl.when(jnp.logical_and(head == H - 1, step == nI - 1))
        def _():
            for cp in writes():
                cp.wait()

    blk = lambda h, i: nI - 1 - i
    cur = lambda off: pl.BlockSpec((TB, HEAD_DIM), lambda h, i: (blk(h, i), off + h))
    prv = lambda off: pl.BlockSpec((TB, HEAD_DIM), lambda h, i: (jnp.maximum(blk(h, i) - 1, 0), off + h))
    bspec = pl.BlockSpec((3, None, BAND, 2 * BAND), lambda h, i: (0, h, 0, 0))
    bspec_in = pl.BlockSpec((6, None, BAND, 2 * BAND), lambda h, i: (0, h, 0, 0))
    sc = lambda *shape: pltpu.VMEM(shape + (HEAD_DIM,), F32)
    return _call(body, name=name, grid=(H, nI),
                 in_specs=[cur(0), cur(0), prv(0), cur(H), prv(H), bspec_in, cur(0), cur(0), cur(0)],
                 out_specs=[cur(0), ANY, bspec],
                 out_shape=[jax.ShapeDtypeStruct((T, D), MXU_DTYPE), jax.ShapeDtypeStruct((T, 2 * D), MXU_DTYPE),
                            jax.ShapeDtypeStruct((3, H, BAND, 2 * BAND), F32)],
                 scratch_shapes=[sc(2 * TB), sc(2 * TB), sc(3, 2 * TB), sc(3, 2 * TB), sc(TB), sc(TB), sc(TB), sc(3, TB),
                                 pltpu.VMEM((2, TB, HEAD_DIM), MXU_DTYPE), pltpu.SemaphoreType.DMA((2,))],
                 args=(q, kv, kv, kv, kv, bias, o, lse, do), sem=("arbitrary", "arbitrary"), comm=comm)


def adamw(name, parts, w, m, v):
    L, R, C = w.shape
    P = parts[0].shape[0]
    br = R if R % 8 else _blk(R, max(8, (1 << 18) // C))
    c1 = 1.0 / (1.0 - ADAM_B1 ** ADAM_STEP)
    c2 = 1.0 / (1.0 - ADAM_B2 ** ADAM_STEP)

    def body(*refs):
        p_refs = refs[:L]
        w_ref, m_ref, v_ref, g_ref, d_ref, nm_ref, nv_ref = refs[L:]
        lay = pl.program_id(0)

        def total(p_ref):
            g = p_ref[0].astype(F32)
            for k in range(1, P):
                g = g + p_ref[k].astype(F32)
            return g

        g = total(p_refs[0])
        for j in range(1, L):
            g = jnp.where(lay == j, total(p_refs[j]), g)
        nm = ADAM_B1 * m_ref[...] + (1.0 - ADAM_B1) * g
        nv = ADAM_B2 * v_ref[...] + (1.0 - ADAM_B2) * (g * g)
        g_ref[...] = g
        nm_ref[...] = nm
        nv_ref[...] = nv
        d_ref[...] = -ADAM_LR * ((nm * c1) / (jnp.sqrt(nv * c2) + ADAM_EPS) + ADAM_WD * w_ref[...])

    row = pl.BlockSpec((None, br, C), lambda l, i: (l, i, 0))
    pspecs = [pl.BlockSpec((P, br, C), lambda l, i, j=j: (0, jnp.where(l == j, i, 0), 0)) for j in range(L)]
    shp = jax.ShapeDtypeStruct((L, R, C), F32)
    return pl.pallas_call(body, name=name, grid=(L, R // br), in_specs=[*pspecs, row, row, row],
                          out_specs=[row] * 4, out_shape=[shp] * 4,
                          compiler_params=_cp(("parallel", "parallel")))(*parts, w, m, v)


def pair_sum(name, g, s1, c_idx):
    _, _, R, C = g.shape
    br = _blk(R, max(16, (1 << 19) // C))

    def body(c_ref, g_ref, s_ref, t_ref):
        t_ref[...] = (g_ref[...].astype(F32) + s_ref[...].astype(F32)).astype(t_ref.dtype)

    return pl.pallas_call(
        body, name=name,
        grid_spec=pltpu.PrefetchScalarGridSpec(
            num_scalar_prefetch=1, grid=(4, R // br),
            in_specs=[pl.BlockSpec((None, None, br, C), lambda j, i, c: (j, c[0], i, 0)),
                      pl.BlockSpec((None, br, C), lambda j, i, c: (j, i, 0))],
            out_specs=pl.BlockSpec((None, br, C), lambda j, i, c: (j, i, 0))),
        out_shape=jax.ShapeDtypeStruct((4, R, C), g.dtype), compiler_params=_cp(("parallel", "parallel")))(c_idx, g, s1)


def _me():
    return lax.axis_index("x"), lax.axis_index("y"), lax.axis_index("c")


ANY = pl.BlockSpec(memory_space=pl.ANY)


class Gather:
    relay_at = 0.7

    def __init__(self, shards):
        n = len(shards)
        self.arrays = list(shards)
        self.out_shapes = [jax.ShapeDtypeStruct((N_DEV,) + s.shape, s.dtype) for s in shards]
        self.sems = [pltpu.SemaphoreType.DMA((n, 7)), pltpu.SemaphoreType.DMA((n, 7)), pltpu.SemaphoreType.DMA((n,))]

    def _ctx(self, ins, outs, sems):
        send_sems, recv_sems, local_sems = sems
        x, y, c = _me()
        chips = [(1 - x, y), (x, 1 - y), (1 - x, 1 - y)]

        def copy(a, k, block, to, src=None):
            px, py, pc = block
            dst = outs[a].at[4 * px + 2 * py + pc]
            return pltpu.make_async_remote_copy(src_ref=dst if src is None else src, dst_ref=dst,
                                                send_sem=send_sems.at[a, k], recv_sem=recv_sems.at[a, k],
                                                device_id=to, device_id_type=MESH)

        def own(a):
            return pltpu.make_async_copy(ins[a], outs[a].at[4 * x + 2 * y + c], local_sems.at[a])

        def first(a):
            return [copy(a, 0, (x, y, c), (x, y, 1 - c), src=ins[a])] + \
                   [copy(a, 1 + j, (x, y, c), (*chip, c), src=ins[a]) for j, chip in enumerate(chips)]

        def passed(a):
            return [copy(a, 4 + j, (*chip, c), (x, y, 1 - c)) for j, chip in enumerate(chips)]

        return (x, y, c), chips, copy, own, first, passed

    def start(self, ins, outs, sems):
        _, _, _, own, first, _ = self._ctx(ins, outs, sems)
        for a in range(len(ins)):
            own(a).start()
            for cp in first(a):
                cp.start()

    def relay(self, ins, outs, sems):
        me, chips, copy, _, _, passed = self._ctx(ins, outs, sems)
        for a in range(len(ins)):
            fwd = passed(a)
            for j, chip in enumerate(chips):
                copy(a, 1 + j, (*chip, me[2]), me).wait_recv()
                fwd[j].start()

    def wait(self, ins, outs, sems):
        (x, y, c), chips, copy, own, first, passed = self._ctx(ins, outs, sems)
        for a in range(len(ins)):
            copy(a, 0, (x, y, 1 - c), (x, y, c)).wait_recv()
            for j, chip in enumerate(chips):
                copy(a, 4 + j, (*chip, 1 - c), (x, y, c)).wait_recv()
            for cp in first(a) + passed(a):
                cp.wait_send()
            own(a).wait()


class PairExchange:
    relay_at = None

    def __init__(self, grads):
        n = len(grads)
        self.arrays = list(grads)
        self.out_shapes = [jax.ShapeDtypeStruct((4,) + g.shape[2:], g.dtype) for g in grads]
        self.sems = [pltpu.SemaphoreType.DMA((n, 4)), pltpu.SemaphoreType.DMA((n, 4))]

    def _copies(self, ins, outs, sems):
        send_sems, recv_sems = sems
        x, y, c = _me()
        return [pltpu.make_async_remote_copy(
            src_ref=ins[a].at[j, 1 - c], dst_ref=outs[a].at[j], send_sem=send_sems.at[a, j],
            recv_sem=recv_sems.at[a, j], device_id=(x, y, 1 - c), device_id_type=MESH)
            for a in range(len(ins)) for j in range(4)]

    def start(self, ins, outs, sems):
        for cp in self._copies(ins, outs, sems):
            cp.start()

    def wait(self, ins, outs, sems):
        for cp in self._copies(ins, outs, sems):
            cp.wait()


def pair_exchange(name, grads):
    comm = PairExchange(grads)
    n = len(grads)

    def body(*refs):
        comm.start(refs[:n], refs[n:2 * n], refs[2 * n:])
        comm.wait(refs[:n], refs[n:2 * n], refs[2 * n:])

    return pl.pallas_call(body, name=name, in_specs=[ANY] * n, out_specs=[ANY] * n, out_shape=comm.out_shapes,
                          scratch_shapes=comm.sems)(*grads)


class ChipExchange:
    relay_at = None

    def __init__(self, sums):
        n = len(sums)
        self.arrays = list(sums)
        self.out_shapes = [jax.ShapeDtypeStruct(s.shape, s.dtype) for s in sums]
        self.sems = [pltpu.SemaphoreType.DMA((n, 3)), pltpu.SemaphoreType.DMA((n, 3)), pltpu.SemaphoreType.DMA((n,))]

    def _copies(self, ins, outs, sems, arrivals):
        send_sems, recv_sems, local_sems = sems
        x, y, c = _me()
        my = 2 * x + y
        chips = [(1 - x, y), (x, 1 - y), (1 - x, 1 - y)]
        mine, sends, recvs = [], [], []
        for a in range(len(ins)):
            mine.append(pltpu.make_async_copy(ins[a].at[my], outs[a].at[my], local_sems.at[a]))
            for j, (px, py) in enumerate(chips):
                sends.append(pltpu.make_async_remote_copy(
                    src_ref=ins[a].at[2 * px + py], dst_ref=outs[a].at[my],
                    send_sem=send_sems.at[a, j], recv_sem=recv_sems.at[a, j],
                    device_id=(px, py, c), device_id_type=MESH))
                if arrivals:
                    recvs.append(pltpu.make_async_remote_copy(
                        src_ref=ins[a].at[my], dst_ref=outs[a].at[2 * px + py],
                        send_sem=send_sems.at[a, j], recv_sem=recv_sems.at[a, j],
                        device_id=(px, py, c), device_id_type=MESH))
        return mine, sends, recvs

    def start(self, ins, outs, sems):
        mine, sends, _ = self._copies(ins, outs, sems, arrivals=False)
        for cp in mine + sends:
            cp.start()

    def wait(self, ins, outs, sems):
        mine, sends, recvs = self._copies(ins, outs, sems, arrivals=True)
        for cp in recvs:
            cp.wait_recv()
        for cp in sends:
            cp.wait_send()
        for cp in mine:
            cp.wait()


def all_reduce_small(name, pack):
    R, C = pack.shape

    def body(p_ref, o_ref, buf, send_sems, recv_sems):
        x, y, c = _me()
        me = 4 * x + 2 * y + c
        buf[me] = p_ref[...]
        copies = []
        for k in range(1, N_DEV):
            px, py, pc = x ^ (k >> 2), y ^ ((k >> 1) & 1), c ^ (k & 1)
            copies.append(pltpu.make_async_remote_copy(
                src_ref=p_ref, dst_ref=buf.at[me], send_sem=send_sems.at[k - 1], recv_sem=recv_sems.at[k - 1],
                device_id=(px, py, pc), device_id_type=MESH))
        for cp in copies:
            cp.start()
        for cp in copies:
            cp.wait()
        acc = buf[0]
        for d in range(1, N_DEV):
            acc = acc + buf[d]
        o_ref[...] = acc

    vm = pl.BlockSpec(memory_space=pltpu.VMEM)
    return pl.pallas_call(
        body, name=name, in_specs=[vm], out_specs=vm, out_shape=jax.ShapeDtypeStruct((R, C), F32),
        scratch_shapes=[pltpu.VMEM((N_DEV, R, C), F32), pltpu.SemaphoreType.DMA((N_DEV - 1,)),
                        pltpu.SemaphoreType.DMA((N_DEV - 1,))],
    )(pack)


def _ep_bias(acc, ex, outs):
    outs[0][...] = acc + ex[0][...]


def _ep_store(acc, ex, outs):
    outs[0][...] = acc.astype(outs[0].dtype)


def _ep_resid(alpha, bias):
    def ep(acc, ex, outs):
        if bias:
            outs[0][...] = alpha * ex[1][...] + (acc + ex[0][...])
        else:
            outs[0][...] = alpha * ex[0][...] + acc
    return ep


def _ep_relu2(acc, ex, outs):
    r = jnp.maximum(acc, 0.0)
    outs[0][...] = (r * r).astype(outs[0].dtype)
    outs[1][...] = r


def _ep_relu2_bwd(acc, ex, outs):
    outs[0][...] = (acc * (2.0 * ex[0][...])).astype(outs[0].dtype)


def _ep_add(acc, ex, outs):
    outs[0][...] = ex[0][...] + acc


def kernel(x, conv_pw1_w, conv_pw1_b, conv_dw_w, conv_dw_b, conv_ln_g, conv_ln_b, conv_pw2_w, conv_pw2_b, w_kv, attn_wq, attn_wo, rel_bias, mlp_w1, mlp_w2, ln_mix_g, ln_mix_b, ln_mlp_g, ln_mlp_b, loss_target, m_conv_pw1_w, m_conv_pw1_b, m_conv_dw_w, m_conv_dw_b, m_conv_ln_g, m_conv_ln_b, m_conv_pw2_w, m_conv_pw2_b, m_w_kv, m_attn_wq, m_attn_wo, m_rel_bias, m_mlp_w1, m_mlp_w2, m_ln_mix_g, m_ln_mix_b, m_ln_mlp_g, m_ln_mlp_b, v_conv_pw1_w, v_conv_pw1_b, v_conv_dw_w, v_conv_dw_b, v_conv_ln_g, v_conv_ln_b, v_conv_pw2_w, v_conv_pw2_b, v_w_kv, v_attn_wq, v_attn_wo, v_rel_bias, v_mlp_w1, v_mlp_w2, v_ln_mix_g, v_ln_mix_b, v_ln_mlp_g, v_ln_mlp_b):
    T, D = x.shape[1], x.shape[2]
    H = D // HEAD_DIM
    depth = mlp_w1.shape[0]
    assert depth == 2 and T % ATT_TB == 0
    alpha = (2 * depth) ** 0.25
    ds_ = D // N_DEV
    xi, yi, ci = _me()
    me = 4 * xi + 2 * yi + ci
    c_idx = ci.astype(jnp.int32).reshape(1)
    x2 = x.reshape(T, D)
    target = loss_target.reshape(T, D)
    n_buckets = rel_bias.shape[0]
    taps = conv_dw_w.shape[1]

    local = {
        "pw1": (conv_pw1_w, 0), "pw2": (conv_pw2_w, 0), "wkv": (w_kv, None), "wq": (attn_wq, 0), "wo": (attn_wo, 0),
        "w1_0": (mlp_w1, 0), "w1_1": (mlp_w1, 1), "w2_0": (mlp_w2, 0), "w2_1": (mlp_w2, 1),
    }
    names = list(local)
    small = jnp.concatenate([conv_dw_w[0], conv_dw_b, conv_ln_g, conv_ln_b, conv_pw2_b,
                             conv_pw1_b.reshape(2, ds_)], axis=0)
    small = jnp.pad(small, ((0, (-small.shape[0]) % 8), (0, 0)))
    shard = {k: cast_bf16("cast_" + k, *local[k]) for k in names}
    G = {}

    def gather_start(*keys):
        return Gather([shard[k] for k in keys])

    def gather_done(keys, delivered):
        G.update(zip(keys, delivered))

    x_b, (g_pw1, g_small) = cast_bf16("cast_x", x2, comm=Gather([shard["pw1"], small]))
    G["pw1"] = g_pw1
    sm = jnp.transpose(g_small, (1, 0, 2)).reshape(small.shape[0], D)
    dw_w = jnp.pad(sm[:taps], ((0, HALO - taps), (0, 0)))
    dw_b, cln_g, cln_b, pw2_b = (sm[taps + k:taps + k + 1] for k in range(4))
    pw1_b = g_small[:, taps + 4:taps + 6, :].reshape(1, 2 * D)

    row = lambda a, l: a[l:l + 1]

    (h1,) = mm_fwd("pw1", x_b, G["pw1"], colsharded=True, epilogue=_ep_bias, outs=[F32], rowvec=[pw1_b])
    (c_pre, s_b), delivered = conv_fwd("conv_fwd", h1, dw_w, dw_b, cln_g, cln_b, comm=gather_start("pw2", "w1_0"))
    gather_done(("pw2", "w1_0"), delivered)
    (z1,) = mm_fwd("pw2", s_b, G["pw2"], colsharded=False, epilogue=_ep_resid(alpha, True), outs=[F32],
                   rowvec=[pw2_b], tiles=[x2])
    x1, x1_b = ln_fwd("ln_mix0", z1, row(ln_mix_g, 0), row(ln_mix_b, 0))

    def mlp_fwd(l, xin, xin_b, up_keys=(), down_keys=()):
        res_ = mm_fwd(f"mlp_up{l}", xin_b, G[f"w1_{l}"], colsharded=True, epilogue=_ep_relu2, outs=[MXU_DTYPE, F32],
                      comm=gather_start(*up_keys) if up_keys else None)
        (act_b, r), delivered = res_ if up_keys else (res_, [])
        gather_done(up_keys, delivered)
        res_ = mm_fwd(f"mlp_down{l}", act_b, G[f"w2_{l}"], colsharded=False, epilogue=_ep_resid(alpha, False),
                      outs=[F32], tiles=[xin], comm=gather_start(*down_keys) if down_keys else None)
        ((z,), delivered) = res_ if down_keys else (res_, [])
        gather_done(down_keys, delivered)
        if l == depth - 1:
            return act_b, r, z, None, None
        y, y_b = ln_fwd(f"ln_mlp{l}", z, row(ln_mlp_g, l), row(ln_mlp_b, l))
        return act_b, r, z, y, y_b

    act0_b, r0, z2, x2_, x2_b = mlp_fwd(0, x1, x1_b, up_keys=("w2_0",), down_keys=("wkv", "wq", "wo"))

    (kv,) = mm_fwd("kv_proj", x2_b, G["wkv"], colsharded=True, epilogue=_ep_store, outs=[F32])
    (q,) = mm_fwd("q_proj", x2_b, G["wq"], colsharded=False, epilogue=_ep_store, outs=[F32])
    tables = _band_tables(n_buckets)
    onehot = jnp.concatenate([(jnp.arange(n_buckets)[:, None] == jnp.asarray(bucket).reshape(1, -1)).astype(F32)
                              for bucket, _ in tables], axis=1)
    KB = onehot.shape[1] // 8
    (btab,) = _mm("rel_bias_table", rel_bias, onehot, grid=(1, 8, 1),
                  a_spec=pl.BlockSpec((n_buckets, H), lambda i, j, k: (0, 0)),
                  b_spec=pl.BlockSpec((n_buckets, KB), lambda i, j, k: (0, j)), dims=TN, k_axis=2,
                  epilogue=_ep_store, out_shapes=[jax.ShapeDtypeStruct((H, 8 * KB), F32)],
                  out_specs=[pl.BlockSpec((H, KB), lambda i, j, k: (0, j))], acc_shape=(H, KB), exact=True)
    band_ok = jnp.asarray(np.stack([ok for _, ok in tables]))[:, None]
    bias = jnp.where(band_ok, jnp.transpose(btab.reshape(H, 3, BAND, 2 * BAND), (1, 0, 2, 3)), NEG)
    has_prev = jnp.asarray(np.arange(2 * BAND) >= BAND)
    bias = jnp.concatenate([bias, jnp.where(has_prev, bias, NEG)], axis=0)
    (o, lse), delivered = attn_fwd("attn_fwd", q, kv, bias, comm=gather_start("w1_1", "w2_1"))
    gather_done(("w1_1", "w2_1"), delivered)
    (z3,) = mm_fwd("o_proj", o, G["wo"], colsharded=False, epilogue=_ep_resid(alpha, False), outs=[F32], tiles=[x2_])
    x3, x3_b = ln_fwd("ln_mix1", z3, row(ln_mix_g, 1), row(ln_mix_b, 1))
    act1_b, r1, z4, _, _ = mlp_fwd(1, x3, x3_b)

    loss_local, *last_norm_bwd = loss_ln_bwd("loss_ln_mlp1_bwd", z4, row(ln_mlp_g, 1), row(ln_mlp_b, 1), target)
    loss = lax.psum(loss_local[0, 0], MESH_AXES)

    parts = {}

    def pair_of(grads):
        g4 = [g.reshape((4, 2) + g.shape[1:]) for g in grads.values()]
        return g4, PairExchange(g4)

    def chip_of(grads, g4, from_sibling):
        return ChipExchange([pair_sum("grad_pair_sum_" + k, g, s, c_idx) for k, g, s in zip(grads, g4, from_sibling)])

    def reduce_done(grads, delivered):
        parts.update(zip(grads, delivered))

    def mlp_bwd(l, dy_out, z, xin_b, act_b, r, pending=None, norm_bwd=None):
        if norm_bwd is not None:
            dz, dz_b, dg, db = norm_bwd
            comm = None
        elif pending is None:
            dz, dz_b, dg, db, _ = ln_bwd(f"ln_mlp{l}_bwd", dy_out, z, row(ln_mlp_g, l))
            comm = None
        else:
            (dz, dz_b, dg, db, _), from_sibling = ln_bwd(f"ln_mlp{l}_bwd", dy_out, z, row(ln_mlp_g, l), comm=pending[2])
            comm = chip_of(pending[0], pending[1], from_sibling)
        res_ = mm_dx(f"mlp_down{l}_dx", dz_b, G[f"w2_{l}"], colsharded=False, epilogue=_ep_relu2_bwd,
                     outs=[MXU_DTYPE], tiles=[r], comm=comm)
        ((dhm_b,), delivered) = res_ if comm is not None else (res_, [])
        dw2 = mm_dw(f"mlp_down{l}_dw", act_b, dz_b, colsharded=False)
        (dxin,) = mm_dx(f"mlp_up{l}_dx", dhm_b, G[f"w1_{l}"], colsharded=True, epilogue=_ep_resid(alpha, False),
                        outs=[F32], tiles=[dz])
        dw1 = mm_dw(f"mlp_up{l}_dw", xin_b, dhm_b, colsharded=True)
        return dxin, dw1, dw2, dg, db, delivered

    dx3, dw1_1, dw2_1, dg_mlp1, db_mlp1, _ = mlp_bwd(1, None, z4, x3_b, act1_b, r1, norm_bwd=last_norm_bwd)
    grads1 = {"w1_1": dw1_1, "w2_1": dw2_1}
    g4_1, pair1 = pair_of(grads1)

    (dz3, dz3_b, dg_mix1, db_mix1, _), from_sibling = ln_bwd("ln_mix1_bwd", dx3, z3, row(ln_mix_g, 1), comm=pair1)
    comm1 = chip_of(grads1, g4_1, from_sibling)
    (do,) = mm_dx("o_proj_dx", dz3_b, G["wo"], colsharded=False, epilogue=_ep_store, outs=[F32])
    dwo = mm_dw("o_proj_dw", o, dz3_b, colsharded=False)
    (dq, dkv, dbias), delivered = attn_bwd("attn_bwd", q, kv, bias, o, lse, do, comm=comm1)
    reduce_done(grads1, delivered)
    (dx2a,) = mm_dx("q_proj_dx", dq, G["wq"], colsharded=False, epilogue=_ep_resid(alpha, False), outs=[F32], tiles=[dz3])
    (dx2,) = mm_dx("kv_proj_dx", dkv, G["wkv"], colsharded=True, epilogue=_ep_add, outs=[F32], tiles=[dx2a])
    dwq = mm_dw("q_proj_dw", x2_b, dq, colsharded=False)
    dwkv = mm_dw("kv_proj_dw", x2_b, dkv, colsharded=True)
    grads2 = {"wo": dwo, "wq": dwq, "wkv": dwkv}
    g4_2, pair2 = pair_of(grads2)

    dbias2 = jnp.transpose(dbias, (1, 0, 2, 3)).reshape(H, -1)
    (drel_t,) = _mm("rel_bias_grad", dbias2, onehot, grid=(1, 1, 8),
                    a_spec=pl.BlockSpec((H, KB), lambda i, j, k: (0, k)),
                    b_spec=pl.BlockSpec((n_buckets, KB), lambda i, j, k: (0, k)), dims=NT, k_axis=2,
                    epilogue=_ep_store, out_shapes=[jax.ShapeDtypeStruct((H, n_buckets), F32)],
                    out_specs=[pl.BlockSpec((H, n_buckets), lambda i, j, k: (0, 0))], acc_shape=(H, n_buckets), exact=True)

    dx1, dw1_0, dw2_0, dg_mlp0, db_mlp0, delivered = mlp_bwd(0, dx2, z2, x1_b, act0_b, r0, pending=(grads2, g4_2, pair2))
    reduce_done(grads2, delivered)

    dz1, dz1_b, dg_mix0, db_mix0, dpw2_b = ln_bwd("ln_mix0_bwd", dx1, z1, row(ln_mix_g, 0))
    (ds,) = mm_dx("pw2_dx", dz1_b, G["pw2"], colsharded=False, epilogue=_ep_store, outs=[F32])
    dwpw2 = mm_dw("pw2_dw", s_b, dz1_b, colsharded=False)
    grads3 = {"w1_0": dw1_0, "w2_0": dw2_0, "pw2": dwpw2}
    g4_3, pair3 = pair_of(grads3)
    (dc, dcln_g, dcln_b, ddw_b), from_sibling = conv_bwd_ln("conv_bwd_ln", ds, c_pre, cln_g, cln_b, comm=pair3)
    (dh1_b, ddw_w, dpw1_b), delivered = conv_bwd_dw("conv_bwd_dw", dc, h1, dw_w,
                                                   comm=chip_of(grads3, g4_3, from_sibling))
    reduce_done(grads3, delivered)
    grads4 = {"pw1": mm_dw("pw1_dw", x_b, dh1_b, colsharded=True)}
    g4_4, _ = pair_of(grads4)
    comm4 = chip_of(grads4, g4_4, pair_exchange("grad_pair_exchange_pw1", g4_4))
    (dx,), delivered = mm_dx("pw1_dx", dh1_b, G["pw1"], colsharded=True, epilogue=_ep_resid(alpha, False), outs=[F32],
                             tiles=[dz1], comm=comm4)
    reduce_done(grads4, delivered)
    grad_x = dx.reshape(1, T, D)

    vec_rows = [dg_mix0, dg_mix1, db_mix0, db_mix1, dg_mlp0, dg_mlp1, db_mlp0, db_mlp1,
                ddw_b, dcln_g, dcln_b, dpw2_b, dpw1_b.reshape(2, D), ddw_w[:taps],
                jnp.pad(jnp.transpose(drel_t).reshape(1, -1), ((0, 0), (0, D - H * n_buckets)))]
    pack = jnp.concatenate(vec_rows, axis=0)
    pack = jnp.pad(pack, ((0, (-pack.shape[0]) % 8), (0, 0)))
    tot = all_reduce_small("grad_small_all_reduce", pack)

    def mine(rows):
        return lax.dynamic_slice_in_dim(rows, me * ds_, ds_, axis=1)

    g_ln_mix_g, g_ln_mix_b, g_ln_mlp_g, g_ln_mlp_b = tot[0:2], tot[2:4], tot[4:6], tot[6:8]
    g_dw_b, g_cln_g, g_cln_b, g_pw2_b = (mine(tot[8 + k:9 + k]) for k in range(4))
    g_pw1_b = lax.dynamic_slice_in_dim(tot[12:14].reshape(1, 2 * D), me * 2 * ds_, 2 * ds_, axis=1)
    g_dw_w = mine(tot[14:14 + taps])
    g_rel = tot[14 + taps, :H * n_buckets].reshape(n_buckets, H)

    res = {}

    def upd(nm, parts_, w, m, v):
        shp = w.shape
        if not isinstance(parts_, list):
            parts_ = [parts_]
        parts_ = [p[None] if p.ndim == 2 else p for p in parts_]
        w3, m3, v3 = (a.reshape((len(parts_),) + parts_[0].shape[1:]) for a in (w, m, v))
        outs = adamw("adamw_" + nm, parts_, w3, m3, v3)
        res[nm] = tuple(o_.reshape(shp) for o_ in outs)

    upd("conv_pw1_w", parts["pw1"], conv_pw1_w, m_conv_pw1_w, v_conv_pw1_w)
    upd("conv_pw1_b", g_pw1_b, conv_pw1_b, m_conv_pw1_b, v_conv_pw1_b)
    upd("conv_dw_w", g_dw_w, conv_dw_w, m_conv_dw_w, v_conv_dw_w)
    upd("conv_dw_b", g_dw_b, conv_dw_b, m_conv_dw_b, v_conv_dw_b)
    upd("conv_ln_g", g_cln_g, conv_ln_g, m_conv_ln_g, v_conv_ln_g)
    upd("conv_ln_b", g_cln_b, conv_ln_b, m_conv_ln_b, v_conv_ln_b)
    upd("conv_pw2_w", parts["pw2"], conv_pw2_w, m_conv_pw2_w, v_conv_pw2_w)
    upd("conv_pw2_b", g_pw2_b, conv_pw2_b, m_conv_pw2_b, v_conv_pw2_b)
    upd("w_kv", parts["wkv"], w_kv, m_w_kv, v_w_kv)
    upd("attn_wq", parts["wq"], attn_wq, m_attn_wq, v_attn_wq)
    upd("attn_wo", parts["wo"], attn_wo, m_attn_wo, v_attn_wo)
    upd("rel_bias", g_rel, rel_bias, m_rel_bias, v_rel_bias)
    upd("mlp_w1", [parts["w1_0"], parts["w1_1"]], mlp_w1, m_mlp_w1, v_mlp_w1)
    upd("mlp_w2", [parts["w2_0"], parts["w2_1"]], mlp_w2, m_mlp_w2, v_mlp_w2)
    upd("ln_mix_g", g_ln_mix_g, ln_mix_g, m_ln_mix_g, v_ln_mix_g)
    upd("ln_mix_b", g_ln_mix_b, ln_mix_b, m_ln_mix_b, v_ln_mix_b)
    upd("ln_mlp_g", g_ln_mlp_g, ln_mlp_g, m_ln_mlp_g, v_ln_mlp_g)
    upd("ln_mlp_b", g_ln_mlp_b, ln_mlp_b, m_ln_mlp_b, v_ln_mlp_b)

    order = ["conv_pw1_w", "conv_pw1_b", "conv_dw_w", "conv_dw_b", "conv_ln_g", "conv_ln_b", "conv_pw2_w",
             "conv_pw2_b", "w_kv", "attn_wq", "attn_wo", "rel_bias", "mlp_w1", "mlp_w2", "ln_mix_g", "ln_mix_b",
             "ln_mlp_g", "ln_mlp_b"]
    return (loss, grad_x, *[res[n_][0] for n_ in order], *[res[n_][1] for n_ in order],
            *[res[n_][2] for n_ in order], *[res[n_][3] for n_ in order])
```
